```python
import math
import jax, jax.numpy as jnp
from jax import lax
import numpy as np

D_MODEL = 4096
BATCH = 4
SEQ = 2048
DEPTH = 2
DEC_BATCH = 16
DEC_SEQ = 16
PAST_LEN = 4096

CHUNK = 64
Q_BLOCK = 128
MLA_HEADS = 16
MLA_NOPE = 128
MLA_ROPE = 64
MLA_V = 128
Q_LORA = 1024
KV_LORA = 512
ROPE_THETA = 10000.0
MLA_WIDTH = MLA_HEADS * MLA_V
MLA_SCALE = 1.0 / math.sqrt(MLA_NOPE + MLA_ROPE)
POOL_WINDOWS = (2, 4, 8, 16)
POOL_GROUP = D_MODEL // 16
POOL_WIDTH = POOL_GROUP * len(POOL_WINDOWS)
POOL_HIST = max(POOL_WINDOWS) - 1
SB_HEADS = 8
SB_HEAD_DIM = 128
SB_WIDTH = SB_HEADS * SB_HEAD_DIM
SB_SCALE = 1.0 / math.sqrt(SB_HEAD_DIM)
MIX_WIDTH = MLA_WIDTH + POOL_WIDTH + SB_WIDTH
D_FF = 11008
EPS = 1e-6
IN_SPLITS = (Q_LORA, KV_LORA, MLA_ROPE, POOL_WIDTH, SB_WIDTH, SB_WIDTH, SB_WIDTH)
IN_WIDTH = sum(IN_SPLITS)

kernel_name = "hybrid_mla_pool_stickbreak_streaming_step"


def rmsnorm(x, g):
    xf = x.astype(jnp.float32)
    y = xf * lax.rsqrt(jnp.mean(xf * xf, axis=-1, keepdims=True) + EPS)
    return y.astype(x.dtype) * g


def rope(x, pos):
    half = x.shape[-1] // 2
    inv = 1.0 / (ROPE_THETA ** (jnp.arange(half, dtype=jnp.float32) / half))
    ang = pos.astype(jnp.float32)[:, None] * inv[None, :]
    shape = (pos.shape[0],) + (1,) * (x.ndim - 3) + (half,)
    cos = jnp.cos(ang).reshape(shape)
    sin = jnp.sin(ang).reshape(shape)
    xf = x.astype(jnp.float32)
    x1, x2 = xf[..., :half], xf[..., half:]
    return jnp.concatenate([x1 * cos - x2 * sin, x2 * cos + x1 * sin], axis=-1).astype(x.dtype)


def swiglu(x, w_gate, w_up, w_down):
    return (jax.nn.silu(x @ w_gate) * (x @ w_up)) @ w_down


def mla_block(q_lat, q_pe, ckv, kpe, qpos, kpos):
    s = (jnp.einsum('bthc,bsc->bhts', q_lat, ckv)
         + jnp.einsum('bthr,bsr->bhts', q_pe, kpe)).astype(jnp.float32) * MLA_SCALE
    mask = (kpos[None, :] // CHUNK) <= (qpos[:, None] // CHUNK)
    s = jnp.where(mask, s, -jnp.inf)
    p = jax.nn.softmax(s, axis=-1).astype(ckv.dtype)
    return jnp.einsum('bhts,bsc->bthc', p, ckv)


def sb_block(q, k, v, qpos, kpos):
    z = jnp.einsum('bthd,bshd->bhts', q, k).astype(jnp.float32) * SB_SCALE
    mask = kpos[None, :] < qpos[:, None]
    l = jnp.where(mask, jax.nn.log_sigmoid(-z), 0.0)
    excl = lax.cumsum(l, axis=3, reverse=True) - l
    a = jnp.where(mask, jnp.exp(jax.nn.log_sigmoid(z) + excl), 0.0).astype(v.dtype)
    return jnp.einsum('bhts,bshd->bthd', a, v)


def pool_mix(u, hist, pos, w_pool, pool_scale):
    B, T, C = u.shape
    ext = jnp.concatenate([hist, u], axis=1).astype(jnp.float32)
    cs = jnp.concatenate([jnp.zeros((B, 1, C), jnp.float32), lax.cumsum(ext, axis=1)], axis=1)
    end = cs[:, POOL_HIST + 1:POOL_HIST + 1 + T]
    groups = []
    for g, w in enumerate(POOL_WINDOWS):
        c0, c1 = g * POOL_GROUP, (g + 1) * POOL_GROUP
        win = end[..., c0:c1] - cs[:, POOL_HIST + 1 - w:POOL_HIST + 1 - w + T, c0:c1]
        cnt = jnp.minimum(pos + 1, w).astype(jnp.float32)[None, :, None]
        groups.append(win / cnt - ext[:, POOL_HIST:, c0:c1])
    pooled = jnp.stack(groups, axis=2).astype(u.dtype)
    out = jnp.einsum('btgc,gcd->btgd', pooled, w_pool).reshape(B, T, POOL_WIDTH)
    return out * pool_scale


def trunk_layer(x, ckv_past, kpe_past, pool_hist, sbk_past, sbv_past,
                g_ffn1, w1_gate, w1_up, w1_down, g_mix, w_in, g_qnorm, w_uq,
                g_kvnorm, w_uk, w_uv, w_pool, pool_scale, g_mla_out, g_sb_out,
                w_out, g_ffn2, w2_gate, w2_up, w2_down):
    B, T, _ = x.shape
    P = ckv_past.shape[1]
    kpos = jnp.arange(P + T)
    pos = kpos[P:]
    x = x + 0.5 * swiglu(rmsnorm(x, g_ffn1), w1_gate, w1_up, w1_down)
    h = rmsnorm(x, g_mix)
    proj = h @ w_in
    c_q, c_kv, k_pe, u, q_sb, k_sb, v_sb = jnp.split(
        proj, np.cumsum(IN_SPLITS)[:-1].tolist(), axis=-1)
    q = (rmsnorm(c_q, g_qnorm) @ w_uq).reshape(B, T, MLA_HEADS, MLA_NOPE + MLA_ROPE)
    q_nope = q[..., :MLA_NOPE]
    q_pe = rope(q[..., MLA_NOPE:], pos)
    q_lat = jnp.einsum('bthd,chd->bthc', q_nope, w_uk)
    ckv_new = rmsnorm(c_kv, g_kvnorm)
    kpe_new = rope(k_pe, pos)
    ckv_all = jnp.concatenate([ckv_past, ckv_new], axis=1)
    kpe_all = jnp.concatenate([kpe_past, kpe_new], axis=1)
    q_sb = q_sb.reshape(B, T, SB_HEADS, SB_HEAD_DIM)
    k_sb = k_sb.reshape(B, T, SB_HEADS, SB_HEAD_DIM)
    v_sb = v_sb.reshape(B, T, SB_HEADS, SB_HEAD_DIM)
    k_all = jnp.concatenate([sbk_past, k_sb], axis=1)
    v_all = jnp.concatenate([sbv_past, v_sb], axis=1)
    mla_outs, sb_outs = [], []
    for s0 in range(0, T, Q_BLOCK):
        s1 = min(s0 + Q_BLOCK, T)
        kend = P + s1
        qp, kp = pos[s0:s1], kpos[:kend]
        mla_outs.append(mla_block(q_lat[:, s0:s1], q_pe[:, s0:s1],
                                  ckv_all[:, :kend], kpe_all[:, :kend], qp, kp))
        sb_outs.append(sb_block(q_sb[:, s0:s1], k_all[:, :kend], v_all[:, :kend], qp, kp))
    o_lat = jnp.concatenate(mla_outs, axis=1)
    o_mla = jnp.einsum('bthc,chd->bthd', o_lat, w_uv).reshape(B, T, MLA_WIDTH)
    o_sb = jnp.concatenate(sb_outs, axis=1).reshape(B, T, SB_WIDTH)
    o_pool = pool_mix(u, pool_hist, pos, w_pool, pool_scale)
    mixed = jnp.concatenate([rmsnorm(o_mla, g_mla_out), o_pool, rmsnorm(o_sb, g_sb_out)], axis=-1)
    x = x + mixed @ w_out
    x = x + 0.5 * swiglu(rmsnorm(x, g_ffn2), w2_gate, w2_up, w2_down)
    new_hist = jnp.concatenate([pool_hist, u], axis=1)[:, -POOL_HIST:]
    return x, ckv_new, kpe_new, new_hist, k_sb, v_sb


def setup_inputs(seed: int = 0) -> dict:
    key = jax.random.key(seed)
    ks = iter(jax.random.split(key, 40))
    f32 = jnp.float32

    def nrm(shape, scale=1.0):
        return jax.random.normal(next(ks), shape, f32) * scale

    def gain(shape):
        return 1.0 + 0.02 * jax.random.normal(next(ks), shape, f32)

    L = DEPTH
    return {
        "x_prompt": nrm((BATCH, SEQ, D_MODEL)),
        "x_sample": nrm((DEC_BATCH, DEC_SEQ, D_MODEL)),
        "cache_ckv": nrm((L, DEC_BATCH, PAST_LEN, KV_LORA)),
        "cache_kpe": nrm((L, DEC_BATCH, PAST_LEN, MLA_ROPE)),
        "state_pool": nrm((L, DEC_BATCH, POOL_HIST, POOL_WIDTH)),
        "cache_sb_k": nrm((L, DEC_BATCH, PAST_LEN, SB_HEADS, SB_HEAD_DIM)),
        "cache_sb_v": nrm((L, DEC_BATCH, PAST_LEN, SB_HEADS, SB_HEAD_DIM)),
        "g_ffn1": gain((L, D_MODEL)),
        "w1_gate": nrm((L, D_MODEL, D_FF), D_MODEL ** -0.5),
        "w1_up": nrm((L, D_MODEL, D_FF), D_MODEL ** -0.5),
        "w1_down": nrm((L, D_FF, D_MODEL), D_FF ** -0.5),
        "g_mix": gain((L, D_MODEL)),
        "w_in": nrm((L, D_MODEL, IN_WIDTH), D_MODEL ** -0.5),
        "g_qnorm": gain((L, Q_LORA)),
        "w_uq": nrm((L, Q_LORA, MLA_HEADS * (MLA_NOPE + MLA_ROPE)), Q_LORA ** -0.5),
        "g_kvnorm": gain((L, KV_LORA)),
        "w_uk": nrm((L, KV_LORA, MLA_HEADS, MLA_NOPE), KV_LORA ** -0.5),
        "w_uv": nrm((L, KV_LORA, MLA_HEADS, MLA_V), KV_LORA ** -0.5),
        "w_pool": nrm((L, len(POOL_WINDOWS), POOL_GROUP, POOL_GROUP), POOL_GROUP ** -0.5),
        "pool_scale": gain((L, POOL_WIDTH)),
        "g_mla_out": gain((L, MLA_WIDTH)),
        "g_sb_out": gain((L, SB_WIDTH)),
        "w_out": nrm((L, MIX_WIDTH, D_MODEL), MIX_WIDTH ** -0.5),
        "g_ffn2": gain((L, D_MODEL)),
        "w2_gate": nrm((L, D_MODEL, D_FF), D_MODEL ** -0.5),
        "w2_up": nrm((L, D_MODEL, D_FF), D_MODEL ** -0.5),
        "w2_down": nrm((L, D_FF, D_MODEL), D_FF ** -0.5),
        "g_final": gain((D_MODEL,)),
    }


def reference(x_prompt, x_sample, cache_ckv, cache_kpe, state_pool, cache_sb_k, cache_sb_v,
              g_ffn1, w1_gate, w1_up, w1_down, g_mix, w_in, g_qnorm, w_uq, g_kvnorm,
              w_uk, w_uv, w_pool, pool_scale, g_mla_out, g_sb_out, w_out, g_ffn2,
              w2_gate, w2_up, w2_down, g_final):
    B = x_prompt.shape[0]
    dt = x_prompt.dtype
    e_ckv = jnp.zeros((B, 0, KV_LORA), dt)
    e_kpe = jnp.zeros((B, 0, MLA_ROPE), dt)
    e_pool = jnp.zeros((B, POOL_HIST, POOL_WIDTH), dt)
    e_sb = jnp.zeros((B, 0, SB_HEADS, SB_HEAD_DIM), dt)
    hp, hs = x_prompt, x_sample
    p_ckv, p_kpe, p_pool, p_sbk, p_sbv = [], [], [], [], []
    s_ckv, s_kpe, s_pool, s_sbk, s_sbv = [], [], [], [], []
    for l in range(DEPTH):
        lw = (g_ffn1[l], w1_gate[l], w1_up[l], w1_down[l], g_mix[l], w_in[l], g_qnorm[l],
              w_uq[l], g_kvnorm[l], w_uk[l], w_uv[l], w_pool[l], pool_scale[l],
              g_mla_out[l], g_sb_out[l], w_out[l], g_ffn2[l], w2_gate[l], w2_up[l], w2_down[l])
        hp, a, b, c, d, e = trunk_layer(hp, e_ckv, e_kpe, e_pool, e_sb, e_sb, *lw)
        p_ckv.append(a); p_kpe.append(b); p_pool.append(c); p_sbk.append(d); p_sbv.append(e)
        hs, a, b, c, d, e = trunk_layer(hs, cache_ckv[l], cache_kpe[l], state_pool[l],
                                        cache_sb_k[l], cache_sb_v[l], *lw)
        s_ckv.append(a); s_kpe.append(b); s_pool.append(c); s_sbk.append(d); s_sbv.append(e)
    y_prompt = rmsnorm(hp, g_final)
    y_sample = rmsnorm(hs, g_final)
    return (y_prompt, y_sample,
            jnp.stack(p_ckv), jnp.stack(p_kpe), jnp.stack(p_pool), jnp.stack(p_sbk), jnp.stack(p_sbv),
            jnp.stack(s_ckv), jnp.stack(s_kpe), jnp.stack(s_pool), jnp.stack(s_sbk), jnp.stack(s_sbv))
```

```python
import functools
import math

import jax
import jax.numpy as jnp
from jax import lax
from jax.experimental import pallas as pl
from jax.experimental.pallas import tpu as pltpu

F32 = jnp.float32
BF16 = jnp.bfloat16

D_MODEL = 4096
D_FF = 11008
DEPTH = 2
P_BATCH, P_SEQ = 4, 2048
S_BATCH, S_SEQ, PAST_LEN = 16, 16, 4096
CHUNK = 64
MLA_HEADS, MLA_NOPE, MLA_ROPE, MLA_V = 16, 128, 64, 128
Q_LORA, KV_LORA = 1024, 512
ROPE_THETA = 10000.0
MLA_SCALE = 1.0 / math.sqrt(MLA_NOPE + MLA_ROPE)
POOL_WINDOWS = (2, 4, 8, 16)
POOL_GROUP = 256
POOL_WIDTH = 1024
POOL_HIST = 15
SB_HEADS, SB_DIM = 8, 128
SB_WIDTH = 1024
SB_SCALE = 1.0 / math.sqrt(SB_DIM)
EPS = 1e-6

V7X_LANES = 128
V7X_VMEM_LIMIT_BYTES = 60000 * 1024
VALUE_SPILL_BYTES = 8 * 1024 * 1024

PROJ_CQ = 0
PROJ_U = 1024
PROJ_SQ = 2048
PROJ_SK = 3072
PROJ_SV = 4096
PROJ_CKV = 5120
PROJ_KPE = 5632
PROJ_W = 5760
KPE_PAD = 128
QK_PAD = 256

ROW_CHUNK = 16
NEG_BIG = -1e30


def _cparams(semantics, vmem_bytes):
    assert vmem_bytes <= V7X_VMEM_LIMIT_BYTES, vmem_bytes
    limit = min(V7X_VMEM_LIMIT_BYTES, vmem_bytes + VALUE_SPILL_BYTES)
    return pltpu.CompilerParams(dimension_semantics=semantics, vmem_limit_bytes=int(limit))


def _rms(x, g):
    ms = jnp.mean(x * x, axis=-1, keepdims=True)
    return (x * lax.rsqrt(ms + EPS)) * g


def _dot(a, b):
    return jnp.dot(a, b, preferred_element_type=F32)


def _dot_nt(a, b):
    return lax.dot_general(a, b, (((1,), (1,)), ((), ())), preferred_element_type=F32)


def _row_loop(rows, fn):
    def body(c, carry):
        fn(pl.multiple_of(c * ROW_CHUNK, ROW_CHUNK))
        return carry
    lax.fori_loop(0, rows // ROW_CHUNK, body, 0)


def _ffn_kernel(x_ref, g_ref, wg_ref, wu_ref, wd_ref, gf_ref, o_ref, xn_ref, *, nj, rows, final_norm):
    j = pl.program_id(1)

    @pl.when(j == 0)
    def _():
        def prep(r):
            x = x_ref[pl.ds(r, ROW_CHUNK), :]
            xn_ref[pl.ds(r, ROW_CHUNK), :] = _rms(x, g_ref[...]).astype(BF16)
            o_ref[pl.ds(r, ROW_CHUNK), :] = jnp.zeros((ROW_CHUNK, x.shape[1]), F32)
        _row_loop(rows, prep)

    xn = xn_ref[...]
    gate = _dot(xn, wg_ref[...])
    up = _dot(xn, wu_ref[...])
    h = (gate * (1.0 / (1.0 + jnp.exp(-gate))) * up).astype(BF16)
    o_ref[...] += _dot(h, wd_ref[...])

    @pl.when(j == nj - 1)
    def _():
        def fin(r):
            y = x_ref[pl.ds(r, ROW_CHUNK), :] + 0.5 * o_ref[pl.ds(r, ROW_CHUNK), :]
            if final_norm:
                y = _rms(y, gf_ref[...])
            o_ref[pl.ds(r, ROW_CHUNK), :] = y
        _row_loop(rows, fin)


def _ffn(x, g, wg, wu, wd, g_final, *, tm, tf, final_norm):
    m, d = x.shape
    f = wg.shape[1]
    assert m % tm == 0 and f % tf == 0 and tm % ROW_CHUNK == 0
    nj = f // tf
    vmem = (2 * tm * d * 4
            + tm * d * 2
            + 2 * tm * d * 4
            + 2 * 3 * d * tf * 2
            + tm * d * 4
            + 4 * tm * tf * 4)
    return pl.pallas_call(
        functools.partial(_ffn_kernel, nj=nj, rows=tm, final_norm=final_norm),
        out_shape=jax.ShapeDtypeStruct((m, d), F32),
        grid=(m // tm, nj),
        in_specs=[
            pl.BlockSpec((tm, d), lambda i, j: (i, 0), pipeline_mode=pl.Buffered(1)),
            pl.BlockSpec((1, d), lambda i, j: (0, 0)),
            pl.BlockSpec((d, tf), lambda i, j: (0, j)),
            pl.BlockSpec((d, tf), lambda i, j: (0, j)),
            pl.BlockSpec((tf, d), lambda i, j: (j, 0)),
            pl.BlockSpec((1, d), lambda i, j: (0, 0)),
        ],
        out_specs=pl.BlockSpec((tm, d), lambda i, j: (i, 0)),
        scratch_shapes=[pltpu.VMEM((tm, d), BF16)],
        compiler_params=_cparams(("parallel", "arbitrary"), vmem),
        name="ffn",
    )(x, g, wg, wu, wd, g_final)


def _inproj_kernel(x_ref, g_ref, w_ref, o_ref, xn_ref, *, rows):
    @pl.when(pl.program_id(1) == 0)
    def _():
        def prep(r):
            xn_ref[pl.ds(r, ROW_CHUNK), :] = _rms(x_ref[pl.ds(r, ROW_CHUNK), :], g_ref[...]).astype(BF16)
        _row_loop(rows, prep)

    o_ref[...] = _dot(xn_ref[...], w_ref[...])


def _inproj(x, g, w, *, tm, tn):
    m, d = x.shape
    n = w.shape[1]
    assert m % tm == 0 and n % tn == 0
    vmem = 2 * tm * d * 4 + tm * d * 2 + 2 * d * tn * 2 + 3 * tm * tn * 4
    return pl.pallas_call(
        functools.partial(_inproj_kernel, rows=tm),
        out_shape=jax.ShapeDtypeStruct((m, n), F32),
        grid=(m // tm, n // tn),
        in_specs=[
            pl.BlockSpec((tm, d), lambda i, j: (i, 0), pipeline_mode=pl.Buffered(1)),
            pl.BlockSpec((1, d), lambda i, j: (0, 0)),
            pl.BlockSpec((d, tn), lambda i, j: (0, j)),
        ],
        out_specs=pl.BlockSpec((tm, tn), lambda i, j: (i, j)),
        scratch_shapes=[pltpu.VMEM((tm, d), BF16)],
        compiler_params=_cparams(("parallel", "arbitrary"), vmem),
        name="inproj",
    )(x, g, w)


def _rope_tile(t, c, sa, sb):
    return t * c + pltpu.roll(t, KPE_PAD - MLA_ROPE // 2, 1) * sa + pltpu.roll(t, MLA_ROPE // 2, 1) * sb


def _mla_prep_kernel(cq_ref, ckv_ref, kpe_ref, c_ref, sa_ref, sb_ref, gq_ref, gkv_ref,
                     wuq_ref, wuk_ref, wuv_ref,
                     ckv_o, kpe_o, q_o, k_o, v_o, cqn_s, ckv_s, kpe_s, *, rows, hb):
    @pl.when(pl.program_id(1) == 0)
    def _():
        def prep(r):
            sl = pl.ds(r, ROW_CHUNK)
            cqn_s[sl, :] = _rms(cq_ref[sl, :], gq_ref[...]).astype(BF16)
            ckv = _rms(ckv_ref[sl, :], gkv_ref[...])
            ckv_o[sl, :] = ckv
            ckv_s[sl, :] = ckv.astype(BF16)
            kpe = _rope_tile(kpe_ref[sl, :], c_ref[sl, :], sa_ref[sl, :], sb_ref[sl, :])
            kpe_o[sl, :] = kpe
            kpe_s[sl, :] = kpe.astype(BF16)
        _row_loop(rows, prep)

    for hh in range(hb):
        qf = _dot(cqn_s[...], wuq_ref[hh]) * MLA_SCALE
        q_o[hh, :, pl.ds(0, MLA_NOPE)] = qf[:, :MLA_NOPE].astype(BF16)
        q_o[hh, :, pl.ds(MLA_NOPE, KPE_PAD)] = _rope_tile(
            qf[:, MLA_NOPE:], c_ref[...], sa_ref[...], sb_ref[...]).astype(BF16)
        k_o[hh, :, pl.ds(0, MLA_NOPE)] = _dot(ckv_s[...], wuk_ref[hh]).astype(BF16)
        k_o[hh, :, pl.ds(MLA_NOPE, KPE_PAD)] = kpe_s[...]
        v_o[hh] = _dot(ckv_s[...], wuv_ref[hh]).astype(BF16)


def _mla_prep(proj, rope_c, rope_sa, rope_sb, gq, gkv, wuq, wuk, wuv, *, tm, hb):
    m = proj.shape[0]
    nh = wuq.shape[0]
    assert m % tm == 0 and nh % hb == 0
    vmem = (2 * tm * (Q_LORA + KV_LORA + 4 * KPE_PAD) * 4
            + 2 * hb * (Q_LORA * QK_PAD + 2 * KV_LORA * MLA_NOPE) * 2
            + 2 * tm * (KV_LORA + KPE_PAD) * 4
            + 2 * hb * tm * (2 * QK_PAD + MLA_V) * 2
            + tm * (Q_LORA + KV_LORA + KPE_PAD) * 2
            + 6 * tm * QK_PAD * 4)
    row_blk = lambda w, c: pl.BlockSpec((tm, w), lambda i, h: (i, c))
    head_w = lambda k, n: pl.BlockSpec((hb, k, n), lambda i, h: (h, 0, 0))
    head_o = lambda n: pl.BlockSpec((hb, tm, n), lambda i, h: (h, i, 0))
    return pl.pallas_call(
        functools.partial(_mla_prep_kernel, rows=tm, hb=hb),
        out_shape=(
            jax.ShapeDtypeStruct((m, KV_LORA), F32),
            jax.ShapeDtypeStruct((m, KPE_PAD), F32),
            jax.ShapeDtypeStruct((nh, m, QK_PAD), BF16),
            jax.ShapeDtypeStruct((nh, m, QK_PAD), BF16),
            jax.ShapeDtypeStruct((nh, m, MLA_V), BF16),
        ),
        grid=(m // tm, nh // hb),
        in_specs=[
            row_blk(Q_LORA, PROJ_CQ // Q_LORA),
            row_blk(KV_LORA, PROJ_CKV // KV_LORA),
            row_blk(KPE_PAD, PROJ_KPE // KPE_PAD),
            row_blk(KPE_PAD, 0), row_blk(KPE_PAD, 0), row_blk(KPE_PAD, 0),
            pl.BlockSpec((1, Q_LORA), lambda i, h: (0, 0)),
            pl.BlockSpec((1, KV_LORA), lambda i, h: (0, 0)),
            head_w(Q_LORA, QK_PAD), head_w(KV_LORA, MLA_NOPE), head_w(KV_LORA, MLA_V),
        ],
        out_specs=(row_blk(KV_LORA, 0), row_blk(KPE_PAD, 0), head_o(QK_PAD), head_o(QK_PAD), head_o(MLA_V)),
        scratch_shapes=[pltpu.VMEM((tm, Q_LORA), BF16), pltpu.VMEM((tm, KV_LORA), BF16),
                        pltpu.VMEM((tm, KPE_PAD), BF16)],
        compiler_params=_cparams(("parallel", "arbitrary"), vmem),
        name="mla_prep",
    )(proj, proj, proj, rope_c, rope_sa, rope_sb, gq, gkv, wuq, wuk, wuv)


def _softmax_step(s, v, m, l, acc):
    m_new = jnp.maximum(m, jnp.max(s, axis=-1, keepdims=True))
    p = jnp.exp(s - m_new)
    alpha = jnp.exp(m - m_new)
    l = alpha * l + jnp.sum(p, axis=-1, keepdims=True)
    acc = alpha * acc + _dot(p.astype(BF16), v)
    return m_new, l, acc


def _mla_prompt_kernel(q_ref, k_ref, v_ref, o_ref, *, seq, tile):
    def q_tile(qi, carry):
        q0 = pl.multiple_of(qi * tile, tile)
        q = q_ref[0, pl.ds(q0, tile), :]

        def k_tile(ki, st):
            k0 = pl.multiple_of(ki * tile, tile)
            s = _dot_nt(q, k_ref[0, pl.ds(k0, tile), :])
            return _softmax_step(s, v_ref[0, pl.ds(k0, tile), :], *st)

        init = (jnp.full((tile, 1), NEG_BIG, F32), jnp.zeros((tile, 1), F32),
                jnp.zeros((tile, MLA_V), F32))
        st = lax.fori_loop(0, qi, k_tile, init)
        s = _dot_nt(q, k_ref[0, pl.ds(q0, tile), :])
        row = lax.broadcasted_iota(jnp.int32, (tile, tile), 0)
        col = lax.broadcasted_iota(jnp.int32, (tile, tile), 1)
        s = jnp.where(col // CHUNK <= row // CHUNK, s, NEG_BIG)
        _, l, acc = _softmax_step(s, v_ref[0, pl.ds(q0, tile), :], *st)
        o_ref[pl.ds(q0, tile), :] = acc / l
        return carry

    lax.fori_loop(0, seq // tile, q_tile, 0)


def _mla_prompt(q, k, v, *, batch, seq, tile):
    nh = q.shape[0]
    assert seq % tile == 0 and tile % CHUNK == 0
    vmem = 2 * seq * (2 * QK_PAD + MLA_V) * 2 + 2 * seq * MLA_V * 4 + 8 * tile * tile * 4
    return pl.pallas_call(
        functools.partial(_mla_prompt_kernel, seq=seq, tile=tile),
        out_shape=jax.ShapeDtypeStruct((batch * seq, nh * MLA_V), F32),
        grid=(batch, nh),
        in_specs=[
            pl.BlockSpec((1, seq, QK_PAD), lambda b, h: (h, b, 0)),
            pl.BlockSpec((1, seq, QK_PAD), lambda b, h: (h, b, 0)),
            pl.BlockSpec((1, seq, MLA_V), lambda b, h: (h, b, 0)),
        ],
        out_specs=pl.BlockSpec((seq, MLA_V), lambda b, h: (b, h)),
        compiler_params=_cparams(("parallel", "parallel"), vmem),
        name="mla_prompt",
    )(q, k, v)


def _mla_sample_kernel(q_ref, wuk_ref, wuv_ref, ckv_ref, kpe_ref, ckvn_ref, kpen_ref, o_ref,
                       qlat_s, qpe_s, m_s, l_s, acc_s, *, nkt, nh, tq):
    kt = pl.program_id(1)
    rows = nh * tq

    @pl.when(kt == 0)
    def _():
        for h in range(nh):
            qh = q_ref[h]
            qlat_s[pl.ds(h * tq, tq), :] = _dot_nt(qh[:, :MLA_NOPE], wuk_ref[h]).astype(BF16)
            qpe_s[pl.ds(h * tq, tq), :] = qh[:, MLA_NOPE:MLA_NOPE + KPE_PAD]
        m_s[...] = jnp.full((rows, 1), NEG_BIG, F32)
        l_s[...] = jnp.zeros((rows, 1), F32)
        acc_s[...] = jnp.zeros((rows, KV_LORA), F32)

    def update(ckv, kpe):
        s = _dot_nt(qlat_s[...], ckv) + _dot_nt(qpe_s[:, pl.ds(0, MLA_ROPE)], kpe)
        m, l, acc = _softmax_step(s, ckv, m_s[...], l_s[...], acc_s[...])
        m_s[...] = m
        l_s[...] = l
        acc_s[...] = acc

    update(ckv_ref[0, 0].astype(BF16), kpe_ref[0, 0].astype(BF16))

    @pl.when(kt == nkt - 1)
    def _():
        update(ckvn_ref[...].astype(BF16), kpen_ref[:, pl.ds(0, MLA_ROPE)].astype(BF16))
        o_lat = (acc_s[...] / l_s[...]).astype(BF16)
        for h in range(nh):
            o_ref[:, pl.ds(h * MLA_V, MLA_V)] = _dot(o_lat[h * tq:(h + 1) * tq], wuv_ref[h])


def _mla_sample(q, wuk, wuv, cache_ckv, cache_kpe, ckv_new, kpe_new, *, layer, row0, tk):
    nh = q.shape[0]
    _, batch, past, _ = cache_ckv.shape
    tq = S_SEQ
    assert past % tk == 0 and row0 % tq == 0
    assert (PAST_LEN + S_SEQ - 1) // CHUNK == PAST_LEN // CHUNK
    nkt = past // tk
    rows = nh * tq
    vmem = (2 * nh * tq * QK_PAD * 2 + 4 * nh * KV_LORA * MLA_NOPE * 2
            + 2 * tk * (KV_LORA + KPE_PAD) * 4 + tk * (KV_LORA + KPE_PAD) * 2
            + 2 * tq * (KV_LORA + KPE_PAD) * 4 + 2 * tq * nh * MLA_V * 4
            + rows * (KV_LORA + KPE_PAD) * 2 + rows * (KV_LORA + 2 * V7X_LANES) * 4
            + 4 * rows * tk * 4)
    rb = row0 // tq
    return pl.pallas_call(
        functools.partial(_mla_sample_kernel, nkt=nkt, nh=nh, tq=tq),
        out_shape=jax.ShapeDtypeStruct((batch * tq, nh * MLA_V), F32),
        grid=(batch, nkt),
        in_specs=[
            pl.BlockSpec((nh, tq, QK_PAD), lambda b, t: (0, rb + b, 0)),
            pl.BlockSpec((nh, KV_LORA, MLA_NOPE), lambda b, t: (0, 0, 0)),
            pl.BlockSpec((nh, KV_LORA, MLA_V), lambda b, t: (0, 0, 0)),
            pl.BlockSpec((1, 1, tk, KV_LORA), lambda b, t: (layer, b, t, 0)),
            pl.BlockSpec((1, 1, tk, MLA_ROPE), lambda b, t: (layer, b, t, 0)),
            pl.BlockSpec((tq, KV_LORA), lambda b, t: (rb + b, 0)),
            pl.BlockSpec((tq, KPE_PAD), lambda b, t: (rb + b, 0)),
        ],
        out_specs=pl.BlockSpec((tq, nh * MLA_V), lambda b, t: (b, 0)),
        scratch_shapes=[pltpu.VMEM((rows, KV_LORA), BF16), pltpu.VMEM((rows, KPE_PAD), BF16),
                        pltpu.VMEM((rows, 1), F32), pltpu.VMEM((rows, 1), F32),
                        pltpu.VMEM((rows, KV_LORA), F32)],
        compiler_params=_cparams(("parallel", "arbitrary"), vmem),
        name="mla_sample",
    )(q, wuk, wuv, cache_ckv, cache_kpe, ckv_new, kpe_new)


def _softplus(z):
    return jnp.maximum(z, 0.0) + jnp.log1p(jnp.exp(-jnp.abs(z)))


def _later_keys_matrix(n):
    row = lax.broadcasted_iota(jnp.int32, (n, n), 0)
    col = lax.broadcasted_iota(jnp.int32, (n, n), 1)
    return (row > col).astype(BF16)


def _later_sum(l, u):
    r = l.shape[0]
    hi = l.astype(BF16)
    lo = (l - hi.astype(F32)).astype(BF16)
    e = _dot(jnp.concatenate([hi, lo], axis=0), u)
    return e[:r] + e[r:]


def _sb_tile(z, v, u, later, acc, mask=None):
    sp = _softplus(z)
    l = -sp if mask is None else jnp.where(mask, -sp, 0.0)
    a = jnp.exp(z - sp + _later_sum(l, u) + later)
    if mask is not None:
        a = jnp.where(mask, a, 0.0)
    acc = acc + _dot(a.astype(BF16), v)
    return later + jnp.sum(l, axis=-1, keepdims=True), acc


def _sb_prompt_kernel(q_ref, k_ref, v_ref, o_ref, qb, kb, vb, *, seq, tile):
    def cast(r):
        sl = pl.ds(r, ROW_CHUNK)
        qb[sl, :] = q_ref[sl, :].astype(BF16)
        kb[sl, :] = k_ref[sl, :].astype(BF16)
        vb[sl, :] = v_ref[sl, :].astype(BF16)
    _row_loop(seq, cast)

    u = _later_keys_matrix(tile)

    def q_tile(qi, carry):
        q0 = pl.multiple_of(qi * tile, tile)
        q = qb[pl.ds(q0, tile), :]
        row = lax.broadcasted_iota(jnp.int32, (tile, tile), 0)
        col = lax.broadcasted_iota(jnp.int32, (tile, tile), 1)
        z = _dot_nt(q, kb[pl.ds(q0, tile), :]) * SB_SCALE
        st = _sb_tile(z, vb[pl.ds(q0, tile), :], u, jnp.zeros((tile, 1), F32),
                      jnp.zeros((tile, SB_DIM), F32), mask=col < row)

        def k_tile(step, st):
            k0 = pl.multiple_of((qi - 1 - step) * tile, tile)
            z = _dot_nt(q, kb[pl.ds(k0, tile), :]) * SB_SCALE
            return _sb_tile(z, vb[pl.ds(k0, tile), :], u, *st)

        _, acc = lax.fori_loop(0, qi, k_tile, st)
        o_ref[pl.ds(q0, tile), :] = acc
        return carry

    lax.fori_loop(0, seq // tile, q_tile, 0)


def _sb_prompt(proj, *, batch, seq, tile):
    assert seq % tile == 0 and seq % ROW_CHUNK == 0
    vmem = 2 * 3 * seq * SB_DIM * 4 + 2 * seq * SB_DIM * 4 + 3 * seq * SB_DIM * 2 + 12 * tile * tile * 4
    col = lambda c0: pl.BlockSpec((seq, SB_DIM), lambda b, h: (b, c0 // SB_DIM + h))
    return pl.pallas_call(
        functools.partial(_sb_prompt_kernel, seq=seq, tile=tile),
        out_shape=jax.ShapeDtypeStruct((batch * seq, SB_WIDTH), F32),
        grid=(batch, SB_HEADS),
        in_specs=[col(PROJ_SQ), col(PROJ_SK), col(PROJ_SV)],
        out_specs=pl.BlockSpec((seq, SB_DIM), lambda b, h: (b, h)),
        scratch_shapes=[pltpu.VMEM((seq, SB_DIM), BF16)] * 3,
        compiler_params=_cparams(("parallel", "parallel"), vmem),
        name="sb_prompt",
    )(proj, proj, proj)


def _sb_sample_kernel(q_ref, kn_ref, vn_ref, kc_ref, vc_ref, o_ref, later_s, acc_s, *, nkt, sub, tq):
    step = pl.program_id(1)
    rows = SB_HEADS * tq
    head = lambda h: pl.ds(h * SB_DIM, SB_DIM)
    hrow = lambda h: pl.ds(h * tq, tq)

    def scores(k_of_head):
        return jnp.concatenate(
            [_dot_nt(q_ref[:, head(h)].astype(BF16), k_of_head(h)) for h in range(SB_HEADS)], axis=0) * SB_SCALE

    def weighted(a, v_of_head):
        return jnp.concatenate(
            [_dot(a[h * tq:(h + 1) * tq], v_of_head(h)) for h in range(SB_HEADS)], axis=0)

    @pl.when(step == 0)
    def _():
        pad = jnp.zeros((V7X_LANES - tq, SB_DIM), BF16)
        z = scores(lambda h: jnp.concatenate([kn_ref[:, head(h)].astype(BF16), pad], axis=0))
        t = lax.broadcasted_iota(jnp.int32, (rows, V7X_LANES), 0) % tq
        s = lax.broadcasted_iota(jnp.int32, (rows, V7X_LANES), 1)
        mask = s < t
        sp = _softplus(z)
        l = jnp.where(mask, -sp, 0.0)
        a = jnp.where(mask, jnp.exp(z - sp + _later_sum(l, _later_keys_matrix(V7X_LANES))), 0.0).astype(BF16)
        acc_s[...] = weighted(a, lambda h: jnp.concatenate([vn_ref[:, head(h)].astype(BF16), pad], axis=0))
        later_s[...] = jnp.sum(l, axis=-1, keepdims=True)

    @pl.when(step > 0)
    def _():
        u = _later_keys_matrix(sub)
        later = later_s[...]
        acc = acc_s[...]
        for c in reversed(range(kc_ref.shape[2] // sub)):
            keys = pl.ds(c * sub, sub)
            z = scores(lambda h: kc_ref[0, 0, keys, head(h)].astype(BF16))
            sp = _softplus(z)
            l = -sp
            a = jnp.exp(z - sp + _later_sum(l, u) + later).astype(BF16)
            acc = acc + weighted(a, lambda h: vc_ref[0, 0, keys, head(h)].astype(BF16))
            later = later + jnp.sum(l, axis=-1, keepdims=True)
        later_s[...] = later
        acc_s[...] = acc

    @pl.when(step == nkt)
    def _():
        for h in range(SB_HEADS):
            o_ref[:, head(h)] = acc_s[hrow(h), :]


def _sb_sample(proj, cache_k, cache_v, *, layer, tk, sub):
    _, batch, past, _ = cache_k.shape
    tq = S_SEQ
    assert past % tk == 0 and tk % sub == 0
    nkt = past // tk
    rows = SB_HEADS * tq
    vmem = (2 * 3 * tq * SB_WIDTH * 4 + 2 * 2 * tk * SB_WIDTH * 4 + 2 * tq * SB_WIDTH * 4
            + rows * (SB_DIM + V7X_LANES) * 4 + 2 * sub * SB_WIDTH * 2 + 16 * rows * sub * 4 + sub * sub * 2)
    newest_first = lambda b, s: (layer, b, nkt - jnp.maximum(s, 1), 0)
    col = lambda c0: pl.BlockSpec((tq, SB_WIDTH), lambda b, s: (b, c0 // SB_WIDTH))
    return pl.pallas_call(
        functools.partial(_sb_sample_kernel, nkt=nkt, sub=sub, tq=tq),
        out_shape=jax.ShapeDtypeStruct((batch * tq, SB_WIDTH), F32),
        grid=(batch, nkt + 1),
        in_specs=[col(PROJ_SQ), col(PROJ_SK), col(PROJ_SV),
                  pl.BlockSpec((1, 1, tk, SB_WIDTH), newest_first),
                  pl.BlockSpec((1, 1, tk, SB_WIDTH), newest_first)],
        out_specs=pl.BlockSpec((tq, SB_WIDTH), lambda b, s: (b, 0)),
        scratch_shapes=[pltpu.VMEM((rows, 1), F32), pltpu.VMEM((rows, SB_DIM), F32)],
        compiler_params=_cparams(("parallel", "arbitrary"), vmem),
        name="sb_sample",
    )(proj, proj, proj, cache_k, cache_v)


def _pool_kernel(u_ref, hist_ref, w_ref, scale_ref, o_ref, carry_s, *, tt, pos0):
    ti = pl.program_id(1)
    halo = carry_s.shape[0]

    @pl.when(ti == 0)
    def _():
        carry_s[...] = hist_ref[0]

    pos = pos0 + ti * tt + lax.broadcasted_iota(jnp.int32, (tt, 1), 0)
    for g, w in enumerate(POOL_WINDOWS):
        cols = pl.ds(g * POOL_GROUP, POOL_GROUP)
        tok = u_ref[:, cols]
        s = jnp.concatenate([carry_s[:, cols], tok], axis=0)
        shift = 1
        while shift < w:
            s = s + pltpu.roll(s, shift, 0)
            shift *= 2
        cnt = jnp.minimum(pos + 1, w).astype(F32)
        pooled = s[halo:] / cnt - tok
        o_ref[:, cols] = (_dot(pooled.astype(BF16), w_ref[g]) * scale_ref[:, cols]).astype(BF16)
    carry_s[...] = u_ref[pl.ds(tt - halo, halo), :]


def _pool(proj, hist, w_pool, scale, *, batch, seq, tt, pos0):
    halo = hist.shape[1]
    assert seq % tt == 0 and tt >= halo and halo > max(POOL_WINDOWS) - 1
    nt = seq // tt
    vmem = (2 * tt * POOL_WIDTH * 4 + 2 * halo * POOL_WIDTH * 4 + 2 * POOL_WIDTH * POOL_GROUP * 2
            + 2 * tt * POOL_WIDTH * 2 + halo * POOL_WIDTH * 4 + 8 * (tt + halo) * POOL_GROUP * 4)
    return pl.pallas_call(
        functools.partial(_pool_kernel, tt=tt, pos0=pos0),
        out_shape=jax.ShapeDtypeStruct((batch * seq, POOL_WIDTH), BF16),
        grid=(batch, nt),
        in_specs=[
            pl.BlockSpec((tt, POOL_WIDTH), lambda b, t: (b * nt + t, PROJ_U // POOL_WIDTH)),
            pl.BlockSpec((1, halo, POOL_WIDTH), lambda b, t: (b, 0, 0)),
            pl.BlockSpec((len(POOL_WINDOWS), POOL_GROUP, POOL_GROUP), lambda b, t: (0, 0, 0)),
            pl.BlockSpec((1, POOL_WIDTH), lambda b, t: (0, 0)),
        ],
        out_specs=pl.BlockSpec((tt, POOL_WIDTH), lambda b, t: (b * nt + t, 0)),
        scratch_shapes=[pltpu.VMEM((halo, POOL_WIDTH), F32)],
        compiler_params=_cparams(("parallel", "arbitrary"), vmem),
        name="pool",
    )(proj, hist, w_pool, scale)


def _outproj_kernel(mla_ref, pool_ref, sb_ref, x_ref, gm_ref, gs_ref, w_ref, o_ref, mix_s, *, rows):
    wm = mla_ref.shape[1]
    wp = pool_ref.shape[1]

    @pl.when(pl.program_id(1) == 0)
    def _():
        def prep(r):
            sl = pl.ds(r, ROW_CHUNK)
            mix_s[sl, pl.ds(0, wm)] = _rms(mla_ref[sl, :], gm_ref[...]).astype(BF16)
            mix_s[sl, pl.ds(wm, wp)] = pool_ref[sl, :]
            mix_s[sl, pl.ds(wm + wp, sb_ref.shape[1])] = _rms(sb_ref[sl, :], gs_ref[...]).astype(BF16)
        _row_loop(rows, prep)

    o_ref[...] = x_ref[...] + _dot(mix_s[...], w_ref[...])


def _outproj(o_mla, o_pool, o_sb, x, g_mla, g_sb, w, *, tm, tn):
    m, d = x.shape
    wm, wp, ws = o_mla.shape[1], o_pool.shape[1], o_sb.shape[1]
    k = wm + wp + ws
    assert m % tm == 0 and d % tn == 0 and w.shape == (k, d)
    vmem = (2 * tm * (wm + ws) * 4 + 2 * tm * wp * 2 + 2 * 2 * tm * tn * 4 + 2 * k * tn * 2
            + tm * k * 2 + 2 * tm * tn * 4)
    full = lambda n: pl.BlockSpec((tm, n), lambda i, j: (i, 0))
    return pl.pallas_call(
        functools.partial(_outproj_kernel, rows=tm),
        out_shape=jax.ShapeDtypeStruct((m, d), F32),
        grid=(m // tm, d // tn),
        in_specs=[full(wm), full(wp), full(ws),
                  pl.BlockSpec((tm, tn), lambda i, j: (i, j)),
                  pl.BlockSpec((1, wm), lambda i, j: (0, 0)),
                  pl.BlockSpec((1, ws), lambda i, j: (0, 0)),
                  pl.BlockSpec((k, tn), lambda i, j: (0, j))],
        out_specs=pl.BlockSpec((tm, tn), lambda i, j: (i, j)),
        scratch_shapes=[pltpu.VMEM((tm, k), BF16)],
        compiler_params=_cparams(("parallel", "arbitrary"), vmem),
        name="outproj",
    )(o_mla, o_pool, o_sb, x, g_mla, g_sb, w)


def _rope_tables(pos):
    half = MLA_ROPE // 2
    inv = 1.0 / (ROPE_THETA ** (jnp.arange(half, dtype=F32) / half))
    ang = pos.astype(F32)[:, None] * inv[None, :]
    cos, sin = jnp.cos(ang), jnp.sin(ang)
    z = lambda n: jnp.zeros((pos.shape[0], n), F32)
    return (jnp.concatenate([cos, cos, z(KPE_PAD - MLA_ROPE)], axis=1),
            jnp.concatenate([-sin, z(KPE_PAD - half)], axis=1),
            jnp.concatenate([z(half), sin, z(KPE_PAD - MLA_ROPE)], axis=1))


def _layer_weights(l, w1_gate, w1_up, w1_down, w_in, w_uq, w_uk, w_uv, w_pool, w_out, w2_gate, w2_up, w2_down):
    wi = w_in[l]
    c_q, c_kv, k_pe, u, sq, sk, sv = jnp.split(
        wi, [Q_LORA, Q_LORA + KV_LORA, Q_LORA + KV_LORA + MLA_ROPE,
             Q_LORA + KV_LORA + MLA_ROPE + POOL_WIDTH,
             Q_LORA + KV_LORA + MLA_ROPE + POOL_WIDTH + SB_WIDTH,
             Q_LORA + KV_LORA + MLA_ROPE + POOL_WIDTH + 2 * SB_WIDTH], axis=1)
    w_in_p = jnp.concatenate(
        [c_q, u, sq, sk, sv, c_kv, k_pe, jnp.zeros((D_MODEL, KPE_PAD - MLA_ROPE), wi.dtype)], axis=1).astype(BF16)
    wuq = w_uq[l].reshape(Q_LORA, MLA_HEADS, MLA_NOPE + MLA_ROPE)
    wuq = jnp.pad(wuq, ((0, 0), (0, 0), (0, QK_PAD - MLA_NOPE - MLA_ROPE)))
    return dict(
        w1=(w1_gate[l].astype(BF16), w1_up[l].astype(BF16), w1_down[l].astype(BF16)),
        w2=(w2_gate[l].astype(BF16), w2_up[l].astype(BF16), w2_down[l].astype(BF16)),
        w_in=w_in_p,
        wuq=wuq.transpose(1, 0, 2).astype(BF16),
        wuk=w_uk[l].transpose(1, 0, 2).astype(BF16),
        wuv=w_uv[l].transpose(1, 0, 2).astype(BF16),
        w_pool=w_pool[l].astype(BF16),
        w_out=w_out[l].astype(BF16),
    )


def kernel(x_prompt, x_sample, cache_ckv, cache_kpe, state_pool, cache_sb_k, cache_sb_v, g_ffn1, w1_gate, w1_up, w1_down, g_mix, w_in, g_qnorm, w_uq, g_kvnorm, w_uk, w_uv, w_pool, pool_scale, g_mla_out, g_sb_out, w_out, g_ffn2, w2_gate, w2_up, w2_down, g_final):
    mp, ms = P_BATCH * P_SEQ, S_BATCH * S_SEQ
    row = lambda g: g.reshape(1, -1)
    gfin = row(g_final)

    rope_p = _rope_tables(jnp.tile(jnp.arange(P_SEQ), P_BATCH))
    rope_s = _rope_tables(PAST_LEN + jnp.tile(jnp.arange(S_SEQ), S_BATCH))
    hist_p = jnp.zeros((P_BATCH, POOL_HIST + 1, POOL_WIDTH), F32)
    cache_k = cache_sb_k.reshape(DEPTH, S_BATCH, PAST_LEN, SB_WIDTH)
    cache_v = cache_sb_v.reshape(DEPTH, S_BATCH, PAST_LEN, SB_WIDTH)

    hp = x_prompt.reshape(mp, D_MODEL)
    hs = x_sample.reshape(ms, D_MODEL)
    outs = {k: [] for k in ("p_ckv", "p_kpe", "p_pool", "p_sbk", "p_sbv",
                            "s_ckv", "s_kpe", "s_pool", "s_sbk", "s_sbv")}

    for l in range(DEPTH):
        w = _layer_weights(l, w1_gate, w1_up, w1_down, w_in, w_uq, w_uk, w_uv, w_pool, w_out,
                           w2_gate, w2_up, w2_down)
        last = l == DEPTH - 1
        gq, gkv = row(g_qnorm[l]), row(g_kvnorm[l])
        pscale = row(pool_scale[l])
        hist_s = jnp.pad(state_pool[l], ((0, 0), (1, 0), (0, 0)))

        def token_side(h, tm, rope):
            h = _ffn(h, row(g_ffn1[l]), *w["w1"], gfin, tm=tm, tf=256, final_norm=False)
            proj = _inproj(h, row(g_mix[l]), w["w_in"], tm=tm, tn=PROJ_W // 5)
            prep = _mla_prep(proj, *rope, gq, gkv, w["wuq"], w["wuk"], w["wuv"], tm=tm, hb=4)
            return h, proj, prep

        def finish(h, tm, o_mla, o_pool, o_sb):
            h = _outproj(o_mla, o_pool, o_sb, h, row(g_mla_out[l]), row(g_sb_out[l]), w["w_out"], tm=tm, tn=1024)
            return _ffn(h, row(g_ffn2[l]), *w["w2"], gfin, tm=tm, tf=256, final_norm=last)

        hp, proj, (ckv, kpe, q, k, v) = token_side(hp, 512, rope_p)
        o_mla = _mla_prompt(q, k, v, batch=P_BATCH, seq=P_SEQ, tile=256)
        o_sb = _sb_prompt(proj, batch=P_BATCH, seq=P_SEQ, tile=256)
        o_pool = _pool(proj, hist_p, w["w_pool"], pscale, batch=P_BATCH, seq=P_SEQ, tt=512, pos0=0)
        hp = finish(hp, 512, o_mla, o_pool, o_sb)
        u = proj[:, PROJ_U:PROJ_U + POOL_WIDTH].reshape(P_BATCH, P_SEQ, POOL_WIDTH)
        outs["p_ckv"].append(ckv.reshape(P_BATCH, P_SEQ, KV_LORA))
        outs["p_kpe"].append(kpe[:, :MLA_ROPE].reshape(P_BATCH, P_SEQ, MLA_ROPE))
        outs["p_pool"].append(u[:, P_SEQ - POOL_HIST:])
        outs["p_sbk"].append(proj[:, PROJ_SK:PROJ_SK + SB_WIDTH].reshape(P_BATCH, P_SEQ, SB_HEADS, SB_DIM))
        outs["p_sbv"].append(proj[:, PROJ_SV:PROJ_SV + SB_WIDTH].reshape(P_BATCH, P_SEQ, SB_HEADS, SB_DIM))

        hs, proj, (ckv, kpe, q, _, _) = token_side(hs, ms, rope_s)
        o_mla = _mla_sample(q, w["wuk"], w["wuv"], cache_ckv, cache_kpe, ckv, kpe, layer=l, row0=0, tk=1024)
        o_sb = _sb_sample(proj, cache_k, cache_v, layer=l, tk=1024, sub=256)
        o_pool = _pool(proj, hist_s, w["w_pool"], pscale, batch=S_BATCH, seq=S_SEQ, tt=S_SEQ, pos0=PAST_LEN)
        hs = finish(hs, ms, o_mla, o_pool, o_sb)
        u = proj[:, PROJ_U:PROJ_U + POOL_WIDTH].reshape(S_BATCH, S_SEQ, POOL_WIDTH)
        outs["s_ckv"].append(ckv.reshape(S_BATCH, S_SEQ, KV_LORA))
        outs["s_kpe"].append(kpe[:, :MLA_ROPE].reshape(S_BATCH, S_SEQ, MLA_ROPE))
        outs["s_pool"].append(u[:, S_SEQ - POOL_HIST:])
        outs["s_sbk"].append(proj[:, PROJ_SK:PROJ_SK + SB_WIDTH].reshape(S_BATCH, S_SEQ, SB_HEADS, SB_DIM))
        outs["s_sbv"].append(proj[:, PROJ_SV:PROJ_SV + SB_WIDTH].reshape(S_BATCH, S_SEQ, SB_HEADS, SB_DIM))

    st = lambda k: jnp.stack(outs[k])
    return (hp.reshape(P_BATCH, P_SEQ, D_MODEL), hs.reshape(S_BATCH, S_SEQ, D_MODEL),
            st("p_ckv"), st("p_kpe"), st("p_pool"), st("p_sbk"), st("p_sbv"),
            st("s_ckv"), st("s_kpe"), st("s_pool"), st("s_sbk"), st("s_sbv"))
```

```python
import functools
import math

import jax
import jax.numpy as jnp
from jax import lax
from jax.experimental import pallas as pl
from jax.experimental.pallas import tpu as pltpu

F32 = jnp.float32
BF16 = jnp.bfloat16

D_MODEL = 4096
D_FF = 11008
DEPTH = 2
P_BATCH, P_SEQ = 4, 2048
S_BATCH, S_SEQ, PAST_LEN = 16, 16, 4096
CHUNK = 64
MLA_HEADS, MLA_NOPE, MLA_ROPE, MLA_V = 16, 128, 64, 128
Q_LORA, KV_LORA = 1024, 512
ROPE_THETA = 10000.0
MLA_SCALE = 1.0 / math.sqrt(MLA_NOPE + MLA_ROPE)
POOL_WINDOWS = (2, 4, 8, 16)
POOL_GROUP = 256
POOL_WIDTH = 1024
POOL_HIST = 15
SB_HEADS, SB_DIM = 8, 128
SB_WIDTH = 1024
SB_SCALE = 1.0 / math.sqrt(SB_DIM)
EPS = 1e-6

V7X_LANES = 128
V7X_VMEM_LIMIT_BYTES = 60000 * 1024
VALUE_SPILL_BYTES = 8 * 1024 * 1024

PROJ_CQ = 0
PROJ_U = 1024
PROJ_SQ = 2048
PROJ_SK = 3072
PROJ_SV = 4096
PROJ_CKV = 5120
PROJ_KPE = 5632
PROJ_W = 5760
KPE_PAD = 128
QK_PAD = 256

ROW_CHUNK = 16
NEG_BIG = -1e30


def _cparams(semantics, vmem_bytes):
    assert vmem_bytes <= V7X_VMEM_LIMIT_BYTES, vmem_bytes
    limit = min(V7X_VMEM_LIMIT_BYTES, vmem_bytes + VALUE_SPILL_BYTES)
    return pltpu.CompilerParams(dimension_semantics=semantics, vmem_limit_bytes=int(limit))


def _rms(x, g):
    ms = jnp.mean(x * x, axis=-1, keepdims=True)
    return (x * lax.rsqrt(ms + EPS)) * g


def _dot(a, b):
    return jnp.dot(a, b, preferred_element_type=F32)


def _dot_nt(a, b):
    return lax.dot_general(a, b, (((1,), (1,)), ((), ())), preferred_element_type=F32)


def _row_loop(rows, fn):
    def body(c, carry):
        fn(pl.multiple_of(c * ROW_CHUNK, ROW_CHUNK))
        return carry
    lax.fori_loop(0, rows // ROW_CHUNK, body, 0)


def _ffn_step(x_ref, g_ref, wg, wu, wd, gf_ref, o_ref, xn_ref, *, nj, rows, final_norm):
    j = pl.program_id(1)

    @pl.when(j == 0)
    def _():
        def prep(r):
            x = x_ref[pl.ds(r, ROW_CHUNK), :]
            xn_ref[pl.ds(r, ROW_CHUNK), :] = _rms(x, g_ref[...]).astype(BF16)
            o_ref[pl.ds(r, ROW_CHUNK), :] = jnp.zeros((ROW_CHUNK, x.shape[1]), F32)
        _row_loop(rows, prep)

    xn = xn_ref[...]
    gate = _dot(xn, wg)
    up = _dot(xn, wu)
    h = (gate * (1.0 / (1.0 + jnp.exp(-gate))) * up).astype(BF16)
    o_ref[...] += _dot(h, wd)

    @pl.when(j == nj - 1)
    def _():
        def fin(r):
            y = x_ref[pl.ds(r, ROW_CHUNK), :] + 0.5 * o_ref[pl.ds(r, ROW_CHUNK), :]
            if final_norm:
                y = _rms(y, gf_ref[...])
            o_ref[pl.ds(r, ROW_CHUNK), :] = y
        _row_loop(rows, fin)


def _ffn_kernel(x_ref, g_ref, wg_ref, wu_ref, wd_ref, gf_ref, o_ref, xn_ref, **kw):
    _ffn_step(x_ref, g_ref, wg_ref[...], wu_ref[...], wd_ref[...], gf_ref, o_ref, xn_ref, **kw)


def _ffn_cast_kernel(x_ref, g_ref, wg_ref, wu_ref, wd_ref, gf_ref, o_ref, wg_o, wu_o, wd_o, xn_ref, **kw):
    w = [src[...].astype(BF16) for src in (wg_ref, wu_ref, wd_ref)]
    for dst, val in zip((wg_o, wu_o, wd_o), w):
        dst[...] = val
    _ffn_step(x_ref, g_ref, *w, gf_ref, o_ref, xn_ref, **kw)


def _ffn(x, g, wg, wu, wd, g_final, *, tm, tf, final_norm, cast=False):
    m, d = x.shape
    f = wg.shape[1]
    assert m % tm == 0 and f % tf == 0 and tm % ROW_CHUNK == 0 and (not cast or m == tm)
    nj = f // tf
    vmem = (2 * tm * d * 4
            + tm * d * 2
            + 2 * tm * d * 4
            + 2 * 3 * d * tf * (6 if cast else 2)
            + tm * d * 4
            + 4 * tm * tf * 4)
    w_specs = [pl.BlockSpec((d, tf), lambda i, j: (0, j)),
               pl.BlockSpec((d, tf), lambda i, j: (0, j)),
               pl.BlockSpec((tf, d), lambda i, j: (j, 0))]
    x_out = jax.ShapeDtypeStruct((m, d), F32)
    x_spec = pl.BlockSpec((tm, d), lambda i, j: (i, 0))
    return pl.pallas_call(
        functools.partial(_ffn_cast_kernel if cast else _ffn_kernel, nj=nj, rows=tm, final_norm=final_norm),
        out_shape=(x_out,) + tuple(jax.ShapeDtypeStruct(w.shape, BF16) for w in (wg, wu, wd)) if cast else x_out,
        grid=(m // tm, nj),
        in_specs=[
            pl.BlockSpec((tm, d), lambda i, j: (i, 0), pipeline_mode=pl.Buffered(1)),
            pl.BlockSpec((1, d), lambda i, j: (0, 0)),
            *w_specs,
            pl.BlockSpec((1, d), lambda i, j: (0, 0)),
        ],
        out_specs=(x_spec, *w_specs) if cast else x_spec,
        scratch_shapes=[pltpu.VMEM((tm, d), BF16)],
        compiler_params=_cparams(("parallel", "arbitrary"), vmem),
        name="ffn_cast" if cast else "ffn",
    )(x, g, wg, wu, wd, g_final)


def _inproj_kernel(x_ref, g_ref, w_ref, o_ref, xn_ref, *, rows):
    @pl.when(pl.program_id(1) == 0)
    def _():
        def prep(r):
            xn_ref[pl.ds(r, ROW_CHUNK), :] = _rms(x_ref[pl.ds(r, ROW_CHUNK), :], g_ref[...]).astype(BF16)
        _row_loop(rows, prep)

    o_ref[...] = _dot(xn_ref[...], w_ref[...])


def _inproj(x, g, w, *, tm, tn):
    m, d = x.shape
    n = w.shape[1]
    assert m % tm == 0 and n % tn == 0
    vmem = 2 * tm * d * 4 + tm * d * 2 + 2 * d * tn * 2 + 3 * tm * tn * 4
    return pl.pallas_call(
        functools.partial(_inproj_kernel, rows=tm),
        out_shape=jax.ShapeDtypeStruct((m, n), F32),
        grid=(m // tm, n // tn),
        in_specs=[
            pl.BlockSpec((tm, d), lambda i, j: (i, 0), pipeline_mode=pl.Buffered(1)),
            pl.BlockSpec((1, d), lambda i, j: (0, 0)),
            pl.BlockSpec((d, tn), lambda i, j: (0, j)),
        ],
        out_specs=pl.BlockSpec((tm, tn), lambda i, j: (i, j)),
        scratch_shapes=[pltpu.VMEM((tm, d), BF16)],
        compiler_params=_cparams(("parallel", "arbitrary"), vmem),
        name="inproj",
    )(x, g, w)


def _rope_tile(t, c, sa, sb):
    return t * c + pltpu.roll(t, KPE_PAD - MLA_ROPE // 2, 1) * sa + pltpu.roll(t, MLA_ROPE // 2, 1) * sb


def _mla_prep_kernel(cq_ref, ckv_ref, kpe_ref, c_ref, sa_ref, sb_ref, gq_ref, gkv_ref,
                     wuq_ref, wuk_ref, wuv_ref,
                     ckv_o, kpe_o, q_o, k_o, v_o, cqn_s, ckv_s, kpe_s, *, rows, hb):
    @pl.when(pl.program_id(1) == 0)
    def _():
        def prep(r):
            sl = pl.ds(r, ROW_CHUNK)
            cqn_s[sl, :] = _rms(cq_ref[sl, :], gq_ref[...]).astype(BF16)
            ckv = _rms(ckv_ref[sl, :], gkv_ref[...])
            ckv_o[sl, :] = ckv
            ckv_s[sl, :] = ckv.astype(BF16)
            kpe = _rope_tile(kpe_ref[sl, :], c_ref[sl, :], sa_ref[sl, :], sb_ref[sl, :])
            kpe_o[sl, :] = kpe
            kpe_s[sl, :] = kpe.astype(BF16)
        _row_loop(rows, prep)

    for hh in range(hb):
        qf = _dot(cqn_s[...], wuq_ref[hh]) * MLA_SCALE
        q_o[hh, :, pl.ds(0, MLA_NOPE)] = qf[:, :MLA_NOPE].astype(BF16)
        q_o[hh, :, pl.ds(MLA_NOPE, KPE_PAD)] = _rope_tile(
            qf[:, MLA_NOPE:], c_ref[...], sa_ref[...], sb_ref[...]).astype(BF16)
        k_o[hh, :, pl.ds(0, MLA_NOPE)] = _dot(ckv_s[...], wuk_ref[hh]).astype(BF16)
        k_o[hh, :, pl.ds(MLA_NOPE, KPE_PAD)] = kpe_s[...]
        v_o[hh] = _dot(ckv_s[...], wuv_ref[hh]).astype(BF16)


def _mla_prep(proj, rope_c, rope_sa, rope_sb, gq, gkv, wuq, wuk, wuv, *, tm, hb):
    m = proj.shape[0]
    nh = wuq.shape[0]
    assert m % tm == 0 and nh % hb == 0
    vmem = (2 * tm * (Q_LORA + KV_LORA + 4 * KPE_PAD) * 4
            + 2 * hb * (Q_LORA * QK_PAD + 2 * KV_LORA * MLA_NOPE) * 2
            + 2 * tm * (KV_LORA + KPE_PAD) * 4
            + 2 * hb * tm * (2 * QK_PAD + MLA_V) * 2
            + tm * (Q_LORA + KV_LORA + KPE_PAD) * 2
            + 6 * tm * QK_PAD * 4)
    row_blk = lambda w, c: pl.BlockSpec((tm, w), lambda i, h: (i, c))
    head_w = lambda k, n: pl.BlockSpec((hb, k, n), lambda i, h: (h, 0, 0))
    head_o = lambda n: pl.BlockSpec((hb, tm, n), lambda i, h: (h, i, 0))
    return pl.pallas_call(
        functools.partial(_mla_prep_kernel, rows=tm, hb=hb),
        out_shape=(
            jax.ShapeDtypeStruct((m, KV_LORA), F32),
            jax.ShapeDtypeStruct((m, KPE_PAD), F32),
            jax.ShapeDtypeStruct((nh, m, QK_PAD), BF16),
            jax.ShapeDtypeStruct((nh, m, QK_PAD), BF16),
            jax.ShapeDtypeStruct((nh, m, MLA_V), BF16),
        ),
        grid=(m // tm, nh // hb),
        in_specs=[
            row_blk(Q_LORA, PROJ_CQ // Q_LORA),
            row_blk(KV_LORA, PROJ_CKV // KV_LORA),
            row_blk(KPE_PAD, PROJ_KPE // KPE_PAD),
            row_blk(KPE_PAD, 0), row_blk(KPE_PAD, 0), row_blk(KPE_PAD, 0),
            pl.BlockSpec((1, Q_LORA), lambda i, h: (0, 0)),
            pl.BlockSpec((1, KV_LORA), lambda i, h: (0, 0)),
            head_w(Q_LORA, QK_PAD), head_w(KV_LORA, MLA_NOPE), head_w(KV_LORA, MLA_V),
        ],
        out_specs=(row_blk(KV_LORA, 0), row_blk(KPE_PAD, 0), head_o(QK_PAD), head_o(QK_PAD), head_o(MLA_V)),
        scratch_shapes=[pltpu.VMEM((tm, Q_LORA), BF16), pltpu.VMEM((tm, KV_LORA), BF16),
                        pltpu.VMEM((tm, KPE_PAD), BF16)],
        compiler_params=_cparams(("parallel", "arbitrary"), vmem),
        name="mla_prep",
    )(proj, proj, proj, rope_c, rope_sa, rope_sb, gq, gkv, wuq, wuk, wuv)


def _softmax_step(s, v, m, l, acc):
    m_new = jnp.maximum(m, jnp.max(s, axis=-1, keepdims=True))
    p = jnp.exp(s - m_new)
    alpha = jnp.exp(m - m_new)
    l = alpha * l + jnp.sum(p, axis=-1, keepdims=True)
    acc = alpha * acc + _dot(p.astype(BF16), v)
    return m_new, l, acc


def _mla_prompt_kernel(q_ref, k_ref, v_ref, o_ref, *, seq, tile):
    def q_tile(qi, carry):
        q0 = pl.multiple_of(qi * tile, tile)
        q = q_ref[0, pl.ds(q0, tile), :]

        def k_tile(ki, st):
            k0 = pl.multiple_of(ki * tile, tile)
            s = _dot_nt(q, k_ref[0, pl.ds(k0, tile), :])
            return _softmax_step(s, v_ref[0, pl.ds(k0, tile), :], *st)

        init = (jnp.full((tile, 1), NEG_BIG, F32), jnp.zeros((tile, 1), F32),
                jnp.zeros((tile, MLA_V), F32))
        st = lax.fori_loop(0, qi, k_tile, init)
        s = _dot_nt(q, k_ref[0, pl.ds(q0, tile), :])
        row = lax.broadcasted_iota(jnp.int32, (tile, tile), 0)
        col = lax.broadcasted_iota(jnp.int32, (tile, tile), 1)
        s = jnp.where(col // CHUNK <= row // CHUNK, s, NEG_BIG)
        _, l, acc = _softmax_step(s, v_ref[0, pl.ds(q0, tile), :], *st)
        o_ref[pl.ds(q0, tile), :] = acc / l
        return carry

    lax.fori_loop(0, seq // tile, q_tile, 0)


def _mla_prompt(q, k, v, *, batch, seq, tile):
    nh = q.shape[0]
    assert seq % tile == 0 and tile % CHUNK == 0
    vmem = 2 * seq * (2 * QK_PAD + MLA_V) * 2 + 2 * seq * MLA_V * 4 + 8 * tile * tile * 4
    return pl.pallas_call(
        functools.partial(_mla_prompt_kernel, seq=seq, tile=tile),
        out_shape=jax.ShapeDtypeStruct((batch * seq, nh * MLA_V), F32),
        grid=(batch, nh),
        in_specs=[
            pl.BlockSpec((1, seq, QK_PAD), lambda b, h: (h, b, 0)),
            pl.BlockSpec((1, seq, QK_PAD), lambda b, h: (h, b, 0)),
            pl.BlockSpec((1, seq, MLA_V), lambda b, h: (h, b, 0)),
        ],
        out_specs=pl.BlockSpec((seq, MLA_V), lambda b, h: (b, h)),
        compiler_params=_cparams(("parallel", "parallel"), vmem),
        name="mla_prompt",
    )(q, k, v)


def _mla_sample_kernel(q_ref, wuk_ref, wuv_ref, ckv_ref, kpe_ref, ckvn_ref, kpen_ref, o_ref,
                       qlat_s, qpe_s, m_s, l_s, acc_s, *, nkt, nh, tq):
    kt = pl.program_id(1)
    rows = nh * tq

    @pl.when(kt == 0)
    def _():
        for h in range(nh):
            qh = q_ref[h]
            qlat_s[pl.ds(h * tq, tq), :] = _dot_nt(qh[:, :MLA_NOPE], wuk_ref[h]).astype(BF16)
            qpe_s[pl.ds(h * tq, tq), :] = qh[:, MLA_NOPE:MLA_NOPE + KPE_PAD]
        m_s[...] = jnp.full((rows, 1), NEG_BIG, F32)
        l_s[...] = jnp.zeros((rows, 1), F32)
        acc_s[...] = jnp.zeros((rows, KV_LORA), F32)

    def update(ckv, kpe):
        s = _dot_nt(qlat_s[...], ckv) + _dot_nt(qpe_s[:, pl.ds(0, MLA_ROPE)], kpe)
        m, l, acc = _softmax_step(s, ckv, m_s[...], l_s[...], acc_s[...])
        m_s[...] = m
        l_s[...] = l
        acc_s[...] = acc

    update(ckv_ref[0, 0].astype(BF16), kpe_ref[0, 0].astype(BF16))

    @pl.when(kt == nkt - 1)
    def _():
        update(ckvn_ref[...].astype(BF16), kpen_ref[:, pl.ds(0, MLA_ROPE)].astype(BF16))
        o_lat = (acc_s[...] / l_s[...]).astype(BF16)
        for h in range(nh):
            o_ref[:, pl.ds(h * MLA_V, MLA_V)] = _dot(o_lat[h * tq:(h + 1) * tq], wuv_ref[h])


def _mla_sample(q, wuk, wuv, cache_ckv, cache_kpe, ckv_new, kpe_new, *, layer, row0, tk):
    nh = q.shape[0]
    _, batch, past, _ = cache_ckv.shape
    tq = S_SEQ
    assert past % tk == 0 and row0 % tq == 0
    assert (PAST_LEN + S_SEQ - 1) // CHUNK == PAST_LEN // CHUNK
    nkt = past // tk
    rows = nh * tq
    vmem = (2 * nh * tq * QK_PAD * 2 + 4 * nh * KV_LORA * MLA_NOPE * 2
            + 2 * tk * (KV_LORA + KPE_PAD) * 4 + tk * (KV_LORA + KPE_PAD) * 2
            + 2 * tq * (KV_LORA + KPE_PAD) * 4 + 2 * tq * nh * MLA_V * 4
            + rows * (KV_LORA + KPE_PAD) * 2 + rows * (KV_LORA + 2 * V7X_LANES) * 4
            + 4 * rows * tk * 4)
    rb = row0 // tq
    return pl.pallas_call(
        functools.partial(_mla_sample_kernel, nkt=nkt, nh=nh, tq=tq),
        out_shape=jax.ShapeDtypeStruct((batch * tq, nh * MLA_V), F32),
        grid=(batch, nkt),
        in_specs=[
            pl.BlockSpec((nh, tq, QK_PAD), lambda b, t: (0, rb + b, 0)),
            pl.BlockSpec((nh, KV_LORA, MLA_NOPE), lambda b, t: (0, 0, 0)),
            pl.BlockSpec((nh, KV_LORA, MLA_V), lambda b, t: (0, 0, 0)),
            pl.BlockSpec((1, 1, tk, KV_LORA), lambda b, t: (layer, b, t, 0)),
            pl.BlockSpec((1, 1, tk, MLA_ROPE), lambda b, t: (layer, b, t, 0)),
            pl.BlockSpec((tq, KV_LORA), lambda b, t: (rb + b, 0)),
            pl.BlockSpec((tq, KPE_PAD), lambda b, t: (rb + b, 0)),
        ],
        out_specs=pl.BlockSpec((tq, nh * MLA_V), lambda b, t: (b, 0)),
        scratch_shapes=[pltpu.VMEM((rows, KV_LORA), BF16), pltpu.VMEM((rows, KPE_PAD), BF16),
                        pltpu.VMEM((rows, 1), F32), pltpu.VMEM((rows, 1), F32),
                        pltpu.VMEM((rows, KV_LORA), F32)],
        compiler_params=_cparams(("parallel", "arbitrary"), vmem),
        name="mla_sample",
    )(q, wuk, wuv, cache_ckv, cache_kpe, ckv_new, kpe_new)


def _softplus(z):
    return jnp.maximum(z, 0.0) + jnp.log(1.0 + jnp.exp(-jnp.abs(z)))


def _later_keys_matrix(n):
    row = lax.broadcasted_iota(jnp.int32, (n, n), 0)
    col = lax.broadcasted_iota(jnp.int32, (n, n), 1)
    return (row > col).astype(BF16)


def _later_sum(x, u):
    r = x.shape[0]
    hi = x.astype(BF16)
    lo = (x - hi.astype(F32)).astype(BF16)
    e = _dot(jnp.concatenate([hi, lo], axis=0), u)
    return e[:r] + e[r:]


def _sb_weights(z, u, decay, mask=None):
    sp = _softplus(z)
    d = sp if mask is None else jnp.where(mask, sp, 0.0)
    a = jnp.exp(z - sp - _later_sum(d, u) - decay)
    if mask is not None:
        a = jnp.where(mask, a, 0.0)
    return a.astype(BF16), decay + jnp.sum(d, axis=-1, keepdims=True)


def _sb_prompt_kernel(q_ref, k_ref, v_ref, o_ref, qb, kb, vb, *, seq, tq, tk):
    def cast(r):
        sl = pl.ds(r, ROW_CHUNK)
        qb[sl, :] = (q_ref[sl, :] * SB_SCALE).astype(BF16)
        kb[sl, :] = k_ref[sl, :].astype(BF16)
        vb[sl, :] = v_ref[sl, :].astype(BF16)
    _row_loop(seq, cast)

    u = _later_keys_matrix(tk)
    per_q = tq // tk

    def q_tile(qi, carry):
        q0 = pl.multiple_of(qi * tq, tq)
        q = qb[pl.ds(q0, tq), :]

        def key_tile(k0, st, mask=None):
            decay, acc = st
            a, decay = _sb_weights(_dot_nt(q, kb[pl.ds(k0, tk), :]), u, decay, mask)
            return decay, acc + _dot(a, vb[pl.ds(k0, tk), :])

        st = (jnp.zeros((tq, 1), F32), jnp.zeros((tq, SB_DIM), F32))
        for d in reversed(range(per_q)):
            row = lax.broadcasted_iota(jnp.int32, (tq, tk), 0)
            col = lax.broadcasted_iota(jnp.int32, (tq, tk), 1) + d * tk
            st = key_tile(pl.multiple_of(q0 + d * tk, tk), st, mask=col < row)

        def older(step, st):
            for d in reversed(range(per_q)):
                st = key_tile(pl.multiple_of(q0 - (step + 1) * tq + d * tk, tk), st)
            return st

        _, acc = lax.fori_loop(0, qi, older, st)
        o_ref[pl.ds(q0, tq), :] = acc
        return carry

    lax.fori_loop(0, seq // tq, q_tile, 0)


def _sb_prompt(proj, *, batch, seq, tq, tk):
    assert seq % tq == 0 and tq % tk == 0 and seq % ROW_CHUNK == 0
    vmem = 2 * 3 * seq * SB_DIM * 4 + 2 * seq * SB_DIM * 4 + 3 * seq * SB_DIM * 2 + 12 * tq * tk * 4
    col = lambda c0: pl.BlockSpec((seq, SB_DIM), lambda b, h: (b, c0 // SB_DIM + h))
    return pl.pallas_call(
        functools.partial(_sb_prompt_kernel, seq=seq, tq=tq, tk=tk),
        out_shape=jax.ShapeDtypeStruct((batch * seq, SB_WIDTH), F32),
        grid=(batch, SB_HEADS),
        in_specs=[col(PROJ_SQ), col(PROJ_SK), col(PROJ_SV)],
        out_specs=pl.BlockSpec((seq, SB_DIM), lambda b, h: (b, h)),
        scratch_shapes=[pltpu.VMEM((seq, SB_DIM), BF16)] * 3,
        compiler_params=_cparams(("parallel", "parallel"), vmem),
        name="sb_prompt",
    )(proj, proj, proj)


def _sb_sample_kernel(q_ref, kn_ref, vn_ref, kc_ref, vc_ref, o_ref, decay_s, acc_s, *, nkt, sub, tq):
    step = pl.program_id(1)
    rows = SB_HEADS * tq
    head = lambda h: pl.ds(h * SB_DIM, SB_DIM)
    hrow = lambda h: pl.ds(h * tq, tq)

    def scores(k_of_head):
        return jnp.concatenate(
            [_dot_nt((q_ref[:, head(h)] * SB_SCALE).astype(BF16), k_of_head(h)) for h in range(SB_HEADS)], axis=0)

    def weighted(a, v_of_head):
        return jnp.concatenate(
            [_dot(a[h * tq:(h + 1) * tq], v_of_head(h)) for h in range(SB_HEADS)], axis=0)

    @pl.when(step == 0)
    def _():
        pad = jnp.zeros((V7X_LANES - tq, SB_DIM), BF16)
        z = scores(lambda h: jnp.concatenate([kn_ref[:, head(h)].astype(BF16), pad], axis=0))
        t = lax.broadcasted_iota(jnp.int32, (rows, V7X_LANES), 0) % tq
        s = lax.broadcasted_iota(jnp.int32, (rows, V7X_LANES), 1)
        a, decay = _sb_weights(z, _later_keys_matrix(V7X_LANES), jnp.zeros((rows, 1), F32), mask=s < t)
        acc_s[...] = weighted(a, lambda h: jnp.concatenate([vn_ref[:, head(h)].astype(BF16), pad], axis=0))
        decay_s[...] = decay

    @pl.when(step > 0)
    def _():
        u = _later_keys_matrix(sub)
        decay = decay_s[...]
        acc = acc_s[...]
        for c in reversed(range(kc_ref.shape[2] // (sub * SB_HEADS))):
            keys = lambda h: pl.ds(c * sub * SB_HEADS + h, sub, stride=SB_HEADS)
            a, decay = _sb_weights(scores(lambda h: kc_ref[0, 0, keys(h), :].astype(BF16)), u, decay)
            acc = acc + weighted(a, lambda h: vc_ref[0, 0, keys(h), :].astype(BF16))
        decay_s[...] = decay
        acc_s[...] = acc

    @pl.when(step == nkt)
    def _():
        for h in range(SB_HEADS):
            o_ref[:, head(h)] = acc_s[hrow(h), :]


def _sb_sample(proj, cache_k, cache_v, *, layer, tk, sub):
    _, batch, past_rows, _ = cache_k.shape
    past = past_rows // SB_HEADS
    tq = S_SEQ
    assert past % tk == 0 and tk % sub == 0
    nkt = past // tk
    rows = SB_HEADS * tq
    vmem = (2 * 3 * tq * SB_WIDTH * 4 + 2 * 2 * tk * SB_WIDTH * 4 + 2 * tq * SB_WIDTH * 4
            + rows * (SB_DIM + V7X_LANES) * 4 + 2 * sub * SB_WIDTH * 2 + 16 * rows * sub * 4 + sub * sub * 2)
    newest_first = lambda b, s: (layer, b, nkt - jnp.maximum(s, 1), 0)
    col = lambda c0: pl.BlockSpec((tq, SB_WIDTH), lambda b, s: (b, c0 // SB_WIDTH))
    return pl.pallas_call(
        functools.partial(_sb_sample_kernel, nkt=nkt, sub=sub, tq=tq),
        out_shape=jax.ShapeDtypeStruct((batch * tq, SB_WIDTH), F32),
        grid=(batch, nkt + 1),
        in_specs=[col(PROJ_SQ), col(PROJ_SK), col(PROJ_SV),
                  pl.BlockSpec((1, 1, tk * SB_HEADS, SB_DIM), newest_first),
                  pl.BlockSpec((1, 1, tk * SB_HEADS, SB_DIM), newest_first)],
        out_specs=pl.BlockSpec((tq, SB_WIDTH), lambda b, s: (b, 0)),
        scratch_shapes=[pltpu.VMEM((rows, 1), F32), pltpu.VMEM((rows, SB_DIM), F32)],
        compiler_params=_cparams(("parallel", "arbitrary"), vmem),
        name="sb_sample",
    )(proj, proj, proj, cache_k, cache_v)


def _sb_rows_kernel(k_ref, v_ref, ko_ref, vo_ref, *, tt):
    for src, dst in ((k_ref, ko_ref), (v_ref, vo_ref)):
        for h in range(SB_HEADS):
            dst[pl.ds(h, tt, stride=SB_HEADS), :] = src[:, pl.ds(h * SB_DIM, SB_DIM)]


def _sb_rows(proj, *, tt):
    m = proj.shape[0]
    assert m % tt == 0
    vmem = 2 * 2 * 2 * tt * SB_WIDTH * 4
    col = lambda c0: pl.BlockSpec((tt, SB_WIDTH), lambda i: (i, c0 // SB_WIDTH))
    out = jax.ShapeDtypeStruct((m * SB_HEADS, SB_DIM), F32)
    return pl.pallas_call(
        functools.partial(_sb_rows_kernel, tt=tt),
        out_shape=(out, out),
        grid=(m // tt,),
        in_specs=[col(PROJ_SK), col(PROJ_SV)],
        out_specs=(pl.BlockSpec((tt * SB_HEADS, SB_DIM), lambda i: (i, 0)),) * 2,
        compiler_params=_cparams(("parallel",), vmem),
        name="sb_rows",
    )(proj, proj)


def _pool_kernel(u_ref, hist_ref, w_ref, scale_ref, o_ref, carry_s, *, tt, pos0):
    ti = pl.program_id(1)
    halo = carry_s.shape[0]

    @pl.when(ti == 0)
    def _():
        carry_s[...] = hist_ref[0]

    pos = pos0 + ti * tt + lax.broadcasted_iota(jnp.int32, (tt, 1), 0)
    for g, w in enumerate(POOL_WINDOWS):
        cols = pl.ds(g * POOL_GROUP, POOL_GROUP)
        tok = u_ref[:, cols]
        s = jnp.concatenate([carry_s[:, cols], tok], axis=0)
        shift = 1
        while shift < w:
            s = s + pltpu.roll(s, shift, 0)
            shift *= 2
        cnt = jnp.minimum(pos + 1, w).astype(F32)
        pooled = s[halo:] / cnt - tok
        o_ref[:, cols] = (_dot(pooled.astype(BF16), w_ref[g]) * scale_ref[:, cols]).astype(BF16)
    carry_s[...] = u_ref[pl.ds(tt - halo, halo), :]


def _pool(proj, hist, w_pool, scale, *, batch, seq, tt, pos0):
    halo = hist.shape[1]
    assert seq % tt == 0 and tt >= halo and halo > max(POOL_WINDOWS) - 1
    nt = seq // tt
    vmem = (2 * tt * POOL_WIDTH * 4 + 2 * halo * POOL_WIDTH * 4 + 2 * POOL_WIDTH * POOL_GROUP * 2
            + 2 * tt * POOL_WIDTH * 2 + halo * POOL_WIDTH * 4 + 8 * (tt + halo) * POOL_GROUP * 4)
    return pl.pallas_call(
        functools.partial(_pool_kernel, tt=tt, pos0=pos0),
        out_shape=jax.ShapeDtypeStruct((batch * seq, POOL_WIDTH), BF16),
        grid=(batch, nt),
        in_specs=[
            pl.BlockSpec((tt, POOL_WIDTH), lambda b, t: (b * nt + t, PROJ_U // POOL_WIDTH)),
            pl.BlockSpec((1, halo, POOL_WIDTH), lambda b, t: (b, 0, 0)),
            pl.BlockSpec((len(POOL_WINDOWS), POOL_GROUP, POOL_GROUP), lambda b, t: (0, 0, 0)),
            pl.BlockSpec((1, POOL_WIDTH), lambda b, t: (0, 0)),
        ],
        out_specs=pl.BlockSpec((tt, POOL_WIDTH), lambda b, t: (b * nt + t, 0)),
        scratch_shapes=[pltpu.VMEM((halo, POOL_WIDTH), F32)],
        compiler_params=_cparams(("parallel", "arbitrary"), vmem),
        name="pool",
    )(proj, hist, w_pool, scale)


def _outproj_kernel(mla_ref, pool_ref, sb_ref, x_ref, gm_ref, gs_ref, w_ref, o_ref, mix_s, *, rows):
    wm = mla_ref.shape[1]
    wp = pool_ref.shape[1]

    @pl.when(pl.program_id(1) == 0)
    def _():
        def prep(r):
            sl = pl.ds(r, ROW_CHUNK)
            mix_s[sl, pl.ds(0, wm)] = _rms(mla_ref[sl, :], gm_ref[...]).astype(BF16)
            mix_s[sl, pl.ds(wm, wp)] = pool_ref[sl, :]
            mix_s[sl, pl.ds(wm + wp, sb_ref.shape[1])] = _rms(sb_ref[sl, :], gs_ref[...]).astype(BF16)
        _row_loop(rows, prep)

    o_ref[...] = x_ref[...] + _dot(mix_s[...], w_ref[...])


def _outproj(o_mla, o_pool, o_sb, x, g_mla, g_sb, w, *, tm, tn):
    m, d = x.shape
    wm, wp, ws = o_mla.shape[1], o_pool.shape[1], o_sb.shape[1]
    k = wm + wp + ws
    assert m % tm == 0 and d % tn == 0 and w.shape == (k, d)
    vmem = (2 * tm * (wm + ws) * 4 + 2 * tm * wp * 2 + 2 * 2 * tm * tn * 4 + 2 * k * tn * 2
            + tm * k * 2 + 2 * tm * tn * 4)
    full = lambda n: pl.BlockSpec((tm, n), lambda i, j: (i, 0))
    return pl.pallas_call(
        functools.partial(_outproj_kernel, rows=tm),
        out_shape=jax.ShapeDtypeStruct((m, d), F32),
        grid=(m // tm, d // tn),
        in_specs=[full(wm), full(wp), full(ws),
                  pl.BlockSpec((tm, tn), lambda i, j: (i, j)),
                  pl.BlockSpec((1, wm), lambda i, j: (0, 0)),
                  pl.BlockSpec((1, ws), lambda i, j: (0, 0)),
                  pl.BlockSpec((k, tn), lambda i, j: (0, j))],
        out_specs=pl.BlockSpec((tm, tn), lambda i, j: (i, j)),
        scratch_shapes=[pltpu.VMEM((tm, k), BF16)],
        compiler_params=_cparams(("parallel", "arbitrary"), vmem),
        name="outproj",
    )(o_mla, o_pool, o_sb, x, g_mla, g_sb, w)


def _rope_tables(pos):
    half = MLA_ROPE // 2
    inv = 1.0 / (ROPE_THETA ** (jnp.arange(half, dtype=F32) / half))
    ang = pos.astype(F32)[:, None] * inv[None, :]
    cos, sin = jnp.cos(ang), jnp.sin(ang)
    z = lambda n: jnp.zeros((pos.shape[0], n), F32)
    return (jnp.concatenate([cos, cos, z(KPE_PAD - MLA_ROPE)], axis=1),
            jnp.concatenate([-sin, z(KPE_PAD - half)], axis=1),
            jnp.concatenate([z(half), sin, z(KPE_PAD - MLA_ROPE)], axis=1))


def _layer_weights(l, w_in, w_uq, w_uk, w_uv, w_pool, w_out):
    wi = w_in[l]
    c_q, c_kv, k_pe, u, sq, sk, sv = jnp.split(
        wi, [Q_LORA, Q_LORA + KV_LORA, Q_LORA + KV_LORA + MLA_ROPE,
             Q_LORA + KV_LORA + MLA_ROPE + POOL_WIDTH,
             Q_LORA + KV_LORA + MLA_ROPE + POOL_WIDTH + SB_WIDTH,
             Q_LORA + KV_LORA + MLA_ROPE + POOL_WIDTH + 2 * SB_WIDTH], axis=1)
    w_in_p = jnp.concatenate(
        [c_q, u, sq, sk, sv, c_kv, k_pe, jnp.zeros((D_MODEL, KPE_PAD - MLA_ROPE), wi.dtype)], axis=1).astype(BF16)
    wuq = w_uq[l].reshape(Q_LORA, MLA_HEADS, MLA_NOPE + MLA_ROPE)
    wuq = jnp.pad(wuq, ((0, 0), (0, 0), (0, QK_PAD - MLA_NOPE - MLA_ROPE)))
    return dict(
        w_in=w_in_p,
        wuq=wuq.transpose(1, 0, 2).astype(BF16),
        wuk=w_uk[l].transpose(1, 0, 2).astype(BF16),
        wuv=w_uv[l].transpose(1, 0, 2).astype(BF16),
        w_pool=w_pool[l].astype(BF16),
        w_out=w_out[l].astype(BF16),
    )


def kernel(x_prompt, x_sample, cache_ckv, cache_kpe, state_pool, cache_sb_k, cache_sb_v, g_ffn1, w1_gate, w1_up, w1_down, g_mix, w_in, g_qnorm, w_uq, g_kvnorm, w_uk, w_uv, w_pool, pool_scale, g_mla_out, g_sb_out, w_out, g_ffn2, w2_gate, w2_up, w2_down, g_final):
    mp, ms = P_BATCH * P_SEQ, S_BATCH * S_SEQ
    tm_p, tf = 512, 256
    row = lambda g: g.reshape(1, -1)
    gfin = row(g_final)

    rope_p = _rope_tables(jnp.tile(jnp.arange(P_SEQ), P_BATCH))
    rope_s = _rope_tables(PAST_LEN + jnp.tile(jnp.arange(S_SEQ), S_BATCH))
    hist_p = jnp.zeros((P_BATCH, POOL_HIST + 1, POOL_WIDTH), F32)
    cache_k = cache_sb_k.reshape(DEPTH, S_BATCH, PAST_LEN * SB_HEADS, SB_DIM)
    cache_v = cache_sb_v.reshape(DEPTH, S_BATCH, PAST_LEN * SB_HEADS, SB_DIM)

    hp = x_prompt.reshape(mp, D_MODEL)
    hs = x_sample.reshape(ms, D_MODEL)
    outs = {k: [] for k in ("p_ckv", "p_kpe", "p_pool", "p_sbk", "p_sbv",
                            "s_ckv", "s_kpe", "s_pool", "s_sbk", "s_sbv")}

    def record(tag, batch, seq, proj, ckv, kpe, sbk, sbv):
        u = proj[:, PROJ_U:PROJ_U + POOL_WIDTH].reshape(batch, seq, POOL_WIDTH)
        outs[tag + "_ckv"].append(ckv.reshape(batch, seq, KV_LORA))
        outs[tag + "_kpe"].append(kpe[:, :MLA_ROPE].reshape(batch, seq, MLA_ROPE))
        outs[tag + "_pool"].append(u[:, seq - POOL_HIST:])
        outs[tag + "_sbk"].append(sbk.reshape(batch, seq, SB_HEADS, SB_DIM))
        outs[tag + "_sbv"].append(sbv.reshape(batch, seq, SB_HEADS, SB_DIM))

    for l in range(DEPTH):
        w = _layer_weights(l, w_in, w_uq, w_uk, w_uv, w_pool, w_out)
        last = l == DEPTH - 1
        gq, gkv = row(g_qnorm[l]), row(g_kvnorm[l])
        pscale = row(pool_scale[l])
        hist_s = jnp.pad(state_pool[l], ((0, 0), (1, 0), (0, 0)))

        def mixer_inputs(h, tm, rope):
            proj = _inproj(h, row(g_mix[l]), w["w_in"], tm=tm, tn=PROJ_W // 5)
            return proj, _mla_prep(proj, *rope, gq, gkv, w["wuq"], w["wuk"], w["wuv"], tm=tm, hb=4)

        def mix(h, tm, o_mla, o_pool, o_sb):
            return _outproj(o_mla, o_pool, o_sb, h, row(g_mla_out[l]), row(g_sb_out[l]), w["w_out"], tm=tm, tn=1024)

        hs, *w1 = _ffn(hs, row(g_ffn1[l]), w1_gate[l], w1_up[l], w1_down[l], gfin, tm=ms, tf=tf // 2,
                       final_norm=False, cast=True)
        proj, (ckv, kpe, q, _, _) = mixer_inputs(hs, ms, rope_s)
        o_mla = _mla_sample(q, w["wuk"], w["wuv"], cache_ckv, cache_kpe, ckv, kpe, layer=l, row0=0, tk=1024)
        o_sb = _sb_sample(proj, cache_k, cache_v, layer=l, tk=1024, sub=256)
        o_pool = _pool(proj, hist_s, w["w_pool"], pscale, batch=S_BATCH, seq=S_SEQ, tt=S_SEQ, pos0=PAST_LEN)
        hs = mix(hs, ms, o_mla, o_pool, o_sb)
        hs, *w2 = _ffn(hs, row(g_ffn2[l]), w2_gate[l], w2_up[l], w2_down[l], gfin, tm=ms, tf=tf // 2,
                       final_norm=last, cast=True)
        record("s", S_BATCH, S_SEQ, proj, ckv, kpe, *_sb_rows(proj, tt=ms))

        hp = _ffn(hp, row(g_ffn1[l]), *w1, gfin, tm=tm_p, tf=tf, final_norm=False)
        proj, (ckv, kpe, q, k, v) = mixer_inputs(hp, tm_p, rope_p)
        o_mla = _mla_prompt(q, k, v, batch=P_BATCH, seq=P_SEQ, tile=512)
        o_sb = _sb_prompt(proj, batch=P_BATCH, seq=P_SEQ, tq=512, tk=256)
        o_pool = _pool(proj, hist_p, w["w_pool"], pscale, batch=P_BATCH, seq=P_SEQ, tt=512, pos0=0)
        hp = mix(hp, tm_p, o_mla, o_pool, o_sb)
        hp = _ffn(hp, row(g_ffn2[l]), *w2, gfin, tm=tm_p, tf=tf, final_norm=last)
        record("p", P_BATCH, P_SEQ, proj, ckv, kpe, *_sb_rows(proj, tt=tm_p))

    st = lambda k: jnp.stack(outs[k])
    return (hp.reshape(P_BATCH, P_SEQ, D_MODEL), hs.reshape(S_BATCH, S_SEQ, D_MODEL),
            st("p_ckv"), st("p_kpe"), st("p_pool"), st("p_sbk"), st("p_sbv"),
            st("s_ckv"), st("s_kpe"), st("s_pool"), st("s_sbk"), st("s_sbv"))
```

```python
import functools
import math

import jax
import jax.numpy as jnp
from jax import lax
from jax.experimental import pallas as pl
from jax.experimental.pallas import tpu as pltpu

F32 = jnp.float32
BF16 = jnp.bfloat16

D_MODEL = 4096
D_FF = 11008
DEPTH = 2
P_BATCH, P_SEQ = 4, 2048
S_BATCH, S_SEQ, PAST_LEN = 16, 16, 4096
CHUNK = 64
MLA_HEADS, MLA_NOPE, MLA_ROPE, MLA_V = 16, 128, 64, 128
Q_LORA, KV_LORA = 1024, 512
ROPE_THETA = 10000.0
MLA_SCALE = 1.0 / math.sqrt(MLA_NOPE + MLA_ROPE)
POOL_WINDOWS = (2, 4, 8, 16)
POOL_GROUP = 256
POOL_WIDTH = 1024
POOL_HIST = 15
SB_HEADS, SB_DIM = 8, 128
SB_WIDTH = 1024
SB_SCALE = 1.0 / math.sqrt(SB_DIM)
EPS = 1e-6

V7X_LANES = 128
V7X_VMEM_LIMIT_BYTES = 60000 * 1024
VALUE_SPILL_BYTES = 8 * 1024 * 1024

PROJ_CQ = 0
PROJ_U = 1024
PROJ_SQ = 2048
PROJ_SK = 3072
PROJ_SV = 4096
PROJ_CKV = 5120
PROJ_KPE = 5632
PROJ_W = 5760
KPE_PAD = 128
QK_PAD = 256

ROW_CHUNK = 64
NEG_BIG = -1e30


def _cparams(semantics, vmem_bytes):
    assert vmem_bytes <= V7X_VMEM_LIMIT_BYTES, vmem_bytes
    limit = min(V7X_VMEM_LIMIT_BYTES, vmem_bytes + VALUE_SPILL_BYTES)
    return pltpu.CompilerParams(dimension_semantics=semantics, vmem_limit_bytes=int(limit))


def _rms(x, g):
    ms = jnp.mean(x * x, axis=-1, keepdims=True)
    return (x * lax.rsqrt(ms + EPS)) * g


def _dot(a, b):
    return jnp.dot(a, b, preferred_element_type=F32)


def _dot_nt(a, b):
    return lax.dot_general(a, b, (((1,), (1,)), ((), ())), preferred_element_type=F32)


def _row_loop(rows, fn):
    def body(c, carry):
        fn(pl.multiple_of(c * ROW_CHUNK, ROW_CHUNK))
        return carry
    lax.fori_loop(0, rows // ROW_CHUNK, body, 0)


def _ffn_step(x_ref, g_ref, wg_ref, wu_ref, wd_ref, gf_ref, o_ref, xn_ref, *, nj, rows, final_norm):
    j = pl.program_id(1)

    @pl.when(j == 0)
    def _():
        def prep(r):
            x = x_ref[pl.ds(r, ROW_CHUNK), :]
            xn_ref[pl.ds(r, ROW_CHUNK), :] = _rms(x, g_ref[...]).astype(BF16)
            o_ref[pl.ds(r, ROW_CHUNK), :] = jnp.zeros((ROW_CHUNK, x.shape[1]), F32)
        _row_loop(rows, prep)

    xn = xn_ref[...]
    gate = _dot(xn, wg_ref[...])
    up = _dot(xn, wu_ref[...])
    h = (gate * (1.0 / (1.0 + jnp.exp(-gate))) * up).astype(BF16)
    o_ref[...] += _dot(h, wd_ref[...])

    @pl.when(j == nj - 1)
    def _():
        def fin(r):
            y = x_ref[pl.ds(r, ROW_CHUNK), :] + 0.5 * o_ref[pl.ds(r, ROW_CHUNK), :]
            if final_norm:
                y = _rms(y, gf_ref[...])
            o_ref[pl.ds(r, ROW_CHUNK), :] = y
        _row_loop(rows, fin)


def _ffn_kernel(x_ref, g_ref, wg_ref, wu_ref, wd_ref, gf_ref, o_ref, xn_ref, **kw):
    _ffn_step(x_ref, g_ref, wg_ref, wu_ref, wd_ref, gf_ref, o_ref, xn_ref, **kw)


def _ffn_cast_kernel(x_ref, g_ref, wg_ref, wu_ref, wd_ref, gf_ref, o_ref, wg_o, wu_o, wd_o, xn_ref, **kw):
    for src, dst in ((wg_ref, wg_o), (wu_ref, wu_o), (wd_ref, wd_o)):
        dst[...] = src[0].astype(BF16)
    _ffn_step(x_ref, g_ref, wg_o, wu_o, wd_o, gf_ref, o_ref, xn_ref, **kw)


def _ffn(x, g, wg, wu, wd, g_final, *, tm, tf, final_norm, cast_layer=None):
    m, d = x.shape
    cast = cast_layer is not None
    f = wg.shape[-1]
    assert m % tm == 0 and f % tf == 0 and tm % ROW_CHUNK == 0 and (not cast or m == tm)
    nj = f // tf
    vmem = (2 * tm * d * 4
            + tm * d * 2
            + 2 * tm * d * 4
            + 2 * 3 * d * tf * (6 if cast else 2)
            + tm * d * 4
            + 4 * tm * tf * 4)
    w_specs = [pl.BlockSpec((d, tf), lambda i, j: (0, j)),
               pl.BlockSpec((d, tf), lambda i, j: (0, j)),
               pl.BlockSpec((tf, d), lambda i, j: (j, 0))]
    w_in_specs = w_specs
    if cast:
        w_in_specs = [pl.BlockSpec((1, d, tf), lambda i, j: (cast_layer, 0, j)),
                      pl.BlockSpec((1, d, tf), lambda i, j: (cast_layer, 0, j)),
                      pl.BlockSpec((1, tf, d), lambda i, j: (cast_layer, j, 0))]
    x_out = jax.ShapeDtypeStruct((m, d), F32)
    x_spec = pl.BlockSpec((tm, d), lambda i, j: (i, 0))
    return pl.pallas_call(
        functools.partial(_ffn_cast_kernel if cast else _ffn_kernel, nj=nj, rows=tm, final_norm=final_norm),
        out_shape=(x_out,) + tuple(jax.ShapeDtypeStruct(w.shape[1:], BF16) for w in (wg, wu, wd)) if cast else x_out,
        grid=(m // tm, nj),
        in_specs=[
            pl.BlockSpec((tm, d), lambda i, j: (i, 0), pipeline_mode=pl.Buffered(1)),
            pl.BlockSpec((1, d), lambda i, j: (0, 0)),
            *w_in_specs,
            pl.BlockSpec((1, d), lambda i, j: (0, 0)),
        ],
        out_specs=(x_spec, *w_specs) if cast else x_spec,
        scratch_shapes=[pltpu.VMEM((tm, d), BF16)],
        compiler_params=_cparams(("parallel", "arbitrary"), vmem),
        name="ffn_cast" if cast else "ffn",
    )(x, g, wg, wu, wd, g_final)


def _inproj_kernel(x_ref, g_ref, w_ref, o_ref, xn_ref, *, rows):
    @pl.when(pl.program_id(1) == 0)
    def _():
        def prep(r):
            xn_ref[pl.ds(r, ROW_CHUNK), :] = _rms(x_ref[pl.ds(r, ROW_CHUNK), :], g_ref[...]).astype(BF16)
        _row_loop(rows, prep)

    o_ref[...] = _dot(xn_ref[...], w_ref[...])


def _inproj(x, g, w, *, tm, tn):
    m, d = x.shape
    n = w.shape[1]
    assert m % tm == 0 and n % tn == 0
    vmem = 2 * tm * d * 4 + tm * d * 2 + 2 * d * tn * 2 + 3 * tm * tn * 4
    return pl.pallas_call(
        functools.partial(_inproj_kernel, rows=tm),
        out_shape=jax.ShapeDtypeStruct((m, n), F32),
        grid=(m // tm, n // tn),
        in_specs=[
            pl.BlockSpec((tm, d), lambda i, j: (i, 0), pipeline_mode=pl.Buffered(1)),
            pl.BlockSpec((1, d), lambda i, j: (0, 0)),
            pl.BlockSpec((d, tn), lambda i, j: (0, j)),
        ],
        out_specs=pl.BlockSpec((tm, tn), lambda i, j: (i, j)),
        scratch_shapes=[pltpu.VMEM((tm, d), BF16)],
        compiler_params=_cparams(("parallel", "arbitrary"), vmem),
        name="inproj",
    )(x, g, w)


def _rope_tile(t, c, sa, sb):
    return t * c + pltpu.roll(t, KPE_PAD - MLA_ROPE // 2, 1) * sa + pltpu.roll(t, MLA_ROPE // 2, 1) * sb


def _mla_prep_kernel(cq_ref, ckv_ref, kpe_ref, c_ref, sa_ref, sb_ref, gq_ref, gkv_ref,
                     wuq_ref, wuk_ref, wuv_ref,
                     ckv_o, kpe_o, q_o, k_o, v_o, cqn_s, ckv_s, kpe_s, *, rows, hb):
    @pl.when(pl.program_id(1) == 0)
    def _():
        def prep(r):
            sl = pl.ds(r, ROW_CHUNK)
            cqn_s[sl, :] = _rms(cq_ref[sl, :], gq_ref[...]).astype(BF16)
            ckv = _rms(ckv_ref[sl, :], gkv_ref[...])
            ckv_o[sl, :] = ckv
            ckv_s[sl, :] = ckv.astype(BF16)
            kpe = _rope_tile(kpe_ref[sl, :], c_ref[sl, :], sa_ref[sl, :], sb_ref[sl, :])
            kpe_o[sl, :] = kpe
            kpe_s[sl, :] = kpe.astype(BF16)
        _row_loop(rows, prep)

    for hh in range(hb):
        qf = _dot(cqn_s[...], wuq_ref[hh]) * MLA_SCALE
        q_o[hh, :, pl.ds(0, MLA_NOPE)] = qf[:, :MLA_NOPE].astype(BF16)
        q_o[hh, :, pl.ds(MLA_NOPE, KPE_PAD)] = _rope_tile(
            qf[:, MLA_NOPE:], c_ref[...], sa_ref[...], sb_ref[...]).astype(BF16)
        k_o[hh, :, pl.ds(0, MLA_NOPE)] = _dot(ckv_s[...], wuk_ref[hh]).astype(BF16)
        k_o[hh, :, pl.ds(MLA_NOPE, KPE_PAD)] = kpe_s[...]
        v_o[hh] = _dot(ckv_s[...], wuv_ref[hh]).astype(BF16)


def _mla_prep(proj, rope_c, rope_sa, rope_sb, gq, gkv, wuq, wuk, wuv, *, tm, hb):
    m = proj.shape[0]
    nh = wuq.shape[0]
    assert m % tm == 0 and nh % hb == 0
    vmem = (2 * tm * (Q_LORA + KV_LORA + 4 * KPE_PAD) * 4
            + 2 * hb * (Q_LORA * QK_PAD + 2 * KV_LORA * MLA_NOPE) * 2
            + 2 * tm * (KV_LORA + KPE_PAD) * 4
            + 2 * hb * tm * (2 * QK_PAD + MLA_V) * 2
            + tm * (Q_LORA + KV_LORA + KPE_PAD) * 2
            + 6 * tm * QK_PAD * 4)
    row_blk = lambda w, c: pl.BlockSpec((tm, w), lambda i, h: (i, c))
    head_w = lambda k, n: pl.BlockSpec((hb, k, n), lambda i, h: (h, 0, 0))
    head_o = lambda n: pl.BlockSpec((hb, tm, n), lambda i, h: (h, i, 0))
    return pl.pallas_call(
        functools.partial(_mla_prep_kernel, rows=tm, hb=hb),
        out_shape=(
            jax.ShapeDtypeStruct((m, KV_LORA), F32),
            jax.ShapeDtypeStruct((m, KPE_PAD), F32),
            jax.ShapeDtypeStruct((nh, m, QK_PAD), BF16),
            jax.ShapeDtypeStruct((nh, m, QK_PAD), BF16),
            jax.ShapeDtypeStruct((nh, m, MLA_V), BF16),
        ),
        grid=(m // tm, nh // hb),
        in_specs=[
            row_blk(Q_LORA, PROJ_CQ // Q_LORA),
            row_blk(KV_LORA, PROJ_CKV // KV_LORA),
            row_blk(KPE_PAD, PROJ_KPE // KPE_PAD),
            row_blk(KPE_PAD, 0), row_blk(KPE_PAD, 0), row_blk(KPE_PAD, 0),
            pl.BlockSpec((1, Q_LORA), lambda i, h: (0, 0)),
            pl.BlockSpec((1, KV_LORA), lambda i, h: (0, 0)),
            head_w(Q_LORA, QK_PAD), head_w(KV_LORA, MLA_NOPE), head_w(KV_LORA, MLA_V),
        ],
        out_specs=(row_blk(KV_LORA, 0), row_blk(KPE_PAD, 0), head_o(QK_PAD), head_o(QK_PAD), head_o(MLA_V)),
        scratch_shapes=[pltpu.VMEM((tm, Q_LORA), BF16), pltpu.VMEM((tm, KV_LORA), BF16),
                        pltpu.VMEM((tm, KPE_PAD), BF16)],
        compiler_params=_cparams(("parallel", "arbitrary"), vmem),
        name="mla_prep",
    )(proj, proj, proj, rope_c, rope_sa, rope_sb, gq, gkv, wuq, wuk, wuv)


def _softmax_step(s, v, m, l, acc):
    m_new = jnp.maximum(m, jnp.max(s, axis=-1, keepdims=True))
    p = jnp.exp(s - m_new)
    alpha = jnp.exp(m - m_new)
    l = alpha * l + jnp.sum(p, axis=-1, keepdims=True)
    acc = alpha * acc + _dot(p.astype(BF16), v)
    return m_new, l, acc


def _mla_prompt_kernel(q_ref, k_ref, v_ref, o_ref, *, seq, tile):
    def q_tile(qi, carry):
        q0 = pl.multiple_of(qi * tile, tile)
        q = q_ref[0, pl.ds(q0, tile), :]

        def k_tile(ki, st):
            k0 = pl.multiple_of(ki * tile, tile)
            s = _dot_nt(q, k_ref[0, pl.ds(k0, tile), :])
            return _softmax_step(s, v_ref[0, pl.ds(k0, tile), :], *st)

        init = (jnp.full((tile, 1), NEG_BIG, F32), jnp.zeros((tile, 1), F32),
                jnp.zeros((tile, MLA_V), F32))
        st = lax.fori_loop(0, qi, k_tile, init)
        s = _dot_nt(q, k_ref[0, pl.ds(q0, tile), :])
        row = lax.broadcasted_iota(jnp.int32, (tile, tile), 0)
        col = lax.broadcasted_iota(jnp.int32, (tile, tile), 1)
        s = jnp.where(col // CHUNK <= row // CHUNK, s, NEG_BIG)
        _, l, acc = _softmax_step(s, v_ref[0, pl.ds(q0, tile), :], *st)
        o_ref[pl.ds(q0, tile), :] = acc / l
        return carry

    lax.fori_loop(0, seq // tile, q_tile, 0)


def _mla_prompt(q, k, v, *, batch, seq, tile):
    nh = q.shape[0]
    assert seq % tile == 0 and tile % CHUNK == 0
    vmem = 2 * seq * (2 * QK_PAD + MLA_V) * 2 + 2 * seq * MLA_V * 4 + 8 * tile * tile * 4
    return pl.pallas_call(
        functools.partial(_mla_prompt_kernel, seq=seq, tile=tile),
        out_shape=jax.ShapeDtypeStruct((batch * seq, nh * MLA_V), F32),
        grid=(batch, nh),
        in_specs=[
            pl.BlockSpec((1, seq, QK_PAD), lambda b, h: (h, b, 0)),
            pl.BlockSpec((1, seq, QK_PAD), lambda b, h: (h, b, 0)),
            pl.BlockSpec((1, seq, MLA_V), lambda b, h: (h, b, 0)),
        ],
        out_specs=pl.BlockSpec((seq, MLA_V), lambda b, h: (b, h)),
        compiler_params=_cparams(("parallel", "parallel"), vmem),
        name="mla_prompt",
    )(q, k, v)


def _mla_sample_kernel(q_ref, wuk_ref, wuv_ref, ckv_ref, kpe_ref, ckvn_ref, kpen_ref, o_ref,
                       qlat_s, qpe_s, m_s, l_s, acc_s, *, nkt, nh, tq):
    kt = pl.program_id(1)
    rows = nh * tq

    @pl.when(kt == 0)
    def _():
        for h in range(nh):
            qh = q_ref[h]
            qlat_s[pl.ds(h * tq, tq), :] = _dot_nt(qh[:, :MLA_NOPE], wuk_ref[h]).astype(BF16)
            qpe_s[pl.ds(h * tq, tq), :] = qh[:, MLA_NOPE:MLA_NOPE + KPE_PAD]
        m_s[...] = jnp.full((rows, 1), NEG_BIG, F32)
        l_s[...] = jnp.zeros((rows, 1), F32)
        acc_s[...] = jnp.zeros((rows, KV_LORA), F32)

    def update(ckv, kpe):
        s = _dot_nt(qlat_s[...], ckv) + _dot_nt(qpe_s[:, pl.ds(0, MLA_ROPE)], kpe)
        m, l, acc = _softmax_step(s, ckv, m_s[...], l_s[...], acc_s[...])
        m_s[...] = m
        l_s[...] = l
        acc_s[...] = acc

    update(ckv_ref[0, 0].astype(BF16), kpe_ref[0, 0].astype(BF16))

    @pl.when(kt == nkt - 1)
    def _():
        update(ckvn_ref[...].astype(BF16), kpen_ref[:, pl.ds(0, MLA_ROPE)].astype(BF16))
        o_lat = (acc_s[...] / l_s[...]).astype(BF16)
        for h in range(nh):
            o_ref[:, pl.ds(h * MLA_V, MLA_V)] = _dot(o_lat[h * tq:(h + 1) * tq], wuv_ref[h])


def _mla_sample(q, wuk, wuv, cache_ckv, cache_kpe, ckv_new, kpe_new, *, layer, row0, tk):
    nh = q.shape[0]
    _, batch, past, _ = cache_ckv.shape
    tq = S_SEQ
    assert past % tk == 0 and row0 % tq == 0
    assert (PAST_LEN + S_SEQ - 1) // CHUNK == PAST_LEN // CHUNK
    nkt = past // tk
    rows = nh * tq
    vmem = (2 * nh * tq * QK_PAD * 2 + 4 * nh * KV_LORA * MLA_NOPE * 2
            + 2 * tk * (KV_LORA + KPE_PAD) * 4 + tk * (KV_LORA + KPE_PAD) * 2
            + 2 * tq * (KV_LORA + KPE_PAD) * 4 + 2 * tq * nh * MLA_V * 4
            + rows * (KV_LORA + KPE_PAD) * 2 + rows * (KV_LORA + 2 * V7X_LANES) * 4
            + 4 * rows * tk * 4)
    rb = row0 // tq
    return pl.pallas_call(
        functools.partial(_mla_sample_kernel, nkt=nkt, nh=nh, tq=tq),
        out_shape=jax.ShapeDtypeStruct((batch * tq, nh * MLA_V), F32),
        grid=(batch, nkt),
        in_specs=[
            pl.BlockSpec((nh, tq, QK_PAD), lambda b, t: (0, rb + b, 0)),
            pl.BlockSpec((nh, KV_LORA, MLA_NOPE), lambda b, t: (0, 0, 0)),
            pl.BlockSpec((nh, KV_LORA, MLA_V), lambda b, t: (0, 0, 0)),
            pl.BlockSpec((1, 1, tk, KV_LORA), lambda b, t: (layer, b, t, 0)),
            pl.BlockSpec((1, 1, tk, MLA_ROPE), lambda b, t: (layer, b, t, 0)),
            pl.BlockSpec((tq, KV_LORA), lambda b, t: (rb + b, 0)),
            pl.BlockSpec((tq, KPE_PAD), lambda b, t: (rb + b, 0)),
        ],
        out_specs=pl.BlockSpec((tq, nh * MLA_V), lambda b, t: (b, 0)),
        scratch_shapes=[pltpu.VMEM((rows, KV_LORA), BF16), pltpu.VMEM((rows, KPE_PAD), BF16),
                        pltpu.VMEM((rows, 1), F32), pltpu.VMEM((rows, 1), F32),
                        pltpu.VMEM((rows, KV_LORA), F32)],
        compiler_params=_cparams(("parallel", "arbitrary"), vmem),
        name="mla_sample",
    )(q, wuk, wuv, cache_ckv, cache_kpe, ckv_new, kpe_new)


def _softplus(z):
    return jnp.maximum(z, 0.0) + jnp.log(1.0 + jnp.exp(-jnp.abs(z)))


def _later_keys_matrix(n):
    row = lax.broadcasted_iota(jnp.int32, (n, n), 0)
    col = lax.broadcasted_iota(jnp.int32, (n, n), 1)
    return (row > col).astype(BF16)


def _later_sum(x, u):
    r = x.shape[0]
    hi = x.astype(BF16)
    lo = (x - hi.astype(F32)).astype(BF16)
    e = _dot(jnp.concatenate([hi, lo], axis=0), u)
    return e[:r] + e[r:]


def _sb_weights(z, u, decay, mask=None):
    sp = _softplus(z)
    d = sp if mask is None else jnp.where(mask, sp, 0.0)
    a = jnp.exp(z - sp - _later_sum(d, u) - decay)
    if mask is not None:
        a = jnp.where(mask, a, 0.0)
    return a.astype(BF16), decay + jnp.sum(d, axis=-1, keepdims=True)


def _sb_prompt_kernel(q_ref, k_ref, v_ref, o_ref, qb, kb, vb, *, seq, tq, tk):
    def cast(r):
        sl = pl.ds(r, ROW_CHUNK)
        qb[sl, :] = (q_ref[sl, :] * SB_SCALE).astype(BF16)
        kb[sl, :] = k_ref[sl, :].astype(BF16)
        vb[sl, :] = v_ref[sl, :].astype(BF16)
    _row_loop(seq, cast)

    u = _later_keys_matrix(tk)
    per_q = tq // tk

    def q_tile(qi, carry):
        q0 = pl.multiple_of(qi * tq, tq)
        q = qb[pl.ds(q0, tq), :]

        def key_tile(k0, st, mask=None):
            decay, acc = st
            a, decay = _sb_weights(_dot_nt(q, kb[pl.ds(k0, tk), :]), u, decay, mask)
            return decay, acc + _dot(a, vb[pl.ds(k0, tk), :])

        st = (jnp.zeros((tq, 1), F32), jnp.zeros((tq, SB_DIM), F32))
        for d in reversed(range(per_q)):
            row = lax.broadcasted_iota(jnp.int32, (tq, tk), 0)
            col = lax.broadcasted_iota(jnp.int32, (tq, tk), 1) + d * tk
            st = key_tile(pl.multiple_of(q0 + d * tk, tk), st, mask=col < row)

        def older(step, st):
            for d in reversed(range(per_q)):
                st = key_tile(pl.multiple_of(q0 - (step + 1) * tq + d * tk, tk), st)
            return st

        _, acc = lax.fori_loop(0, qi, older, st)
        o_ref[pl.ds(q0, tq), :] = acc
        return carry

    lax.fori_loop(0, seq // tq, q_tile, 0)


def _sb_prompt(proj, *, batch, seq, tq, tk):
    assert seq % tq == 0 and tq % tk == 0 and seq % ROW_CHUNK == 0
    vmem = 2 * 3 * seq * SB_DIM * 4 + 2 * seq * SB_DIM * 4 + 3 * seq * SB_DIM * 2 + 12 * tq * tk * 4
    col = lambda c0: pl.BlockSpec((seq, SB_DIM), lambda b, h: (b, c0 // SB_DIM + h))
    return pl.pallas_call(
        functools.partial(_sb_prompt_kernel, seq=seq, tq=tq, tk=tk),
        out_shape=jax.ShapeDtypeStruct((batch * seq, SB_WIDTH), F32),
        grid=(batch, SB_HEADS),
        in_specs=[col(PROJ_SQ), col(PROJ_SK), col(PROJ_SV)],
        out_specs=pl.BlockSpec((seq, SB_DIM), lambda b, h: (b, h)),
        scratch_shapes=[pltpu.VMEM((seq, SB_DIM), BF16)] * 3,
        compiler_params=_cparams(("parallel", "parallel"), vmem),
        name="sb_prompt",
    )(proj, proj, proj)


def _sb_sample_kernel(q_ref, kn_ref, vn_ref, kc_ref, vc_ref, o_ref, decay_s, acc_s, *, nkt, sub, tq):
    step = pl.program_id(1)
    rows = SB_HEADS * tq
    head = lambda h: pl.ds(h * SB_DIM, SB_DIM)
    hrow = lambda h: pl.ds(h * tq, tq)

    def scores(k_of_head):
        return jnp.concatenate(
            [_dot_nt((q_ref[:, head(h)] * SB_SCALE).astype(BF16), k_of_head(h)) for h in range(SB_HEADS)], axis=0)

    def weighted(a, v_of_head):
        return jnp.concatenate(
            [_dot(a[h * tq:(h + 1) * tq], v_of_head(h)) for h in range(SB_HEADS)], axis=0)

    @pl.when(step == 0)
    def _():
        pad = jnp.zeros((V7X_LANES - tq, SB_DIM), BF16)
        z = scores(lambda h: jnp.concatenate([kn_ref[:, head(h)].astype(BF16), pad], axis=0))
        t = lax.broadcasted_iota(jnp.int32, (rows, V7X_LANES), 0) % tq
        s = lax.broadcasted_iota(jnp.int32, (rows, V7X_LANES), 1)
        a, decay = _sb_weights(z, _later_keys_matrix(V7X_LANES), jnp.zeros((rows, 1), F32), mask=s < t)
        acc_s[...] = weighted(a, lambda h: jnp.concatenate([vn_ref[:, head(h)].astype(BF16), pad], axis=0))
        decay_s[...] = decay

    @pl.when(step > 0)
    def _():
        u = _later_keys_matrix(sub)
        decay = decay_s[...]
        acc = acc_s[...]
        for c in reversed(range(kc_ref.shape[2] // (sub * SB_HEADS))):
            keys = lambda h: pl.ds(c * sub * SB_HEADS + h, sub, stride=SB_HEADS)
            a, decay = _sb_weights(scores(lambda h: kc_ref[0, 0, keys(h), :].astype(BF16)), u, decay)
            acc = acc + weighted(a, lambda h: vc_ref[0, 0, keys(h), :].astype(BF16))
        decay_s[...] = decay
        acc_s[...] = acc

    @pl.when(step == nkt)
    def _():
        for h in range(SB_HEADS):
            o_ref[:, head(h)] = acc_s[hrow(h), :]


def _sb_sample(proj, cache_k, cache_v, *, layer, tk, sub):
    _, batch, past_rows, _ = cache_k.shape
    past = past_rows // SB_HEADS
    tq = S_SEQ
    assert past % tk == 0 and tk % sub == 0
    nkt = past // tk
    rows = SB_HEADS * tq
    vmem = (2 * 3 * tq * SB_WIDTH * 4 + 2 * 2 * tk * SB_WIDTH * 4 + 2 * tq * SB_WIDTH * 4
            + rows * (SB_DIM + V7X_LANES) * 4 + 2 * sub * SB_WIDTH * 2 + 16 * rows * sub * 4 + sub * sub * 2)
    newest_first = lambda b, s: (layer, b, nkt - jnp.maximum(s, 1), 0)
    col = lambda c0: pl.BlockSpec((tq, SB_WIDTH), lambda b, s: (b, c0 // SB_WIDTH))
    return pl.pallas_call(
        functools.partial(_sb_sample_kernel, nkt=nkt, sub=sub, tq=tq),
        out_shape=jax.ShapeDtypeStruct((batch * tq, SB_WIDTH), F32),
        grid=(batch, nkt + 1),
        in_specs=[col(PROJ_SQ), col(PROJ_SK), col(PROJ_SV),
                  pl.BlockSpec((1, 1, tk * SB_HEADS, SB_DIM), newest_first),
                  pl.BlockSpec((1, 1, tk * SB_HEADS, SB_DIM), newest_first)],
        out_specs=pl.BlockSpec((tq, SB_WIDTH), lambda b, s: (b, 0)),
        scratch_shapes=[pltpu.VMEM((rows, 1), F32), pltpu.VMEM((rows, SB_DIM), F32)],
        compiler_params=_cparams(("parallel", "arbitrary"), vmem),
        name="sb_sample",
    )(proj, proj, proj, cache_k, cache_v)


def _sb_rows_kernel(k_ref, v_ref, ko_ref, vo_ref, *, tt):
    for src, dst in ((k_ref, ko_ref), (v_ref, vo_ref)):
        for h in range(SB_HEADS):
            dst[pl.ds(h, tt, stride=SB_HEADS), :] = src[:, pl.ds(h * SB_DIM, SB_DIM)]


def _sb_rows(proj, *, tt):
    m = proj.shape[0]
    assert m % tt == 0
    vmem = 2 * 2 * 2 * tt * SB_WIDTH * 4
    col = lambda c0: pl.BlockSpec((tt, SB_WIDTH), lambda i: (i, c0 // SB_WIDTH))
    out = jax.ShapeDtypeStruct((m * SB_HEADS, SB_DIM), F32)
    return pl.pallas_call(
        functools.partial(_sb_rows_kernel, tt=tt),
        out_shape=(out, out),
        grid=(m // tt,),
        in_specs=[col(PROJ_SK), col(PROJ_SV)],
        out_specs=(pl.BlockSpec((tt * SB_HEADS, SB_DIM), lambda i: (i, 0)),) * 2,
        compiler_params=_cparams(("parallel",), vmem),
        name="sb_rows",
    )(proj, proj)


def _pool_kernel(u_ref, hist_ref, w_ref, scale_ref, o_ref, carry_s, *, tt, pos0):
    ti = pl.program_id(1)
    halo = carry_s.shape[0]

    @pl.when(ti == 0)
    def _():
        carry_s[...] = hist_ref[0]

    pos = pos0 + ti * tt + lax.broadcasted_iota(jnp.int32, (tt, 1), 0)
    for g, w in enumerate(POOL_WINDOWS):
        cols = pl.ds(g * POOL_GROUP, POOL_GROUP)
        tok = u_ref[:, cols]
        s = jnp.concatenate([carry_s[:, cols], tok], axis=0)
        shift = 1
        while shift < w:
            s = s + pltpu.roll(s, shift, 0)
            shift *= 2
        cnt = jnp.minimum(pos + 1, w).astype(F32)
        pooled = s[halo:] / cnt - tok
        o_ref[:, cols] = (_dot(pooled.astype(BF16), w_ref[g]) * scale_ref[:, cols]).astype(BF16)
    carry_s[...] = u_ref[pl.ds(tt - halo, halo), :]


def _pool(proj, hist, w_pool, scale, *, batch, seq, tt, pos0):
    halo = hist.shape[1]
    assert seq % tt == 0 and tt >= halo and halo > max(POOL_WINDOWS) - 1
    nt = seq // tt
    vmem = (2 * tt * POOL_WIDTH * 4 + 2 * halo * POOL_WIDTH * 4 + 2 * POOL_WIDTH * POOL_GROUP * 2
            + 2 * tt * POOL_WIDTH * 2 + halo * POOL_WIDTH * 4 + 8 * (tt + halo) * POOL_GROUP * 4)
    return pl.pallas_call(
        functools.partial(_pool_kernel, tt=tt, pos0=pos0),
        out_shape=jax.ShapeDtypeStruct((batch * seq, POOL_WIDTH), BF16),
        grid=(batch, nt),
        in_specs=[
            pl.BlockSpec((tt, POOL_WIDTH), lambda b, t: (b * nt + t, PROJ_U // POOL_WIDTH)),
            pl.BlockSpec((1, halo, POOL_WIDTH), lambda b, t: (b, 0, 0)),
            pl.BlockSpec((len(POOL_WINDOWS), POOL_GROUP, POOL_GROUP), lambda b, t: (0, 0, 0)),
            pl.BlockSpec((1, POOL_WIDTH), lambda b, t: (0, 0)),
        ],
        out_specs=pl.BlockSpec((tt, POOL_WIDTH), lambda b, t: (b * nt + t, 0)),
        scratch_shapes=[pltpu.VMEM((halo, POOL_WIDTH), F32)],
        compiler_params=_cparams(("parallel", "arbitrary"), vmem),
        name="pool",
    )(proj, hist, w_pool, scale)


def _outproj_kernel(mla_ref, pool_ref, sb_ref, x_ref, gm_ref, gs_ref, w_ref, o_ref, mix_s, *, rows):
    wm = mla_ref.shape[1]
    wp = pool_ref.shape[1]

    @pl.when(pl.program_id(1) == 0)
    def _():
        def prep(r):
            sl = pl.ds(r, ROW_CHUNK)
            mix_s[sl, pl.ds(0, wm)] = _rms(mla_ref[sl, :], gm_ref[...]).astype(BF16)
            mix_s[sl, pl.ds(wm, wp)] = pool_ref[sl, :]
            mix_s[sl, pl.ds(wm + wp, sb_ref.shape[1])] = _rms(sb_ref[sl, :], gs_ref[...]).astype(BF16)
        _row_loop(rows, prep)

    o_ref[...] = x_ref[...] + _dot(mix_s[...], w_ref[...])


def _outproj(o_mla, o_pool, o_sb, x, g_mla, g_sb, w, *, tm, tn):
    m, d = x.shape
    wm, wp, ws = o_mla.shape[1], o_pool.shape[1], o_sb.shape[1]
    k = wm + wp + ws
    assert m % tm == 0 and d % tn == 0 and w.shape == (k, d)
    vmem = (2 * tm * (wm + ws) * 4 + 2 * tm * wp * 2 + 2 * 2 * tm * tn * 4 + 2 * k * tn * 2
            + tm * k * 2 + 2 * tm * tn * 4)
    full = lambda n: pl.BlockSpec((tm, n), lambda i, j: (i, 0))
    return pl.pallas_call(
        functools.partial(_outproj_kernel, rows=tm),
        out_shape=jax.ShapeDtypeStruct((m, d), F32),
        grid=(m // tm, d // tn),
        in_specs=[full(wm), full(wp), full(ws),
                  pl.BlockSpec((tm, tn), lambda i, j: (i, j)),
                  pl.BlockSpec((1, wm), lambda i, j: (0, 0)),
                  pl.BlockSpec((1, ws), lambda i, j: (0, 0)),
                  pl.BlockSpec((k, tn), lambda i, j: (0, j))],
        out_specs=pl.BlockSpec((tm, tn), lambda i, j: (i, j)),
        scratch_shapes=[pltpu.VMEM((tm, k), BF16)],
        compiler_params=_cparams(("parallel", "arbitrary"), vmem),
        name="outproj",
    )(o_mla, o_pool, o_sb, x, g_mla, g_sb, w)


def _rope_tables(pos):
    half = MLA_ROPE // 2
    inv = 1.0 / (ROPE_THETA ** (jnp.arange(half, dtype=F32) / half))
    ang = pos.astype(F32)[:, None] * inv[None, :]
    cos, sin = jnp.cos(ang), jnp.sin(ang)
    z = lambda n: jnp.zeros((pos.shape[0], n), F32)
    return (jnp.concatenate([cos, cos, z(KPE_PAD - MLA_ROPE)], axis=1),
            jnp.concatenate([-sin, z(KPE_PAD - half)], axis=1),
            jnp.concatenate([z(half), sin, z(KPE_PAD - MLA_ROPE)], axis=1))


def _layer_weights(l, w_in, w_uq, w_uk, w_uv, w_pool, w_out):
    wi = w_in[l]
    c_q, c_kv, k_pe, u, sq, sk, sv = jnp.split(
        wi, [Q_LORA, Q_LORA + KV_LORA, Q_LORA + KV_LORA + MLA_ROPE,
             Q_LORA + KV_LORA + MLA_ROPE + POOL_WIDTH,
             Q_LORA + KV_LORA + MLA_ROPE + POOL_WIDTH + SB_WIDTH,
             Q_LORA + KV_LORA + MLA_ROPE + POOL_WIDTH + 2 * SB_WIDTH], axis=1)
    w_in_p = jnp.concatenate(
        [c_q, u, sq, sk, sv, c_kv, k_pe, jnp.zeros((D_MODEL, KPE_PAD - MLA_ROPE), wi.dtype)], axis=1).astype(BF16)
    wuq = w_uq[l].reshape(Q_LORA, MLA_HEADS, MLA_NOPE + MLA_ROPE)
    wuq = jnp.pad(wuq, ((0, 0), (0, 0), (0, QK_PAD - MLA_NOPE - MLA_ROPE)))
    return dict(
        w_in=w_in_p,
        wuq=wuq.transpose(1, 0, 2).astype(BF16),
        wuk=w_uk[l].transpose(1, 0, 2).astype(BF16),
        wuv=w_uv[l].transpose(1, 0, 2).astype(BF16),
        w_pool=w_pool[l].astype(BF16),
        w_out=w_out[l].astype(BF16),
    )


def kernel(x_prompt, x_sample, cache_ckv, cache_kpe, state_pool, cache_sb_k, cache_sb_v, g_ffn1, w1_gate, w1_up, w1_down, g_mix, w_in, g_qnorm, w_uq, g_kvnorm, w_uk, w_uv, w_pool, pool_scale, g_mla_out, g_sb_out, w_out, g_ffn2, w2_gate, w2_up, w2_down, g_final):
    mp, ms = P_BATCH * P_SEQ, S_BATCH * S_SEQ
    tm_p, tf = 512, 256
    row = lambda g: g.reshape(1, -1)
    gfin = row(g_final)

    rope_p = _rope_tables(jnp.tile(jnp.arange(P_SEQ), P_BATCH))
    rope_s = _rope_tables(PAST_LEN + jnp.tile(jnp.arange(S_SEQ), S_BATCH))
    hist_p = jnp.zeros((P_BATCH, POOL_HIST + 1, POOL_WIDTH), F32)
    cache_k = cache_sb_k.reshape(DEPTH, S_BATCH, PAST_LEN * SB_HEADS, SB_DIM)
    cache_v = cache_sb_v.reshape(DEPTH, S_BATCH, PAST_LEN * SB_HEADS, SB_DIM)

    hp = x_prompt.reshape(mp, D_MODEL)
    hs = x_sample.reshape(ms, D_MODEL)
    outs = {k: [] for k in ("p_ckv", "p_kpe", "p_pool", "p_sbk", "p_sbv",
                            "s_ckv", "s_kpe", "s_pool", "s_sbk", "s_sbv")}

    def record(tag, batch, seq, proj, ckv, kpe, sbk, sbv):
        u = proj[:, PROJ_U:PROJ_U + POOL_WIDTH].reshape(batch, seq, POOL_WIDTH)
        outs[tag + "_ckv"].append(ckv.reshape(batch, seq, KV_LORA))
        outs[tag + "_kpe"].append(kpe[:, :MLA_ROPE].reshape(batch, seq, MLA_ROPE))
        outs[tag + "_pool"].append(u[:, seq - POOL_HIST:])
        outs[tag + "_sbk"].append(sbk.reshape(batch, seq, SB_HEADS, SB_DIM))
        outs[tag + "_sbv"].append(sbv.reshape(batch, seq, SB_HEADS, SB_DIM))

    for l in range(DEPTH):
        w = _layer_weights(l, w_in, w_uq, w_uk, w_uv, w_pool, w_out)
        last = l == DEPTH - 1
        gq, gkv = row(g_qnorm[l]), row(g_kvnorm[l])
        pscale = row(pool_scale[l])
        hist_s = jnp.pad(state_pool[l], ((0, 0), (1, 0), (0, 0)))

        def mixer_inputs(h, tm, rope):
            proj = _inproj(h, row(g_mix[l]), w["w_in"], tm=tm, tn=PROJ_W // 5)
            return proj, _mla_prep(proj, *rope, gq, gkv, w["wuq"], w["wuk"], w["wuv"], tm=tm, hb=4)

        def mix(h, tm, o_mla, o_pool, o_sb):
            return _outproj(o_mla, o_pool, o_sb, h, row(g_mla_out[l]), row(g_sb_out[l]), w["w_out"], tm=tm, tn=1024)

        hs, *w1 = _ffn(hs, row(g_ffn1[l]), w1_gate, w1_up, w1_down, gfin, tm=ms, tf=tf // 2,
                       final_norm=False, cast_layer=l)
        proj, (ckv, kpe, q, _, _) = mixer_inputs(hs, ms, rope_s)
        o_mla = _mla_sample(q, w["wuk"], w["wuv"], cache_ckv, cache_kpe, ckv, kpe, layer=l, row0=0, tk=1024)
        o_sb = _sb_sample(proj, cache_k, cache_v, layer=l, tk=1024, sub=256)
        o_pool = _pool(proj, hist_s, w["w_pool"], pscale, batch=S_BATCH, seq=S_SEQ, tt=S_SEQ, pos0=PAST_LEN)
        hs = mix(hs, ms, o_mla, o_pool, o_sb)
        hs, *w2 = _ffn(hs, row(g_ffn2[l]), w2_gate, w2_up, w2_down, gfin, tm=ms, tf=tf // 2,
                       final_norm=last, cast_layer=l)
        record("s", S_BATCH, S_SEQ, proj, ckv, kpe, *_sb_rows(proj, tt=ms))

        hp = _ffn(hp, row(g_ffn1[l]), *w1, gfin, tm=tm_p, tf=tf, final_norm=False)
        proj, (ckv, kpe, q, k, v) = mixer_inputs(hp, tm_p, rope_p)
        o_mla = _mla_prompt(q, k, v, batch=P_BATCH, seq=P_SEQ, tile=512)
        o_sb = _sb_prompt(proj, batch=P_BATCH, seq=P_SEQ, tq=512, tk=256)
        o_pool = _pool(proj, hist_p, w["w_pool"], pscale, batch=P_BATCH, seq=P_SEQ, tt=512, pos0=0)
        hp = mix(hp, tm_p, o_mla, o_pool, o_sb)
        hp = _ffn(hp, row(g_ffn2[l]), *w2, gfin, tm=tm_p, tf=tf, final_norm=last)
        record("p", P_BATCH, P_SEQ, proj, ckv, kpe, *_sb_rows(proj, tt=tm_p))

    st = lambda k: jnp.stack(outs[k])
    return (hp.reshape(P_BATCH, P_SEQ, D_MODEL), hs.reshape(S_BATCH, S_SEQ, D_MODEL),
            st("p_ckv"), st("p_kpe"), st("p_pool"), st("p_sbk"), st("p_sbv"),
            st("s_ckv"), st("s_kpe"), st("s_pool"), st("s_sbk"), st("s_sbv"))
```

```python
import functools
import math

import jax
import jax.numpy as jnp
from jax import lax
from jax.experimental import pallas as pl
from jax.experimental.pallas import tpu as pltpu

F32 = jnp.float32
BF16 = jnp.bfloat16

D_MODEL = 4096
D_FF = 11008
DEPTH = 2
P_BATCH, P_SEQ = 4, 2048
S_BATCH, S_SEQ, PAST_LEN = 16, 16, 4096
CHUNK = 64
MLA_HEADS, MLA_NOPE, MLA_ROPE, MLA_V = 16, 128, 64, 128
Q_LORA, KV_LORA = 1024, 512
ROPE_THETA = 10000.0
MLA_SCALE = 1.0 / math.sqrt(MLA_NOPE + MLA_ROPE)
POOL_WINDOWS = (2, 4, 8, 16)
POOL_GROUP = 256
POOL_WIDTH = 1024
POOL_HIST = 15
SB_HEADS, SB_DIM = 8, 128
SB_WIDTH = 1024
SB_SCALE = 1.0 / math.sqrt(SB_DIM)
EPS = 1e-6

V7X_LANES = 128
V7X_VMEM_LIMIT_BYTES = 60000 * 1024
VALUE_SPILL_BYTES = 8 * 1024 * 1024

PROJ_CQ = 0
PROJ_U = 1024
PROJ_SQ = 2048
PROJ_SK = 3072
PROJ_SV = 4096
PROJ_CKV = 5120
PROJ_KPE = 5632
PROJ_W = 5760
KPE_PAD = 128
QK_PAD = 256

ROW_CHUNK = 64
NEG_BIG = -1e30


def _cparams(semantics, vmem_bytes):
    assert vmem_bytes <= V7X_VMEM_LIMIT_BYTES, vmem_bytes
    limit = min(V7X_VMEM_LIMIT_BYTES, vmem_bytes + VALUE_SPILL_BYTES)
    return pltpu.CompilerParams(dimension_semantics=semantics, vmem_limit_bytes=int(limit))


def _rms(x, g):
    ms = jnp.mean(x * x, axis=-1, keepdims=True)
    return (x * lax.rsqrt(ms + EPS)) * g


def _dot(a, b):
    return jnp.dot(a, b, preferred_element_type=F32)


def _dot_nt(a, b):
    return lax.dot_general(a, b, (((1,), (1,)), ((), ())), preferred_element_type=F32)


def _row_loop(rows, fn):
    def body(c, carry):
        fn(pl.multiple_of(c * ROW_CHUNK, ROW_CHUNK))
        return carry
    lax.fori_loop(0, rows // ROW_CHUNK, body, 0)


def _ffn_step(x_ref, g_ref, wg_ref, wu_ref, wd_ref, gf_ref, o_ref, xn_ref, *, nj, rows, final_norm):
    j = pl.program_id(1)

    @pl.when(j == 0)
    def _():
        def prep(r):
            x = x_ref[pl.ds(r, ROW_CHUNK), :]
            xn_ref[pl.ds(r, ROW_CHUNK), :] = _rms(x, g_ref[...]).astype(BF16)
            o_ref[pl.ds(r, ROW_CHUNK), :] = jnp.zeros((ROW_CHUNK, x.shape[1]), F32)
        _row_loop(rows, prep)

    xn = xn_ref[...]
    gate = _dot(xn, wg_ref[...])
    up = _dot(xn, wu_ref[...])
    h = (gate * (1.0 / (1.0 + jnp.exp(-gate))) * up).astype(BF16)
    o_ref[...] += _dot(h, wd_ref[...])

    @pl.when(j == nj - 1)
    def _():
        def fin(r):
            y = x_ref[pl.ds(r, ROW_CHUNK), :] + 0.5 * o_ref[pl.ds(r, ROW_CHUNK), :]
            if final_norm:
                y = _rms(y, gf_ref[...])
            o_ref[pl.ds(r, ROW_CHUNK), :] = y
        _row_loop(rows, fin)


def _ffn_kernel(x_ref, g_ref, wg_ref, wu_ref, wd_ref, gf_ref, o_ref, xn_ref, **kw):
    _ffn_step(x_ref, g_ref, wg_ref, wu_ref, wd_ref, gf_ref, o_ref, xn_ref, **kw)


def _ffn_cast_kernel(x_ref, g_ref, wg_ref, wu_ref, wd_ref, gf_ref, o_ref, wg_o, wu_o, wd_o, xn_ref, **kw):
    for src, dst in ((wg_ref, wg_o), (wu_ref, wu_o), (wd_ref, wd_o)):
        dst[...] = src[0].astype(BF16)
    _ffn_step(x_ref, g_ref, wg_o, wu_o, wd_o, gf_ref, o_ref, xn_ref, **kw)


def _ffn(x, g, wg, wu, wd, g_final, *, tm, tf, final_norm, cast_layer=None):
    m, d = x.shape
    cast = cast_layer is not None
    f = wg.shape[-1]
    assert m % tm == 0 and f % tf == 0 and tm % ROW_CHUNK == 0 and (not cast or m == tm)
    nj = f // tf
    vmem = (2 * tm * d * 4
            + tm * d * 2
            + 2 * tm * d * 4
            + 2 * 3 * d * tf * (6 if cast else 2)
            + tm * d * 4
            + 4 * tm * tf * 4)
    w_specs = [pl.BlockSpec((d, tf), lambda i, j: (0, j)),
               pl.BlockSpec((d, tf), lambda i, j: (0, j)),
               pl.BlockSpec((tf, d), lambda i, j: (j, 0))]
    w_in_specs = w_specs
    if cast:
        w_in_specs = [pl.BlockSpec((1, d, tf), lambda i, j: (cast_layer, 0, j)),
                      pl.BlockSpec((1, d, tf), lambda i, j: (cast_layer, 0, j)),
                      pl.BlockSpec((1, tf, d), lambda i, j: (cast_layer, j, 0))]
    x_out = jax.ShapeDtypeStruct((m, d), F32)
    x_spec = pl.BlockSpec((tm, d), lambda i, j: (i, 0))
    return pl.pallas_call(
        functools.partial(_ffn_cast_kernel if cast else _ffn_kernel, nj=nj, rows=tm, final_norm=final_norm),
        out_shape=(x_out,) + tuple(jax.ShapeDtypeStruct(w.shape[1:], BF16) for w in (wg, wu, wd)) if cast else x_out,
        grid=(m // tm, nj),
        in_specs=[
            pl.BlockSpec((tm, d), lambda i, j: (i, 0)),
            pl.BlockSpec((1, d), lambda i, j: (0, 0)),
            *w_in_specs,
            pl.BlockSpec((1, d), lambda i, j: (0, 0)),
        ],
        out_specs=(x_spec, *w_specs) if cast else x_spec,
        scratch_shapes=[pltpu.VMEM((tm, d), BF16)],
        compiler_params=_cparams(("parallel", "arbitrary"), vmem),
        name="ffn_cast" if cast else "ffn",
    )(x, g, wg, wu, wd, g_final)


def _inproj_kernel(x_ref, g_ref, w_ref, o_ref, xn_ref, *, rows):
    @pl.when(pl.program_id(1) == 0)
    def _():
        def prep(r):
            xn_ref[pl.ds(r, ROW_CHUNK), :] = _rms(x_ref[pl.ds(r, ROW_CHUNK), :], g_ref[...]).astype(BF16)
        _row_loop(rows, prep)

    o_ref[...] = _dot(xn_ref[...], w_ref[...])


def _inproj(x, g, w, *, tm, tn):
    m, d = x.shape
    n = w.shape[1]
    assert m % tm == 0 and n % tn == 0
    vmem = 2 * tm * d * 4 + tm * d * 2 + 2 * d * tn * 2 + 3 * tm * tn * 4
    return pl.pallas_call(
        functools.partial(_inproj_kernel, rows=tm),
        out_shape=jax.ShapeDtypeStruct((m, n), F32),
        grid=(m // tm, n // tn),
        in_specs=[
            pl.BlockSpec((tm, d), lambda i, j: (i, 0)),
            pl.BlockSpec((1, d), lambda i, j: (0, 0)),
            pl.BlockSpec((d, tn), lambda i, j: (0, j)),
        ],
        out_specs=pl.BlockSpec((tm, tn), lambda i, j: (i, j)),
        scratch_shapes=[pltpu.VMEM((tm, d), BF16)],
        compiler_params=_cparams(("parallel", "arbitrary"), vmem),
        name="inproj",
    )(x, g, w)


def _rope_tile(t, c, sa, sb):
    return t * c + pltpu.roll(t, KPE_PAD - MLA_ROPE // 2, 1) * sa + pltpu.roll(t, MLA_ROPE // 2, 1) * sb


def _mla_prep_kernel(cq_ref, ckv_ref, kpe_ref, c_ref, sa_ref, sb_ref, gq_ref, gkv_ref,
                     wuq_ref, wuk_ref, wuv_ref,
                     ckv_o, kpe_o, q_o, k_o, v_o, cqn_s, ckv_s, kpe_s, *, rows, hb):
    @pl.when(pl.program_id(1) == 0)
    def _():
        def prep(r):
            sl = pl.ds(r, ROW_CHUNK)
            cqn_s[sl, :] = _rms(cq_ref[sl, :], gq_ref[...]).astype(BF16)
            ckv = _rms(ckv_ref[sl, :], gkv_ref[...])
            ckv_o[sl, :] = ckv
            ckv_s[sl, :] = ckv.astype(BF16)
            kpe = _rope_tile(kpe_ref[sl, :], c_ref[sl, :], sa_ref[sl, :], sb_ref[sl, :])
            kpe_o[sl, :] = kpe
            kpe_s[sl, :] = kpe.astype(BF16)
        _row_loop(rows, prep)

    for hh in range(hb):
        qf = _dot(cqn_s[...], wuq_ref[hh]) * MLA_SCALE
        q_o[hh, :, pl.ds(0, MLA_NOPE)] = qf[:, :MLA_NOPE].astype(BF16)
        q_o[hh, :, pl.ds(MLA_NOPE, KPE_PAD)] = _rope_tile(
            qf[:, MLA_NOPE:], c_ref[...], sa_ref[...], sb_ref[...]).astype(BF16)
        k_o[hh, :, pl.ds(0, MLA_NOPE)] = _dot(ckv_s[...], wuk_ref[hh]).astype(BF16)
        k_o[hh, :, pl.ds(MLA_NOPE, KPE_PAD)] = kpe_s[...]
        v_o[hh] = _dot(ckv_s[...], wuv_ref[hh]).astype(BF16)


def _mla_prep(proj, rope_c, rope_sa, rope_sb, gq, gkv, wuq, wuk, wuv, *, tm, hb):
    m = proj.shape[0]
    nh = wuq.shape[0]
    assert m % tm == 0 and nh % hb == 0
    vmem = (2 * tm * (Q_LORA + KV_LORA + 4 * KPE_PAD) * 4
            + 2 * hb * (Q_LORA * QK_PAD + 2 * KV_LORA * MLA_NOPE) * 2
            + 2 * tm * (KV_LORA + KPE_PAD) * 4
            + 2 * hb * tm * (2 * QK_PAD + MLA_V) * 2
            + tm * (Q_LORA + KV_LORA + KPE_PAD) * 2
            + 6 * tm * QK_PAD * 4)
    row_blk = lambda w, c: pl.BlockSpec((tm, w), lambda i, h: (i, c))
    head_w = lambda k, n: pl.BlockSpec((hb, k, n), lambda i, h: (h, 0, 0))
    head_o = lambda n: pl.BlockSpec((hb, tm, n), lambda i, h: (h, i, 0))
    return pl.pallas_call(
        functools.partial(_mla_prep_kernel, rows=tm, hb=hb),
        out_shape=(
            jax.ShapeDtypeStruct((m, KV_LORA), F32),
            jax.ShapeDtypeStruct((m, KPE_PAD), F32),
            jax.ShapeDtypeStruct((nh, m, QK_PAD), BF16),
            jax.ShapeDtypeStruct((nh, m, QK_PAD), BF16),
            jax.ShapeDtypeStruct((nh, m, MLA_V), BF16),
        ),
        grid=(m // tm, nh // hb),
        in_specs=[
            row_blk(Q_LORA, PROJ_CQ // Q_LORA),
            row_blk(KV_LORA, PROJ_CKV // KV_LORA),
            row_blk(KPE_PAD, PROJ_KPE // KPE_PAD),
            row_blk(KPE_PAD, 0), row_blk(KPE_PAD, 0), row_blk(KPE_PAD, 0),
            pl.BlockSpec((1, Q_LORA), lambda i, h: (0, 0)),
            pl.BlockSpec((1, KV_LORA), lambda i, h: (0, 0)),
            head_w(Q_LORA, QK_PAD), head_w(KV_LORA, MLA_NOPE), head_w(KV_LORA, MLA_V),
        ],
        out_specs=(row_blk(KV_LORA, 0), row_blk(KPE_PAD, 0), head_o(QK_PAD), head_o(QK_PAD), head_o(MLA_V)),
        scratch_shapes=[pltpu.VMEM((tm, Q_LORA), BF16), pltpu.VMEM((tm, KV_LORA), BF16),
                        pltpu.VMEM((tm, KPE_PAD), BF16)],
        compiler_params=_cparams(("parallel", "arbitrary"), vmem),
        name="mla_prep",
    )(proj, proj, proj, rope_c, rope_sa, rope_sb, gq, gkv, wuq, wuk, wuv)


def _softmax_step(s, v, m, l, acc):
    m_new = jnp.maximum(m, jnp.max(s, axis=-1, keepdims=True))
    p = jnp.exp(s - m_new)
    alpha = jnp.exp(m - m_new)
    l = alpha * l + jnp.sum(p, axis=-1, keepdims=True)
    acc = alpha * acc + _dot(p.astype(BF16), v)
    return m_new, l, acc


def _mla_prompt_kernel(q_ref, k_ref, v_ref, o_ref, *, seq, tile, hp):
    def q_tile(qi, carry):
        q0 = pl.multiple_of(qi * tile, tile)
        qs = [q_ref[h, pl.ds(q0, tile), :] for h in range(hp)]

        def scores(h, k0):
            return _dot_nt(qs[h], k_ref[h, pl.ds(k0, tile), :])

        def k_tile(ki, sts):
            k0 = pl.multiple_of(ki * tile, tile)
            return tuple(_softmax_step(scores(h, k0), v_ref[h, pl.ds(k0, tile), :], *sts[h]) for h in range(hp))

        init = (jnp.full((tile, 1), NEG_BIG, F32), jnp.zeros((tile, 1), F32), jnp.zeros((tile, MLA_V), F32))
        sts = lax.fori_loop(0, qi, k_tile, (init,) * hp)
        row = lax.broadcasted_iota(jnp.int32, (tile, tile), 0)
        col = lax.broadcasted_iota(jnp.int32, (tile, tile), 1)
        visible = col // CHUNK <= row // CHUNK
        for h in range(hp):
            s = jnp.where(visible, scores(h, q0), NEG_BIG)
            _, l, acc = _softmax_step(s, v_ref[h, pl.ds(q0, tile), :], *sts[h])
            o_ref[pl.ds(q0, tile), pl.ds(h * MLA_V, MLA_V)] = acc / l
        return carry

    lax.fori_loop(0, seq // tile, q_tile, 0)


def _mla_prompt(q, k, v, *, batch, seq, tile, hp):
    nh = q.shape[0]
    assert seq % tile == 0 and tile % CHUNK == 0 and nh % hp == 0
    vmem = hp * (2 * seq * (2 * QK_PAD + MLA_V) * 2 + 2 * seq * MLA_V * 4 + 8 * tile * tile * 4)
    return pl.pallas_call(
        functools.partial(_mla_prompt_kernel, seq=seq, tile=tile, hp=hp),
        out_shape=jax.ShapeDtypeStruct((batch * seq, nh * MLA_V), F32),
        grid=(batch, nh // hp),
        in_specs=[
            pl.BlockSpec((hp, seq, QK_PAD), lambda b, h: (h, b, 0)),
            pl.BlockSpec((hp, seq, QK_PAD), lambda b, h: (h, b, 0)),
            pl.BlockSpec((hp, seq, MLA_V), lambda b, h: (h, b, 0)),
        ],
        out_specs=pl.BlockSpec((seq, hp * MLA_V), lambda b, h: (b, h)),
        compiler_params=_cparams(("parallel", "parallel"), vmem),
        name="mla_prompt",
    )(q, k, v)


def _mla_sample_kernel(q_ref, wuk_ref, wuv_ref, ckv_ref, kpe_ref, ckvn_ref, kpen_ref, o_ref,
                       qlat_s, qpe_s, m_s, l_s, acc_s, *, nkt, nh, tq):
    kt = pl.program_id(1)
    rows = nh * tq

    @pl.when(kt == 0)
    def _():
        for h in range(nh):
            qh = q_ref[h]
            qlat_s[pl.ds(h * tq, tq), :] = _dot_nt(qh[:, :MLA_NOPE], wuk_ref[h]).astype(BF16)
            qpe_s[pl.ds(h * tq, tq), :] = qh[:, MLA_NOPE:MLA_NOPE + KPE_PAD]
        m_s[...] = jnp.full((rows, 1), NEG_BIG, F32)
        l_s[...] = jnp.zeros((rows, 1), F32)
        acc_s[...] = jnp.zeros((rows, KV_LORA), F32)

    def update(ckv, kpe):
        s = _dot_nt(qlat_s[...], ckv) + _dot_nt(qpe_s[:, pl.ds(0, MLA_ROPE)], kpe)
        m, l, acc = _softmax_step(s, ckv, m_s[...], l_s[...], acc_s[...])
        m_s[...] = m
        l_s[...] = l
        acc_s[...] = acc

    update(ckv_ref[0, 0].astype(BF16), kpe_ref[0, 0].astype(BF16))

    @pl.when(kt == nkt - 1)
    def _():
        update(ckvn_ref[...].astype(BF16), kpen_ref[:, pl.ds(0, MLA_ROPE)].astype(BF16))
        o_lat = (acc_s[...] / l_s[...]).astype(BF16)
        for h in range(nh):
            o_ref[:, pl.ds(h * MLA_V, MLA_V)] = _dot(o_lat[h * tq:(h + 1) * tq], wuv_ref[h])


def _mla_sample(q, wuk, wuv, cache_ckv, cache_kpe, ckv_new, kpe_new, *, layer, row0, tk):
    nh = q.shape[0]
    _, batch, past, _ = cache_ckv.shape
    tq = S_SEQ
    assert past % tk == 0 and row0 % tq == 0
    assert (PAST_LEN + S_SEQ - 1) // CHUNK == PAST_LEN // CHUNK
    nkt = past // tk
    rows = nh * tq
    vmem = (2 * nh * tq * QK_PAD * 2 + 4 * nh * KV_LORA * MLA_NOPE * 2
            + 2 * tk * (KV_LORA + KPE_PAD) * 4 + tk * (KV_LORA + KPE_PAD) * 2
            + 2 * tq * (KV_LORA + KPE_PAD) * 4 + 2 * tq * nh * MLA_V * 4
            + rows * (KV_LORA + KPE_PAD) * 2 + rows * (KV_LORA + 2 * V7X_LANES) * 4
            + 4 * rows * tk * 4)
    rb = row0 // tq
    return pl.pallas_call(
        functools.partial(_mla_sample_kernel, nkt=nkt, nh=nh, tq=tq),
        out_shape=jax.ShapeDtypeStruct((batch * tq, nh * MLA_V), F32),
        grid=(batch, nkt),
        in_specs=[
            pl.BlockSpec((nh, tq, QK_PAD), lambda b, t: (0, rb + b, 0)),
            pl.BlockSpec((nh, KV_LORA, MLA_NOPE), lambda b, t: (0, 0, 0)),
            pl.BlockSpec((nh, KV_LORA, MLA_V), lambda b, t: (0, 0, 0)),
            pl.BlockSpec((1, 1, tk, KV_LORA), lambda b, t: (layer, b, t, 0)),
            pl.BlockSpec((1, 1, tk, MLA_ROPE), lambda b, t: (layer, b, t, 0)),
            pl.BlockSpec((tq, KV_LORA), lambda b, t: (rb + b, 0)),
            pl.BlockSpec((tq, KPE_PAD), lambda b, t: (rb + b, 0)),
        ],
        out_specs=pl.BlockSpec((tq, nh * MLA_V), lambda b, t: (b, 0)),
        scratch_shapes=[pltpu.VMEM((rows, KV_LORA), BF16), pltpu.VMEM((rows, KPE_PAD), BF16),
                        pltpu.VMEM((rows, 1), F32), pltpu.VMEM((rows, 1), F32),
                        pltpu.VMEM((rows, KV_LORA), F32)],
        compiler_params=_cparams(("parallel", "arbitrary"), vmem),
        name="mla_sample",
    )(q, wuk, wuv, cache_ckv, cache_kpe, ckv_new, kpe_new)


def _softplus(z):
    return jnp.maximum(z, 0.0) + jnp.log(1.0 + jnp.exp(-jnp.abs(z)))


def _later_keys_matrix(n):
    row = lax.broadcasted_iota(jnp.int32, (n, n), 0)
    col = lax.broadcasted_iota(jnp.int32, (n, n), 1)
    return (row > col).astype(BF16)


def _later_sum(x, u):
    r = x.shape[0]
    hi = x.astype(BF16)
    lo = (x - hi.astype(F32)).astype(BF16)
    e = _dot(jnp.concatenate([hi, lo], axis=0), u)
    return e[:r] + e[r:]


def _sb_weights(z, u, decay, mask=None):
    sp = _softplus(z)
    d = sp if mask is None else jnp.where(mask, sp, 0.0)
    a = jnp.exp(z - sp - _later_sum(d, u) - decay)
    if mask is not None:
        a = jnp.where(mask, a, 0.0)
    return a.astype(BF16), decay + jnp.sum(d, axis=-1, keepdims=True)


def _sb_prompt_kernel(q_ref, k_ref, v_ref, o_ref, qb, kb, vb, *, seq, tq, tk):
    def cast(r):
        sl = pl.ds(r, ROW_CHUNK)
        qb[sl, :] = (q_ref[sl, :] * SB_SCALE).astype(BF16)
        kb[sl, :] = k_ref[sl, :].astype(BF16)
        vb[sl, :] = v_ref[sl, :].astype(BF16)
    _row_loop(seq, cast)

    u = _later_keys_matrix(tk)
    per_q = tq // tk

    def q_tile(qi, carry):
        q0 = pl.multiple_of(qi * tq, tq)
        q = qb[pl.ds(q0, tq), :]

        def key_tile(k0, st, mask=None):
            decay, acc = st
            a, decay = _sb_weights(_dot_nt(q, kb[pl.ds(k0, tk), :]), u, decay, mask)
            return decay, acc + _dot(a, vb[pl.ds(k0, tk), :])

        st = (jnp.zeros((tq, 1), F32), jnp.zeros((tq, SB_DIM), F32))
        for d in reversed(range(per_q)):
            row = lax.broadcasted_iota(jnp.int32, (tq, tk), 0)
            col = lax.broadcasted_iota(jnp.int32, (tq, tk), 1) + d * tk
            st = key_tile(pl.multiple_of(q0 + d * tk, tk), st, mask=col < row)

        def older(step, st):
            for d in reversed(range(per_q)):
                st = key_tile(pl.multiple_of(q0 - (step + 1) * tq + d * tk, tk), st)
            return st

        _, acc = lax.fori_loop(0, qi, older, st)
        o_ref[pl.ds(q0, tq), :] = acc
        return carry

    lax.fori_loop(0, seq // tq, q_tile, 0)


def _sb_prompt(proj, *, batch, seq, tq, tk):
    assert seq % tq == 0 and tq % tk == 0 and seq % ROW_CHUNK == 0
    vmem = 2 * 3 * seq * SB_DIM * 4 + 2 * seq * SB_DIM * 4 + 3 * seq * SB_DIM * 2 + 12 * tq * tk * 4
    col = lambda c0: pl.BlockSpec((seq, SB_DIM), lambda b, h: (b, c0 // SB_DIM + h))
    return pl.pallas_call(
        functools.partial(_sb_prompt_kernel, seq=seq, tq=tq, tk=tk),
        out_shape=jax.ShapeDtypeStruct((batch * seq, SB_WIDTH), F32),
        grid=(batch, SB_HEADS),
        in_specs=[col(PROJ_SQ), col(PROJ_SK), col(PROJ_SV)],
        out_specs=pl.BlockSpec((seq, SB_DIM), lambda b, h: (b, h)),
        scratch_shapes=[pltpu.VMEM((seq, SB_DIM), BF16)] * 3,
        compiler_params=_cparams(("parallel", "parallel"), vmem),
        name="sb_prompt",
    )(proj, proj, proj)


def _sb_sample_kernel(q_ref, kn_ref, vn_ref, kc_ref, vc_ref, o_ref, decay_s, acc_s, *, nkt, sub, tq):
    step = pl.program_id(1)
    rows = SB_HEADS * tq
    head = lambda h: pl.ds(h * SB_DIM, SB_DIM)
    hrow = lambda h: pl.ds(h * tq, tq)

    def scores(k_of_head):
        return jnp.concatenate(
            [_dot_nt((q_ref[:, head(h)] * SB_SCALE).astype(BF16), k_of_head(h)) for h in range(SB_HEADS)], axis=0)

    def weighted(a, v_of_head):
        return jnp.concatenate(
            [_dot(a[h * tq:(h + 1) * tq], v_of_head(h)) for h in range(SB_HEADS)], axis=0)

    @pl.when(step == 0)
    def _():
        pad = jnp.zeros((V7X_LANES - tq, SB_DIM), BF16)
        z = scores(lambda h: jnp.concatenate([kn_ref[:, head(h)].astype(BF16), pad], axis=0))
        t = lax.broadcasted_iota(jnp.int32, (rows, V7X_LANES), 0) % tq
        s = lax.broadcasted_iota(jnp.int32, (rows, V7X_LANES), 1)
        a, decay = _sb_weights(z, _later_keys_matrix(V7X_LANES), jnp.zeros((rows, 1), F32), mask=s < t)
        acc_s[...] = weighted(a, lambda h: jnp.concatenate([vn_ref[:, head(h)].astype(BF16), pad], axis=0))
        decay_s[...] = decay

    @pl.when(step > 0)
    def _():
        u = _later_keys_matrix(sub)
        decay = decay_s[...]
        acc = acc_s[...]
        for c in reversed(range(kc_ref.shape[2] // (sub * SB_HEADS))):
            keys = lambda h: pl.ds(c * sub * SB_HEADS + h, sub, stride=SB_HEADS)
            a, decay = _sb_weights(scores(lambda h: kc_ref[0, 0, keys(h), :].astype(BF16)), u, decay)
            acc = acc + weighted(a, lambda h: vc_ref[0, 0, keys(h), :].astype(BF16))
        decay_s[...] = decay
        acc_s[...] = acc

    @pl.when(step == nkt)
    def _():
        for h in range(SB_HEADS):
            o_ref[:, head(h)] = acc_s[hrow(h), :]


def _sb_sample(proj, cache_k, cache_v, *, layer, tk, sub):
    _, batch, past_rows, _ = cache_k.shape
    past = past_rows // SB_HEADS
    tq = S_SEQ
    assert past % tk == 0 and tk % sub == 0
    nkt = past // tk
    rows = SB_HEADS * tq
    vmem = (2 * 3 * tq * SB_WIDTH * 4 + 2 * 2 * tk * SB_WIDTH * 4 + 2 * tq * SB_WIDTH * 4
            + rows * (SB_DIM + V7X_LANES) * 4 + 2 * sub * SB_WIDTH * 2 + 16 * rows * sub * 4 + sub * sub * 2)
    newest_first = lambda b, s: (layer, b, nkt - jnp.maximum(s, 1), 0)
    col = lambda c0: pl.BlockSpec((tq, SB_WIDTH), lambda b, s: (b, c0 // SB_WIDTH))
    return pl.pallas_call(
        functools.partial(_sb_sample_kernel, nkt=nkt, sub=sub, tq=tq),
        out_shape=jax.ShapeDtypeStruct((batch * tq, SB_WIDTH), F32),
        grid=(batch, nkt + 1),
        in_specs=[col(PROJ_SQ), col(PROJ_SK), col(PROJ_SV),
                  pl.BlockSpec((1, 1, tk * SB_HEADS, SB_DIM), newest_first),
                  pl.BlockSpec((1, 1, tk * SB_HEADS, SB_DIM), newest_first)],
        out_specs=pl.BlockSpec((tq, SB_WIDTH), lambda b, s: (b, 0)),
        scratch_shapes=[pltpu.VMEM((rows, 1), F32), pltpu.VMEM((rows, SB_DIM), F32)],
        compiler_params=_cparams(("parallel", "arbitrary"), vmem),
        name="sb_sample",
    )(proj, proj, proj, cache_k, cache_v)


def _sb_rows_kernel(*refs, tt, layers):
    srcs, (ko_ref, vo_ref) = refs[:2 * layers], refs[2 * layers:]
    for l in range(layers):
        @pl.when(pl.program_id(0) == l)
        def _():
            for src, dst in ((srcs[2 * l], ko_ref), (srcs[2 * l + 1], vo_ref)):
                for h in range(SB_HEADS):
                    dst[0, pl.ds(h, tt, stride=SB_HEADS), :] = src[:, pl.ds(h * SB_DIM, SB_DIM)]


def _sb_rows(projs, *, tt):
    layers = len(projs)
    m = projs[0].shape[0]
    assert m % tt == 0
    nt = m // tt
    vmem = (2 * layers + 2) * 2 * tt * SB_WIDTH * 4

    def col(l, c0):
        return pl.BlockSpec((tt, SB_WIDTH), lambda g, i: (jnp.clip(i + (g - l) * nt, 0, nt - 1), c0 // SB_WIDTH))

    out = jax.ShapeDtypeStruct((layers, m * SB_HEADS, SB_DIM), F32)
    return pl.pallas_call(
        functools.partial(_sb_rows_kernel, tt=tt, layers=layers),
        out_shape=(out, out),
        grid=(layers, nt),
        in_specs=[col(l, c0) for l in range(layers) for c0 in (PROJ_SK, PROJ_SV)],
        out_specs=(pl.BlockSpec((1, tt * SB_HEADS, SB_DIM), lambda g, i: (g, i, 0)),) * 2,
        compiler_params=_cparams(("arbitrary", "arbitrary"), vmem),
        name="sb_rows",
    )(*[p for p in projs for _ in range(2)])


def _pool_kernel(u_ref, hist_ref, w_ref, scale_ref, o_ref, carry_s, *, tt, pos0):
    ti = pl.program_id(1)
    halo = carry_s.shape[0]

    @pl.when(ti == 0)
    def _():
        carry_s[...] = hist_ref[0]

    pos = pos0 + ti * tt + lax.broadcasted_iota(jnp.int32, (tt, 1), 0)
    for g, w in enumerate(POOL_WINDOWS):
        cols = pl.ds(g * POOL_GROUP, POOL_GROUP)
        tok = u_ref[:, cols]
        s = jnp.concatenate([carry_s[:, cols], tok], axis=0)
        shift = 1
        while shift < w:
            s = s + pltpu.roll(s, shift, 0)
            shift *= 2
        cnt = jnp.minimum(pos + 1, w).astype(F32)
        pooled = s[halo:] / cnt - tok
        o_ref[:, cols] = (_dot(pooled.astype(BF16), w_ref[g]) * scale_ref[:, cols]).astype(BF16)
    carry_s[...] = u_ref[pl.ds(tt - halo, halo), :]


def _pool(proj, hist, w_pool, scale, *, batch, seq, tt, pos0):
    halo = hist.shape[1]
    assert seq % tt == 0 and tt >= halo and halo > max(POOL_WINDOWS) - 1
    nt = seq // tt
    vmem = (2 * tt * POOL_WIDTH * 4 + 2 * halo * POOL_WIDTH * 4 + 2 * POOL_WIDTH * POOL_GROUP * 2
            + 2 * tt * POOL_WIDTH * 2 + halo * POOL_WIDTH * 4 + 8 * (tt + halo) * POOL_GROUP * 4)
    return pl.pallas_call(
        functools.partial(_pool_kernel, tt=tt, pos0=pos0),
        out_shape=jax.ShapeDtypeStruct((batch * seq, POOL_WIDTH), BF16),
        grid=(batch, nt),
        in_specs=[
            pl.BlockSpec((tt, POOL_WIDTH), lambda b, t: (b * nt + t, PROJ_U // POOL_WIDTH)),
            pl.BlockSpec((1, halo, POOL_WIDTH), lambda b, t: (b, 0, 0)),
            pl.BlockSpec((len(POOL_WINDOWS), POOL_GROUP, POOL_GROUP), lambda b, t: (0, 0, 0)),
            pl.BlockSpec((1, POOL_WIDTH), lambda b, t: (0, 0)),
        ],
        out_specs=pl.BlockSpec((tt, POOL_WIDTH), lambda b, t: (b * nt + t, 0)),
        scratch_shapes=[pltpu.VMEM((halo, POOL_WIDTH), F32)],
        compiler_params=_cparams(("parallel", "arbitrary"), vmem),
        name="pool",
    )(proj, hist, w_pool, scale)


def _outproj_kernel(mla_ref, pool_ref, sb_ref, x_ref, gm_ref, gs_ref, w_ref, o_ref, mix_s, *, rows):
    wm = mla_ref.shape[1]
    wp = pool_ref.shape[1]

    @pl.when(pl.program_id(1) == 0)
    def _():
        def prep(r):
            sl = pl.ds(r, ROW_CHUNK)
            mix_s[sl, pl.ds(0, wm)] = _rms(mla_ref[sl, :], gm_ref[...]).astype(BF16)
            mix_s[sl, pl.ds(wm, wp)] = pool_ref[sl, :]
            mix_s[sl, pl.ds(wm + wp, sb_ref.shape[1])] = _rms(sb_ref[sl, :], gs_ref[...]).astype(BF16)
        _row_loop(rows, prep)

    o_ref[...] = x_ref[...] + _dot(mix_s[...], w_ref[...])


def _outproj(o_mla, o_pool, o_sb, x, g_mla, g_sb, w, *, tm, tn):
    m, d = x.shape
    wm, wp, ws = o_mla.shape[1], o_pool.shape[1], o_sb.shape[1]
    k = wm + wp + ws
    assert m % tm == 0 and d % tn == 0 and w.shape == (k, d)
    vmem = (2 * tm * (wm + ws) * 4 + 2 * tm * wp * 2 + 2 * 2 * tm * tn * 4 + 2 * k * tn * 2
            + tm * k * 2 + 2 * tm * tn * 4)
    full = lambda n: pl.BlockSpec((tm, n), lambda i, j: (i, 0))
    return pl.pallas_call(
        functools.partial(_outproj_kernel, rows=tm),
        out_shape=jax.ShapeDtypeStruct((m, d), F32),
        grid=(m // tm, d // tn),
        in_specs=[full(wm), full(wp), full(ws),
                  pl.BlockSpec((tm, tn), lambda i, j: (i, j)),
                  pl.BlockSpec((1, wm), lambda i, j: (0, 0)),
                  pl.BlockSpec((1, ws), lambda i, j: (0, 0)),
                  pl.BlockSpec((k, tn), lambda i, j: (0, j))],
        out_specs=pl.BlockSpec((tm, tn), lambda i, j: (i, j)),
        scratch_shapes=[pltpu.VMEM((tm, k), BF16)],
        compiler_params=_cparams(("parallel", "arbitrary"), vmem),
        name="outproj",
    )(o_mla, o_pool, o_sb, x, g_mla, g_sb, w)


def _rope_tables(pos):
    half = MLA_ROPE // 2
    inv = 1.0 / (ROPE_THETA ** (jnp.arange(half, dtype=F32) / half))
    ang = pos.astype(F32)[:, None] * inv[None, :]
    cos, sin = jnp.cos(ang), jnp.sin(ang)
    z = lambda n: jnp.zeros((pos.shape[0], n), F32)
    return (jnp.concatenate([cos, cos, z(KPE_PAD - MLA_ROPE)], axis=1),
            jnp.concatenate([-sin, z(KPE_PAD - half)], axis=1),
            jnp.concatenate([z(half), sin, z(KPE_PAD - MLA_ROPE)], axis=1))


def _layer_weights(l, w_in, w_uq, w_uk, w_uv, w_pool, w_out):
    wi = w_in[l]
    c_q, c_kv, k_pe, u, sq, sk, sv = jnp.split(
        wi, [Q_LORA, Q_LORA + KV_LORA, Q_LORA + KV_LORA + MLA_ROPE,
             Q_LORA + KV_LORA + MLA_ROPE + POOL_WIDTH,
             Q_LORA + KV_LORA + MLA_ROPE + POOL_WIDTH + SB_WIDTH,
             Q_LORA + KV_LORA + MLA_ROPE + POOL_WIDTH + 2 * SB_WIDTH], axis=1)
    w_in_p = jnp.concatenate(
        [c_q, u, sq, sk, sv, c_kv, k_pe, jnp.zeros((D_MODEL, KPE_PAD - MLA_ROPE), wi.dtype)], axis=1).astype(BF16)
    wuq = w_uq[l].reshape(Q_LORA, MLA_HEADS, MLA_NOPE + MLA_ROPE)
    wuq = jnp.pad(wuq, ((0, 0), (0, 0), (0, QK_PAD - MLA_NOPE - MLA_ROPE)))
    return dict(
        w_in=w_in_p,
        wuq=wuq.transpose(1, 0, 2).astype(BF16),
        wuk=w_uk[l].transpose(1, 0, 2).astype(BF16),
        wuv=w_uv[l].transpose(1, 0, 2).astype(BF16),
        w_pool=w_pool[l].astype(BF16),
        w_out=w_out[l].astype(BF16),
    )


def kernel(x_prompt, x_sample, cache_ckv, cache_kpe, state_pool, cache_sb_k, cache_sb_v, g_ffn1, w1_gate, w1_up, w1_down, g_mix, w_in, g_qnorm, w_uq, g_kvnorm, w_uk, w_uv, w_pool, pool_scale, g_mla_out, g_sb_out, w_out, g_ffn2, w2_gate, w2_up, w2_down, g_final):
    mp, ms = P_BATCH * P_SEQ, S_BATCH * S_SEQ
    tm_p, tf = 512, 256
    row = lambda g: g.reshape(1, -1)
    gfin = row(g_final)

    rope_p = _rope_tables(jnp.tile(jnp.arange(P_SEQ), P_BATCH))
    rope_s = _rope_tables(PAST_LEN + jnp.tile(jnp.arange(S_SEQ), S_BATCH))
    hist_p = jnp.zeros((P_BATCH, POOL_HIST + 1, POOL_WIDTH), F32)
    cache_k = cache_sb_k.reshape(DEPTH, S_BATCH, PAST_LEN * SB_HEADS, SB_DIM)
    cache_v = cache_sb_v.reshape(DEPTH, S_BATCH, PAST_LEN * SB_HEADS, SB_DIM)

    hp = x_prompt.reshape(mp, D_MODEL)
    hs = x_sample.reshape(ms, D_MODEL)
    outs = {k: [] for k in ("p_proj", "p_ckv", "p_kpe", "p_pool", "s_proj", "s_ckv", "s_kpe", "s_pool")}

    def record(tag, batch, seq, proj, ckv, kpe):
        outs[tag + "_proj"].append(proj)
        outs[tag + "_ckv"].append(ckv.reshape(batch, seq, KV_LORA))
        outs[tag + "_kpe"].append(kpe[:, :MLA_ROPE].reshape(batch, seq, MLA_ROPE))
        outs[tag + "_pool"].append(
            proj.reshape(batch, seq, PROJ_W)[:, seq - POOL_HIST:, PROJ_U:PROJ_U + POOL_WIDTH])

    for l in range(DEPTH):
        w = _layer_weights(l, w_in, w_uq, w_uk, w_uv, w_pool, w_out)
        last = l == DEPTH - 1
        gq, gkv = row(g_qnorm[l]), row(g_kvnorm[l])
        pscale = row(pool_scale[l])
        hist_s = jnp.pad(state_pool[l], ((0, 0), (1, 0), (0, 0)))

        def mixer_inputs(h, tm, rope):
            proj = _inproj(h, row(g_mix[l]), w["w_in"], tm=tm, tn=PROJ_W // 5)
            return proj, _mla_prep(proj, *rope, gq, gkv, w["wuq"], w["wuk"], w["wuv"], tm=tm, hb=8)

        def mix(h, tm, o_mla, o_pool, o_sb):
            return _outproj(o_mla, o_pool, o_sb, h, row(g_mla_out[l]), row(g_sb_out[l]), w["w_out"], tm=tm, tn=1024)

        hs, *w1 = _ffn(hs, row(g_ffn1[l]), w1_gate, w1_up, w1_down, gfin, tm=ms, tf=tf // 2,
                       final_norm=False, cast_layer=l)
        proj, (ckv, kpe, q, _, _) = mixer_inputs(hs, ms, rope_s)
        o_mla = _mla_sample(q, w["wuk"], w["wuv"], cache_ckv, cache_kpe, ckv, kpe, layer=l, row0=0, tk=1024)
        o_sb = _sb_sample(proj, cache_k, cache_v, layer=l, tk=1024, sub=256)
        o_pool = _pool(proj, hist_s, w["w_pool"], pscale, batch=S_BATCH, seq=S_SEQ, tt=S_SEQ, pos0=PAST_LEN)
        hs = mix(hs, ms, o_mla, o_pool, o_sb)
        hs, *w2 = _ffn(hs, row(g_ffn2[l]), w2_gate, w2_up, w2_down, gfin, tm=ms, tf=tf // 2,
                       final_norm=last, cast_layer=l)
        record("s", S_BATCH, S_SEQ, proj, ckv, kpe)

        hp = _ffn(hp, row(g_ffn1[l]), *w1, gfin, tm=tm_p, tf=tf, final_norm=False)
        proj, (ckv, kpe, q, k, v) = mixer_inputs(hp, tm_p, rope_p)
        o_mla = _mla_prompt(q, k, v, batch=P_BATCH, seq=P_SEQ, tile=512, hp=2)
        o_sb = _sb_prompt(proj, batch=P_BATCH, seq=P_SEQ, tq=512, tk=256)
        o_pool = _pool(proj, hist_p, w["w_pool"], pscale, batch=P_BATCH, seq=P_SEQ, tt=512, pos0=0)
        hp = mix(hp, tm_p, o_mla, o_pool, o_sb)
        hp = _ffn(hp, row(g_ffn2[l]), *w2, gfin, tm=tm_p, tf=tf, final_norm=last)
        record("p", P_BATCH, P_SEQ, proj, ckv, kpe)

    st = lambda k: jnp.stack(outs[k])
    heads = lambda a, batch, seq: a.reshape(DEPTH, batch, seq, SB_HEADS, SB_DIM)
    p_sbk, p_sbv = _sb_rows(outs["p_proj"], tt=tm_p)
    s_sbk, s_sbv = _sb_rows(outs["s_proj"], tt=ms)
    return (hp.reshape(P_BATCH, P_SEQ, D_MODEL), hs.reshape(S_BATCH, S_SEQ, D_MODEL),
            st("p_ckv"), st("p_kpe"), st("p_pool"), heads(p_sbk, P_BATCH, P_SEQ), heads(p_sbv, P_BATCH, P_SEQ),
            st("s_ckv"), st("s_kpe"), st("s_pool"), heads(s_sbk, S_BATCH, S_SEQ), heads(s_sbv, S_BATCH, S_SEQ))
```

```python
import functools
import math

import jax
import jax.numpy as jnp
from jax import lax
from jax.experimental import pallas as pl
from jax.experimental.pallas import tpu as pltpu

F32 = jnp.float32
BF16 = jnp.bfloat16

D_MODEL = 4096
D_FF = 11008
DEPTH = 2
P_BATCH, P_SEQ = 4, 2048
S_BATCH, S_SEQ, PAST_LEN = 16, 16, 4096
CHUNK = 64
MLA_HEADS, MLA_NOPE, MLA_ROPE, MLA_V = 16, 128, 64, 128
Q_LORA, KV_LORA = 1024, 512
ROPE_THETA = 10000.0
MLA_SCALE = 1.0 / math.sqrt(MLA_NOPE + MLA_ROPE)
POOL_WINDOWS = (2, 4, 8, 16)
POOL_GROUP = 256
POOL_WIDTH = 1024
POOL_HIST = 15
SB_HEADS, SB_DIM = 8, 128
SB_WIDTH = 1024
SB_SCALE = 1.0 / math.sqrt(SB_DIM)
EPS = 1e-6

V7X_LANES = 128
V7X_VMEM_LIMIT_BYTES = 60000 * 1024
VALUE_SPILL_BYTES = 8 * 1024 * 1024

PROJ_CQ = 0
PROJ_U = 1024
PROJ_SQ = 2048
PROJ_SK = 3072
PROJ_SV = 4096
PROJ_CKV = 5120
PROJ_KPE = 5632
PROJ_W = 5760
KPE_PAD = 128
QK_PAD = 256

ROW_CHUNK = 64
NEG_BIG = -1e30


def _cparams(semantics, vmem_bytes):
    assert vmem_bytes <= V7X_VMEM_LIMIT_BYTES, vmem_bytes
    limit = min(V7X_VMEM_LIMIT_BYTES, vmem_bytes + VALUE_SPILL_BYTES)
    return pltpu.CompilerParams(dimension_semantics=semantics, vmem_limit_bytes=int(limit))


def _rms(x, g):
    ms = jnp.mean(x * x, axis=-1, keepdims=True)
    return (x * lax.rsqrt(ms + EPS)) * g


def _dot(a, b):
    return jnp.dot(a, b, preferred_element_type=F32)


def _dot_nt(a, b):
    return lax.dot_general(a, b, (((1,), (1,)), ((), ())), preferred_element_type=F32)


def _row_loop(rows, fn):
    def body(c, carry):
        fn(pl.multiple_of(c * ROW_CHUNK, ROW_CHUNK))
        return carry
    lax.fori_loop(0, rows // ROW_CHUNK, body, 0)


def _ffn_step(x_ref, g_ref, wg_ref, wu_ref, wd_ref, gf_ref, o_ref, xn_ref, *, nj, rows, final_norm):
    j = pl.program_id(1)

    @pl.when(j == 0)
    def _():
        def prep(r):
            x = x_ref[pl.ds(r, ROW_CHUNK), :]
            xn_ref[pl.ds(r, ROW_CHUNK), :] = _rms(x, g_ref[...]).astype(BF16)
            o_ref[pl.ds(r, ROW_CHUNK), :] = jnp.zeros((ROW_CHUNK, x.shape[1]), F32)
        _row_loop(rows, prep)

    xn = xn_ref[...]
    gate = _dot(xn, wg_ref[...])
    up = _dot(xn, wu_ref[...])
    h = (gate * (1.0 / (1.0 + jnp.exp(-gate))) * up).astype(BF16)
    o_ref[...] += _dot(h, wd_ref[...])

    @pl.when(j == nj - 1)
    def _():
        def fin(r):
            y = x_ref[pl.ds(r, ROW_CHUNK), :] + 0.5 * o_ref[pl.ds(r, ROW_CHUNK), :]
            if final_norm:
                y = _rms(y, gf_ref[...])
            o_ref[pl.ds(r, ROW_CHUNK), :] = y
        _row_loop(rows, fin)


def _ffn_kernel(x_ref, g_ref, wg_ref, wu_ref, wd_ref, gf_ref, o_ref, xn_ref, **kw):
    _ffn_step(x_ref, g_ref, wg_ref, wu_ref, wd_ref, gf_ref, o_ref, xn_ref, **kw)


def _ffn_cast_kernel(x_ref, g_ref, wg_ref, wu_ref, wd_ref, gf_ref, o_ref, wg_o, wu_o, wd_o, xn_ref, **kw):
    for src, dst in ((wg_ref, wg_o), (wu_ref, wu_o), (wd_ref, wd_o)):
        dst[...] = src[0].astype(BF16)
    _ffn_step(x_ref, g_ref, wg_o, wu_o, wd_o, gf_ref, o_ref, xn_ref, **kw)


def _ffn(x, g, wg, wu, wd, g_final, *, tm, tf, final_norm, cast_layer=None):
    m, d = x.shape
    cast = cast_layer is not None
    f = wg.shape[-1]
    assert m % tm == 0 and f % tf == 0 and tm % ROW_CHUNK == 0 and (not cast or m == tm)
    nj = f // tf
    vmem = (2 * tm * d * 4
            + tm * d * 2
            + 2 * tm * d * 4
            + 2 * 3 * d * tf * (6 if cast else 2)
            + tm * d * 4
            + 4 * tm * tf * 4)
    w_specs = [pl.BlockSpec((d, tf), lambda i, j: (0, j)),
               pl.BlockSpec((d, tf), lambda i, j: (0, j)),
               pl.BlockSpec((tf, d), lambda i, j: (j, 0))]
    w_in_specs = w_specs
    if cast:
        w_in_specs = [pl.BlockSpec((1, d, tf), lambda i, j: (cast_layer, 0, j)),
                      pl.BlockSpec((1, d, tf), lambda i, j: (cast_layer, 0, j)),
                      pl.BlockSpec((1, tf, d), lambda i, j: (cast_layer, j, 0))]
    x_out = jax.ShapeDtypeStruct((m, d), F32)
    x_spec = pl.BlockSpec((tm, d), lambda i, j: (i, 0))
    return pl.pallas_call(
        functools.partial(_ffn_cast_kernel if cast else _ffn_kernel, nj=nj, rows=tm, final_norm=final_norm),
        out_shape=(x_out,) + tuple(jax.ShapeDtypeStruct(w.shape[1:], BF16) for w in (wg, wu, wd)) if cast else x_out,
        grid=(m // tm, nj),
        in_specs=[
            pl.BlockSpec((tm, d), lambda i, j: (i, 0)),
            pl.BlockSpec((1, d), lambda i, j: (0, 0)),
            *w_in_specs,
            pl.BlockSpec((1, d), lambda i, j: (0, 0)),
        ],
        out_specs=(x_spec, *w_specs) if cast else x_spec,
        scratch_shapes=[pltpu.VMEM((tm, d), BF16)],
        compiler_params=_cparams(("parallel", "arbitrary"), vmem),
        name="ffn_cast" if cast else "ffn",
    )(x, g, wg, wu, wd, g_final)


def _inproj_kernel(x_ref, g_ref, w_ref, o_ref, xn_ref, *, rows):
    @pl.when(pl.program_id(1) == 0)
    def _():
        def prep(r):
            xn_ref[pl.ds(r, ROW_CHUNK), :] = _rms(x_ref[pl.ds(r, ROW_CHUNK), :], g_ref[...]).astype(BF16)
        _row_loop(rows, prep)

    o_ref[...] = _dot(xn_ref[...], w_ref[...])


def _inproj(x, g, w, *, tm, tn):
    m, d = x.shape
    n = w.shape[1]
    assert m % tm == 0 and n % tn == 0
    vmem = 2 * tm * d * 4 + tm * d * 2 + 2 * d * tn * 2 + 3 * tm * tn * 4
    return pl.pallas_call(
        functools.partial(_inproj_kernel, rows=tm),
        out_shape=jax.ShapeDtypeStruct((m, n), F32),
        grid=(m // tm, n // tn),
        in_specs=[
            pl.BlockSpec((tm, d), lambda i, j: (i, 0)),
            pl.BlockSpec((1, d), lambda i, j: (0, 0)),
            pl.BlockSpec((d, tn), lambda i, j: (0, j)),
        ],
        out_specs=pl.BlockSpec((tm, tn), lambda i, j: (i, j)),
        scratch_shapes=[pltpu.VMEM((tm, d), BF16)],
        compiler_params=_cparams(("parallel", "arbitrary"), vmem),
        name="inproj",
    )(x, g, w)


def _rope_tile(t, c, sa, sb):
    return t * c + pltpu.roll(t, KPE_PAD - MLA_ROPE // 2, 1) * sa + pltpu.roll(t, MLA_ROPE // 2, 1) * sb


def _mla_prep_kernel(cq_ref, ckv_ref, kpe_ref, c_ref, sa_ref, sb_ref, gq_ref, gkv_ref,
                     wuq_ref, wukv_ref,
                     ckv_o, kpe_o, q_o, k_o, v_o, cqn_s, ckv_s, kpe_s, *, rows, hb):
    @pl.when(pl.program_id(1) == 0)
    def _():
        def prep(r):
            sl = pl.ds(r, ROW_CHUNK)
            cqn_s[sl, :] = _rms(cq_ref[sl, :], gq_ref[...]).astype(BF16)
            ckv = _rms(ckv_ref[sl, :], gkv_ref[...])
            ckv_o[sl, :] = ckv
            ckv_s[sl, :] = ckv.astype(BF16)
            kpe = _rope_tile(kpe_ref[sl, :], c_ref[sl, :], sa_ref[sl, :], sb_ref[sl, :])
            kpe_o[sl, :] = kpe
            kpe_s[sl, :] = kpe.astype(BF16)
        _row_loop(rows, prep)

    for hh in range(hb):
        qf = _dot(cqn_s[...], wuq_ref[hh]) * MLA_SCALE
        q_o[hh, :, pl.ds(0, MLA_NOPE)] = qf[:, :MLA_NOPE].astype(BF16)
        q_o[hh, :, pl.ds(MLA_NOPE, KPE_PAD)] = _rope_tile(
            qf[:, MLA_NOPE:], c_ref[...], sa_ref[...], sb_ref[...]).astype(BF16)
        kv = _dot(ckv_s[...], wukv_ref[hh]).astype(BF16)
        k_o[hh, :, pl.ds(0, MLA_NOPE)] = kv[:, :MLA_NOPE]
        k_o[hh, :, pl.ds(MLA_NOPE, KPE_PAD)] = kpe_s[...]
        v_o[hh] = kv[:, MLA_NOPE:]


def _mla_prep(proj, rope_c, rope_sa, rope_sb, gq, gkv, wuq, wukv, *, tm, hb):
    m = proj.shape[0]
    nh = wuq.shape[0]
    assert m % tm == 0 and nh % hb == 0
    vmem = (2 * tm * (Q_LORA + KV_LORA + 4 * KPE_PAD) * 4
            + 2 * hb * (Q_LORA * QK_PAD + 2 * KV_LORA * MLA_NOPE) * 2
            + 2 * tm * (KV_LORA + KPE_PAD) * 4
            + 2 * hb * tm * (2 * QK_PAD + MLA_V) * 2
            + tm * (Q_LORA + KV_LORA + KPE_PAD) * 2
            + 6 * tm * QK_PAD * 4)
    row_blk = lambda w, c: pl.BlockSpec((tm, w), lambda i, h: (i, c))
    head_w = lambda k, n: pl.BlockSpec((hb, k, n), lambda i, h: (h, 0, 0))
    head_o = lambda n: pl.BlockSpec((hb, tm, n), lambda i, h: (h, i, 0))
    return pl.pallas_call(
        functools.partial(_mla_prep_kernel, rows=tm, hb=hb),
        out_shape=(
            jax.ShapeDtypeStruct((m, KV_LORA), F32),
            jax.ShapeDtypeStruct((m, KPE_PAD), F32),
            jax.ShapeDtypeStruct((nh, m, QK_PAD), BF16),
            jax.ShapeDtypeStruct((nh, m, QK_PAD), BF16),
            jax.ShapeDtypeStruct((nh, m, MLA_V), BF16),
        ),
        grid=(m // tm, nh // hb),
        in_specs=[
            row_blk(Q_LORA, PROJ_CQ // Q_LORA),
            row_blk(KV_LORA, PROJ_CKV // KV_LORA),
            row_blk(KPE_PAD, PROJ_KPE // KPE_PAD),
            row_blk(KPE_PAD, 0), row_blk(KPE_PAD, 0), row_blk(KPE_PAD, 0),
            pl.BlockSpec((1, Q_LORA), lambda i, h: (0, 0)),
            pl.BlockSpec((1, KV_LORA), lambda i, h: (0, 0)),
            head_w(Q_LORA, QK_PAD), head_w(KV_LORA, MLA_NOPE + MLA_V),
        ],
        out_specs=(row_blk(KV_LORA, 0), row_blk(KPE_PAD, 0), head_o(QK_PAD), head_o(QK_PAD), head_o(MLA_V)),
        scratch_shapes=[pltpu.VMEM((tm, Q_LORA), BF16), pltpu.VMEM((tm, KV_LORA), BF16),
                        pltpu.VMEM((tm, KPE_PAD), BF16)],
        compiler_params=_cparams(("parallel", "arbitrary"), vmem),
        name="mla_prep",
    )(proj, proj, proj, rope_c, rope_sa, rope_sb, gq, gkv, wuq, wukv)


def _softmax_step(s, v, m, l, acc):
    m_new = jnp.maximum(m, jnp.max(s, axis=-1, keepdims=True))
    p = jnp.exp(s - m_new)
    alpha = jnp.exp(m - m_new)
    l = alpha * l + jnp.sum(p, axis=-1, keepdims=True)
    acc = alpha * acc + _dot(p.astype(BF16), v)
    return m_new, l, acc


def _mla_prompt_kernel(q_ref, k_ref, v_ref, o_ref, *, seq, tile, hp):
    def q_tile(qi, carry):
        q0 = pl.multiple_of(qi * tile, tile)
        qs = [q_ref[h, pl.ds(q0, tile), :] for h in range(hp)]

        def scores(h, k0):
            return _dot_nt(qs[h], k_ref[h, pl.ds(k0, tile), :])

        def k_tile(ki, sts):
            k0 = pl.multiple_of(ki * tile, tile)
            return tuple(_softmax_step(scores(h, k0), v_ref[h, pl.ds(k0, tile), :], *sts[h]) for h in range(hp))

        init = (jnp.full((tile, 1), NEG_BIG, F32), jnp.zeros((tile, 1), F32), jnp.zeros((tile, MLA_V), F32))
        sts = lax.fori_loop(0, qi, k_tile, (init,) * hp)
        row = lax.broadcasted_iota(jnp.int32, (tile, tile), 0)
        col = lax.broadcasted_iota(jnp.int32, (tile, tile), 1)
        visible = col // CHUNK <= row // CHUNK
        for h in range(hp):
            s = jnp.where(visible, scores(h, q0), NEG_BIG)
            _, l, acc = _softmax_step(s, v_ref[h, pl.ds(q0, tile), :], *sts[h])
            o_ref[pl.ds(q0, tile), pl.ds(h * MLA_V, MLA_V)] = acc / l
        return carry

    lax.fori_loop(0, seq // tile, q_tile, 0)


def _mla_prompt(q, k, v, *, batch, seq, tile, hp):
    nh = q.shape[0]
    assert seq % tile == 0 and tile % CHUNK == 0 and nh % hp == 0
    vmem = hp * (2 * seq * (2 * QK_PAD + MLA_V) * 2 + 2 * seq * MLA_V * 4 + 8 * tile * tile * 4)
    return pl.pallas_call(
        functools.partial(_mla_prompt_kernel, seq=seq, tile=tile, hp=hp),
        out_shape=jax.ShapeDtypeStruct((batch * seq, nh * MLA_V), F32),
        grid=(batch, nh // hp),
        in_specs=[
            pl.BlockSpec((hp, seq, QK_PAD), lambda b, h: (h, b, 0)),
            pl.BlockSpec((hp, seq, QK_PAD), lambda b, h: (h, b, 0)),
            pl.BlockSpec((hp, seq, MLA_V), lambda b, h: (h, b, 0)),
        ],
        out_specs=pl.BlockSpec((seq, hp * MLA_V), lambda b, h: (b, h)),
        compiler_params=_cparams(("parallel", "parallel"), vmem),
        name="mla_prompt",
    )(q, k, v)


def _mla_sample_kernel(q_ref, wuk_ref, wuv_ref, ckv_ref, kpe_ref, ckvn_ref, kpen_ref, o_ref,
                       qlat_s, qpe_s, m_s, l_s, acc_s, *, nkt, nh, tq):
    kt = pl.program_id(1)
    rows = nh * tq

    @pl.when(kt == 0)
    def _():
        for h in range(nh):
            qh = q_ref[h]
            qlat_s[pl.ds(h * tq, tq), :] = _dot_nt(qh[:, :MLA_NOPE], wuk_ref[h]).astype(BF16)
            qpe_s[pl.ds(h * tq, tq), :] = qh[:, MLA_NOPE:MLA_NOPE + KPE_PAD]
        m_s[...] = jnp.full((rows, 1), NEG_BIG, F32)
        l_s[...] = jnp.zeros((rows, 1), F32)
        acc_s[...] = jnp.zeros((rows, KV_LORA), F32)

    def update(ckv, kpe):
        s = _dot_nt(qlat_s[...], ckv) + _dot_nt(qpe_s[:, pl.ds(0, MLA_ROPE)], kpe)
        m, l, acc = _softmax_step(s, ckv, m_s[...], l_s[...], acc_s[...])
        m_s[...] = m
        l_s[...] = l
        acc_s[...] = acc

    update(ckv_ref[0, 0].astype(BF16), kpe_ref[0, 0].astype(BF16))

    @pl.when(kt == nkt - 1)
    def _():
        update(ckvn_ref[...].astype(BF16), kpen_ref[:, pl.ds(0, MLA_ROPE)].astype(BF16))
        o_lat = (acc_s[...] / l_s[...]).astype(BF16)
        for h in range(nh):
            o_ref[:, pl.ds(h * MLA_V, MLA_V)] = _dot(o_lat[h * tq:(h + 1) * tq], wuv_ref[h])


def _mla_sample(q, wuk, wuv, cache_ckv, cache_kpe, ckv_new, kpe_new, *, layer, row0, tk):
    nh = q.shape[0]
    _, batch, past, _ = cache_ckv.shape
    tq = S_SEQ
    assert past % tk == 0 and row0 % tq == 0
    assert (PAST_LEN + S_SEQ - 1) // CHUNK == PAST_LEN // CHUNK
    nkt = past // tk
    rows = nh * tq
    vmem = (2 * nh * tq * QK_PAD * 2 + 4 * nh * KV_LORA * MLA_NOPE * 2
            + 2 * tk * (KV_LORA + KPE_PAD) * 4 + tk * (KV_LORA + KPE_PAD) * 2
            + 2 * tq * (KV_LORA + KPE_PAD) * 4 + 2 * tq * nh * MLA_V * 4
            + rows * (KV_LORA + KPE_PAD) * 2 + rows * (KV_LORA + 2 * V7X_LANES) * 4
            + 4 * rows * tk * 4)
    rb = row0 // tq
    return pl.pallas_call(
        functools.partial(_mla_sample_kernel, nkt=nkt, nh=nh, tq=tq),
        out_shape=jax.ShapeDtypeStruct((batch * tq, nh * MLA_V), F32),
        grid=(batch, nkt),
        in_specs=[
            pl.BlockSpec((nh, tq, QK_PAD), lambda b, t: (0, rb + b, 0)),
            pl.BlockSpec((nh, KV_LORA, MLA_NOPE), lambda b, t: (0, 0, 0)),
            pl.BlockSpec((nh, KV_LORA, MLA_V), lambda b, t: (0, 0, 0)),
            pl.BlockSpec((1, 1, tk, KV_LORA), lambda b, t: (layer, b, t, 0)),
            pl.BlockSpec((1, 1, tk, MLA_ROPE), lambda b, t: (layer, b, t, 0)),
            pl.BlockSpec((tq, KV_LORA), lambda b, t: (rb + b, 0)),
            pl.BlockSpec((tq, KPE_PAD), lambda b, t: (rb + b, 0)),
        ],
        out_specs=pl.BlockSpec((tq, nh * MLA_V), lambda b, t: (b, 0)),
        scratch_shapes=[pltpu.VMEM((rows, KV_LORA), BF16), pltpu.VMEM((rows, KPE_PAD), BF16),
                        pltpu.VMEM((rows, 1), F32), pltpu.VMEM((rows, 1), F32),
                        pltpu.VMEM((rows, KV_LORA), F32)],
        compiler_params=_cparams(("parallel", "arbitrary"), vmem),
        name="mla_sample",
    )(q, wuk, wuv, cache_ckv, cache_kpe, ckv_new, kpe_new)


def _softplus(z):
    return jnp.maximum(z, 0.0) + jnp.log(1.0 + jnp.exp(-jnp.abs(z)))


def _later_keys_matrix(n):
    row = lax.broadcasted_iota(jnp.int32, (n, n), 0)
    col = lax.broadcasted_iota(jnp.int32, (n, n), 1)
    return (row > col).astype(BF16)


def _later_sum(x, u):
    r = x.shape[0]
    hi = x.astype(BF16)
    lo = (x - hi.astype(F32)).astype(BF16)
    e = _dot(jnp.concatenate([hi, lo], axis=0), u)
    return e[:r] + e[r:]


def _sb_weights(z, u, decay, mask=None):
    sp = _softplus(z)
    d = sp if mask is None else jnp.where(mask, sp, 0.0)
    a = jnp.exp(z - sp - _later_sum(d, u) - decay)
    if mask is not None:
        a = jnp.where(mask, a, 0.0)
    return a.astype(BF16), decay + jnp.sum(d, axis=-1, keepdims=True)


def _sb_prompt_kernel(q_ref, k_ref, v_ref, o_ref, qb, kb, vb, *, seq, tq, tk, hp):
    heads = range(hp)
    hcol = lambda h: pl.ds(h * SB_DIM, SB_DIM)

    def cast(r):
        sl = pl.ds(r, ROW_CHUNK)
        qb[sl, :] = (q_ref[sl, :] * SB_SCALE).astype(BF16)
        kb[sl, :] = k_ref[sl, :].astype(BF16)
        vb[sl, :] = v_ref[sl, :].astype(BF16)
    _row_loop(seq, cast)

    u = _later_keys_matrix(tk)
    per_q = tq // tk

    def q_tile(qi, carry):
        q0 = pl.multiple_of(qi * tq, tq)
        qs = [qb[pl.ds(q0, tq), hcol(h)] for h in heads]

        def key_tile(k0, sts, mask=None):
            out = []
            for h in heads:
                decay, acc = sts[h]
                a, decay = _sb_weights(_dot_nt(qs[h], kb[pl.ds(k0, tk), hcol(h)]), u, decay, mask)
                out.append((decay, acc + _dot(a, vb[pl.ds(k0, tk), hcol(h)])))
            return tuple(out)

        sts = ((jnp.zeros((tq, 1), F32), jnp.zeros((tq, SB_DIM), F32)),) * hp
        for d in reversed(range(per_q)):
            row = lax.broadcasted_iota(jnp.int32, (tq, tk), 0)
            col = lax.broadcasted_iota(jnp.int32, (tq, tk), 1) + d * tk
            sts = key_tile(pl.multiple_of(q0 + d * tk, tk), sts, mask=col < row)

        def older(step, sts):
            for d in reversed(range(per_q)):
                sts = key_tile(pl.multiple_of(q0 - (step + 1) * tq + d * tk, tk), sts)
            return sts

        sts = lax.fori_loop(0, qi, older, sts)
        for h in heads:
            o_ref[pl.ds(q0, tq), hcol(h)] = sts[h][1]
        return carry

    lax.fori_loop(0, seq // tq, q_tile, 0)


def _sb_prompt(proj, *, batch, seq, tq, tk, hp):
    assert seq % tq == 0 and tq % tk == 0 and seq % ROW_CHUNK == 0 and SB_HEADS % hp == 0
    w = hp * SB_DIM
    vmem = hp * (2 * 3 * seq * SB_DIM * 4 + 2 * seq * SB_DIM * 4 + 3 * seq * SB_DIM * 2 + 12 * tq * tk * 4)
    col = lambda c0: pl.BlockSpec((seq, w), lambda b, h: (b, c0 // w + h))
    return pl.pallas_call(
        functools.partial(_sb_prompt_kernel, seq=seq, tq=tq, tk=tk, hp=hp),
        out_shape=jax.ShapeDtypeStruct((batch * seq, SB_WIDTH), F32),
        grid=(batch, SB_HEADS // hp),
        in_specs=[col(PROJ_SQ), col(PROJ_SK), col(PROJ_SV)],
        out_specs=pl.BlockSpec((seq, w), lambda b, h: (b, h)),
        scratch_shapes=[pltpu.VMEM((seq, w), BF16)] * 3,
        compiler_params=_cparams(("parallel", "parallel"), vmem),
        name="sb_prompt",
    )(proj, proj, proj)


def _sb_sample_kernel(q_ref, kn_ref, vn_ref, kc_ref, vc_ref, o_ref, decay_s, acc_s, *, nkt, sub, tq):
    step = pl.program_id(1)
    rows = SB_HEADS * tq
    head = lambda h: pl.ds(h * SB_DIM, SB_DIM)
    hrow = lambda h: pl.ds(h * tq, tq)

    def scores(k_of_head):
        return jnp.concatenate(
            [_dot_nt((q_ref[:, head(h)] * SB_SCALE).astype(BF16), k_of_head(h)) for h in range(SB_HEADS)], axis=0)

    def weighted(a, v_of_head):
        return jnp.concatenate(
            [_dot(a[h * tq:(h + 1) * tq], v_of_head(h)) for h in range(SB_HEADS)], axis=0)

    @pl.when(step == 0)
    def _():
        pad = jnp.zeros((V7X_LANES - tq, SB_DIM), BF16)
        z = scores(lambda h: jnp.concatenate([kn_ref[:, head(h)].astype(BF16), pad], axis=0))
        t = lax.broadcasted_iota(jnp.int32, (rows, V7X_LANES), 0) % tq
        s = lax.broadcasted_iota(jnp.int32, (rows, V7X_LANES), 1)
        a, decay = _sb_weights(z, _later_keys_matrix(V7X_LANES), jnp.zeros((rows, 1), F32), mask=s < t)
        acc_s[...] = weighted(a, lambda h: jnp.concatenate([vn_ref[:, head(h)].astype(BF16), pad], axis=0))
        decay_s[...] = decay

    @pl.when(step > 0)
    def _():
        u = _later_keys_matrix(sub)
        decay = decay_s[...]
        acc = acc_s[...]
        for c in reversed(range(kc_ref.shape[2] // (sub * SB_HEADS))):
            keys = lambda h: pl.ds(c * sub * SB_HEADS + h, sub, stride=SB_HEADS)
            a, decay = _sb_weights(scores(lambda h: kc_ref[0, 0, keys(h), :].astype(BF16)), u, decay)
            acc = acc + weighted(a, lambda h: vc_ref[0, 0, keys(h), :].astype(BF16))
        decay_s[...] = decay
        acc_s[...] = acc

    @pl.when(step == nkt)
    def _():
        for h in range(SB_HEADS):
            o_ref[:, head(h)] = acc_s[hrow(h), :]


def _sb_sample(proj, cache_k, cache_v, *, layer, tk, sub):
    _, batch, past_rows, _ = cache_k.shape
    past = past_rows // SB_HEADS
    tq = S_SEQ
    assert past % tk == 0 and tk % sub == 0
    nkt = past // tk
    rows = SB_HEADS * tq
    vmem = (2 * 3 * tq * SB_WIDTH * 4 + 2 * 2 * tk * SB_WIDTH * 4 + 2 * tq * SB_WIDTH * 4
            + rows * (SB_DIM + V7X_LANES) * 4 + 2 * sub * SB_WIDTH * 2 + 16 * rows * sub * 4 + sub * sub * 2)
    newest_first = lambda b, s: (layer, b, nkt - jnp.maximum(s, 1), 0)
    col = lambda c0: pl.BlockSpec((tq, SB_WIDTH), lambda b, s: (b, c0 // SB_WIDTH))
    return pl.pallas_call(
        functools.partial(_sb_sample_kernel, nkt=nkt, sub=sub, tq=tq),
        out_shape=jax.ShapeDtypeStruct((batch * tq, SB_WIDTH), F32),
        grid=(batch, nkt + 1),
        in_specs=[col(PROJ_SQ), col(PROJ_SK), col(PROJ_SV),
                  pl.BlockSpec((1, 1, tk * SB_HEADS, SB_DIM), newest_first),
                  pl.BlockSpec((1, 1, tk * SB_HEADS, SB_DIM), newest_first)],
        out_specs=pl.BlockSpec((tq, SB_WIDTH), lambda b, s: (b, 0)),
        scratch_shapes=[pltpu.VMEM((rows, 1), F32), pltpu.VMEM((rows, SB_DIM), F32)],
        compiler_params=_cparams(("parallel", "arbitrary"), vmem),
        name="sb_sample",
    )(proj, proj, proj, cache_k, cache_v)


def _sb_rows_kernel(*refs, tt, layers):
    srcs, (ko_ref, vo_ref) = refs[:2 * layers], refs[2 * layers:]
    for l in range(layers):
        @pl.when(pl.program_id(0) == l)
        def _():
            for src, dst in ((srcs[2 * l], ko_ref), (srcs[2 * l + 1], vo_ref)):
                for h in range(SB_HEADS):
                    dst[0, pl.ds(h, tt, stride=SB_HEADS), :] = src[:, pl.ds(h * SB_DIM, SB_DIM)]


def _sb_rows(projs, *, tt):
    layers = len(projs)
    m = projs[0].shape[0]
    assert m % tt == 0
    nt = m // tt
    vmem = (2 * layers + 2) * 2 * tt * SB_WIDTH * 4

    def col(l, c0):
        return pl.BlockSpec((tt, SB_WIDTH), lambda g, i: (jnp.clip(i + (g - l) * nt, 0, nt - 1), c0 // SB_WIDTH))

    out = jax.ShapeDtypeStruct((layers, m * SB_HEADS, SB_DIM), F32)
    return pl.pallas_call(
        functools.partial(_sb_rows_kernel, tt=tt, layers=layers),
        out_shape=(out, out),
        grid=(layers, nt),
        in_specs=[col(l, c0) for l in range(layers) for c0 in (PROJ_SK, PROJ_SV)],
        out_specs=(pl.BlockSpec((1, tt * SB_HEADS, SB_DIM), lambda g, i: (g, i, 0)),) * 2,
        compiler_params=_cparams(("arbitrary", "arbitrary"), vmem),
        name="sb_rows",
    )(*[p for p in projs for _ in range(2)])


def _pool_kernel(u_ref, hist_ref, w_ref, scale_ref, o_ref, carry_s, *, tt, pos0):
    ti = pl.program_id(1)
    halo = carry_s.shape[0]

    @pl.when(ti == 0)
    def _():
        carry_s[...] = hist_ref[0]

    pos = pos0 + ti * tt + lax.broadcasted_iota(jnp.int32, (tt, 1), 0)
    for g, w in enumerate(POOL_WINDOWS):
        cols = pl.ds(g * POOL_GROUP, POOL_GROUP)
        tok = u_ref[:, cols]
        s = jnp.concatenate([carry_s[:, cols], tok], axis=0)
        shift = 1
        while shift < w:
            s = s + pltpu.roll(s, shift, 0)
            shift *= 2
        cnt = jnp.minimum(pos + 1, w).astype(F32)
        pooled = s[halo:] / cnt - tok
        o_ref[:, cols] = (_dot(pooled.astype(BF16), w_ref[g]) * scale_ref[:, cols]).astype(BF16)
    carry_s[...] = u_ref[pl.ds(tt - halo, halo), :]


def _pool(proj, hist, w_pool, scale, *, batch, seq, tt, pos0):
    halo = hist.shape[1]
    assert seq % tt == 0 and tt >= halo and halo > max(POOL_WINDOWS) - 1
    nt = seq // tt
    vmem = (2 * tt * POOL_WIDTH * 4 + 2 * halo * POOL_WIDTH * 4 + 2 * POOL_WIDTH * POOL_GROUP * 2
            + 2 * tt * POOL_WIDTH * 2 + halo * POOL_WIDTH * 4 + 8 * (tt + halo) * POOL_GROUP * 4)
    return pl.pallas_call(
        functools.partial(_pool_kernel, tt=tt, pos0=pos0),
        out_shape=jax.ShapeDtypeStruct((batch * seq, POOL_WIDTH), BF16),
        grid=(batch, nt),
        in_specs=[
            pl.BlockSpec((tt, POOL_WIDTH), lambda b, t: (b * nt + t, PROJ_U // POOL_WIDTH)),
            pl.BlockSpec((1, halo, POOL_WIDTH), lambda b, t: (b, 0, 0)),
            pl.BlockSpec((len(POOL_WINDOWS), POOL_GROUP, POOL_GROUP), lambda b, t: (0, 0, 0)),
            pl.BlockSpec((1, POOL_WIDTH), lambda b, t: (0, 0)),
        ],
        out_specs=pl.BlockSpec((tt, POOL_WIDTH), lambda b, t: (b * nt + t, 0)),
        scratch_shapes=[pltpu.VMEM((halo, POOL_WIDTH), F32)],
        compiler_params=_cparams(("parallel", "arbitrary"), vmem),
        name="pool",
    )(proj, hist, w_pool, scale)


def _outproj_kernel(mla_ref, pool_ref, sb_ref, x_ref, gm_ref, gs_ref, w_ref, o_ref, mix_s, *, rows):
    wm = mla_ref.shape[1]
    wp = pool_ref.shape[1]

    @pl.when(pl.program_id(1) == 0)
    def _():
        def prep(r):
            sl = pl.ds(r, ROW_CHUNK)
            mix_s[sl, pl.ds(0, wm)] = _rms(mla_ref[sl, :], gm_ref[...]).astype(BF16)
            mix_s[sl, pl.ds(wm, wp)] = pool_ref[sl, :]
            mix_s[sl, pl.ds(wm + wp, sb_ref.shape[1])] = _rms(sb_ref[sl, :], gs_ref[...]).astype(BF16)
        _row_loop(rows, prep)

    o_ref[...] = x_ref[...] + _dot(mix_s[...], w_ref[...])


def _outproj(o_mla, o_pool, o_sb, x, g_mla, g_sb, w, *, tm, tn):
    m, d = x.shape
    wm, wp, ws = o_mla.shape[1], o_pool.shape[1], o_sb.shape[1]
    k = wm + wp + ws
    assert m % tm == 0 and d % tn == 0 and w.shape == (k, d)
    vmem = (2 * tm * (wm + ws) * 4 + 2 * tm * wp * 2 + 2 * 2 * tm * tn * 4 + 2 * k * tn * 2
            + tm * k * 2 + 2 * tm * tn * 4)
    full = lambda n: pl.BlockSpec((tm, n), lambda i, j: (i, 0))
    return pl.pallas_call(
        functools.partial(_outproj_kernel, rows=tm),
        out_shape=jax.ShapeDtypeStruct((m, d), F32),
        grid=(m // tm, d // tn),
        in_specs=[full(wm), full(wp), full(ws),
                  pl.BlockSpec((tm, tn), lambda i, j: (i, j)),
                  pl.BlockSpec((1, wm), lambda i, j: (0, 0)),
                  pl.BlockSpec((1, ws), lambda i, j: (0, 0)),
                  pl.BlockSpec((k, tn), lambda i, j: (0, j))],
        out_specs=pl.BlockSpec((tm, tn), lambda i, j: (i, j)),
        scratch_shapes=[pltpu.VMEM((tm, k), BF16)],
        compiler_params=_cparams(("parallel", "arbitrary"), vmem),
        name="outproj",
    )(o_mla, o_pool, o_sb, x, g_mla, g_sb, w)


def _rope_tables(pos):
    half = MLA_ROPE // 2
    inv = 1.0 / (ROPE_THETA ** (jnp.arange(half, dtype=F32) / half))
    ang = pos.astype(F32)[:, None] * inv[None, :]
    cos, sin = jnp.cos(ang), jnp.sin(ang)
    z = lambda n: jnp.zeros((pos.shape[0], n), F32)
    return (jnp.concatenate([cos, cos, z(KPE_PAD - MLA_ROPE)], axis=1),
            jnp.concatenate([-sin, z(KPE_PAD - half)], axis=1),
            jnp.concatenate([z(half), sin, z(KPE_PAD - MLA_ROPE)], axis=1))


W_IN_PIECES = (
    (0, Q_LORA, PROJ_CQ),
    (Q_LORA, KV_LORA, PROJ_CKV),
    (Q_LORA + KV_LORA, MLA_ROPE, PROJ_KPE),
    (Q_LORA + KV_LORA + MLA_ROPE, POOL_WIDTH, PROJ_U),
    (Q_LORA + KV_LORA + MLA_ROPE + POOL_WIDTH, SB_WIDTH, PROJ_SQ),
    (Q_LORA + KV_LORA + MLA_ROPE + POOL_WIDTH + SB_WIDTH, SB_WIDTH, PROJ_SK),
    (Q_LORA + KV_LORA + MLA_ROPE + POOL_WIDTH + 2 * SB_WIDTH, SB_WIDTH, PROJ_SV),
)


def _w_in_kernel(w_ref, o_ref):
    rows = o_ref.shape[0]
    o_ref[:, pl.ds(PROJ_KPE, KPE_PAD)] = jnp.zeros((rows, KPE_PAD), BF16)
    for src, width, dst in W_IN_PIECES:
        o_ref[:, pl.ds(dst, width)] = w_ref[0, :, pl.ds(src, width)].astype(BF16)


def _w_in_bf16(w_in, l, *, tr):
    _, d, n = w_in.shape
    assert d % tr == 0
    vmem = 2 * tr * n * 4 + 2 * tr * PROJ_W * 2 + tr * PROJ_W * 4
    return pl.pallas_call(
        _w_in_kernel,
        out_shape=jax.ShapeDtypeStruct((d, PROJ_W), BF16),
        grid=(d // tr,),
        in_specs=[pl.BlockSpec((1, tr, n), lambda i: (l, i, 0))],
        out_specs=pl.BlockSpec((tr, PROJ_W), lambda i: (i, 0)),
        compiler_params=_cparams(("parallel",), vmem),
        name="w_in_bf16",
    )(w_in)


def _cast_kernel(w_ref, o_ref):
    o_ref[...] = w_ref[0].astype(BF16)


def _layer_bf16(w, l, *, tr):
    _, r, c = w.shape
    assert r % tr == 0
    return pl.pallas_call(
        _cast_kernel,
        out_shape=jax.ShapeDtypeStruct((r, c), BF16),
        grid=(r // tr,),
        in_specs=[pl.BlockSpec((1, tr, c), lambda i: (l, i, 0))],
        out_specs=pl.BlockSpec((tr, c), lambda i: (i, 0)),
        compiler_params=_cparams(("parallel",), 2 * tr * c * 6),
        name="layer_bf16",
    )(w)


def _layer_weights(l, w_in, w_uq, w_uk, w_uv, w_pool, w_out):
    wuq = w_uq[l].reshape(Q_LORA, MLA_HEADS, MLA_NOPE + MLA_ROPE)
    wuq = jnp.pad(wuq, ((0, 0), (0, 0), (0, QK_PAD - MLA_NOPE - MLA_ROPE)))
    wuk = w_uk[l].transpose(1, 0, 2).astype(BF16)
    wuv = w_uv[l].transpose(1, 0, 2).astype(BF16)
    return dict(
        w_in=_w_in_bf16(w_in, l, tr=256),
        wuq=wuq.transpose(1, 0, 2).astype(BF16),
        wuk=wuk,
        wuv=wuv,
        wukv=jnp.concatenate([wuk, wuv], axis=-1),
        w_pool=w_pool[l].astype(BF16),
        w_out=_layer_bf16(w_out, l, tr=512),
    )


def kernel(x_prompt, x_sample, cache_ckv, cache_kpe, state_pool, cache_sb_k, cache_sb_v, g_ffn1, w1_gate, w1_up, w1_down, g_mix, w_in, g_qnorm, w_uq, g_kvnorm, w_uk, w_uv, w_pool, pool_scale, g_mla_out, g_sb_out, w_out, g_ffn2, w2_gate, w2_up, w2_down, g_final):
    mp, ms = P_BATCH * P_SEQ, S_BATCH * S_SEQ
    tm_p, tf = 512, 256
    row = lambda g: g.reshape(1, -1)
    gfin = row(g_final)

    rope_p = _rope_tables(jnp.tile(jnp.arange(P_SEQ), P_BATCH))
    rope_s = _rope_tables(PAST_LEN + jnp.tile(jnp.arange(S_SEQ), S_BATCH))
    hist_p = jnp.zeros((P_BATCH, POOL_HIST + 1, POOL_WIDTH), F32)
    cache_k = cache_sb_k.reshape(DEPTH, S_BATCH, PAST_LEN * SB_HEADS, SB_DIM)
    cache_v = cache_sb_v.reshape(DEPTH, S_BATCH, PAST_LEN * SB_HEADS, SB_DIM)

    hp = x_prompt.reshape(mp, D_MODEL)
    hs = x_sample.reshape(ms, D_MODEL)
    outs = {k: [] for k in ("p_proj", "p_ckv", "p_kpe", "p_pool", "s_proj", "s_ckv", "s_kpe", "s_pool")}

    def record(tag, batch, seq, proj, ckv, kpe):
        outs[tag + "_proj"].append(proj)
        outs[tag + "_ckv"].append(ckv.reshape(batch, seq, KV_LORA))
        outs[tag + "_kpe"].append(kpe[:, :MLA_ROPE].reshape(batch, seq, MLA_ROPE))
        outs[tag + "_pool"].append(
            proj.reshape(batch, seq, PROJ_W)[:, seq - POOL_HIST:, PROJ_U:PROJ_U + POOL_WIDTH])

    for l in range(DEPTH):
        w = _layer_weights(l, w_in, w_uq, w_uk, w_uv, w_pool, w_out)
        last = l == DEPTH - 1
        gq, gkv = row(g_qnorm[l]), row(g_kvnorm[l])
        pscale = row(pool_scale[l])
        hist_s = jnp.pad(state_pool[l], ((0, 0), (1, 0), (0, 0)))

        def mixer_inputs(h, tm, rope):
            proj = _inproj(h, row(g_mix[l]), w["w_in"], tm=tm, tn=PROJ_W // 5)
            return proj, _mla_prep(proj, *rope, gq, gkv, w["wuq"], w["wukv"], tm=tm, hb=8)

        def mix(h, tm, o_mla, o_pool, o_sb):
            return _outproj(o_mla, o_pool, o_sb, h, row(g_mla_out[l]), row(g_sb_out[l]), w["w_out"], tm=tm, tn=1024)

        hs, *w1 = _ffn(hs, row(g_ffn1[l]), w1_gate, w1_up, w1_down, gfin, tm=ms, tf=tf // 2,
                       final_norm=False, cast_layer=l)
        proj, (ckv, kpe, q, _, _) = mixer_inputs(hs, ms, rope_s)
        o_mla = _mla_sample(q, w["wuk"], w["wuv"], cache_ckv, cache_kpe, ckv, kpe, layer=l, row0=0, tk=2048)
        o_sb = _sb_sample(proj, cache_k, cache_v, layer=l, tk=1024, sub=256)
        o_pool = _pool(proj, hist_s, w["w_pool"], pscale, batch=S_BATCH, seq=S_SEQ, tt=S_SEQ, pos0=PAST_LEN)
        hs = mix(hs, ms, o_mla, o_pool, o_sb)
        hs, *w2 = _ffn(hs, row(g_ffn2[l]), w2_gate, w2_up, w2_down, gfin, tm=ms, tf=tf // 2,
                       final_norm=last, cast_layer=l)
        record("s", S_BATCH, S_SEQ, proj, ckv, kpe)

        hp = _ffn(hp, row(g_ffn1[l]), *w1, gfin, tm=tm_p, tf=tf, final_norm=False)
        proj, (ckv, kpe, q, k, v) = mixer_inputs(hp, tm_p, rope_p)
        o_mla = _mla_prompt(q, k, v, batch=P_BATCH, seq=P_SEQ, tile=512, hp=2)
        o_sb = _sb_prompt(proj, batch=P_BATCH, seq=P_SEQ, tq=512, tk=256, hp=1)
        o_pool = _pool(proj, hist_p, w["w_pool"], pscale, batch=P_BATCH, seq=P_SEQ, tt=512, pos0=0)
        hp = mix(hp, tm_p, o_mla, o_pool, o_sb)
        hp = _ffn(hp, row(g_ffn2[l]), *w2, gfin, tm=tm_p, tf=tf, final_norm=last)
        record("p", P_BATCH, P_SEQ, proj, ckv, kpe)

    st = lambda k: jnp.stack(outs[k])
    heads = lambda a, batch, seq: a.reshape(DEPTH, batch, seq, SB_HEADS, SB_DIM)
    p_sbk, p_sbv = _sb_rows(outs["p_proj"], tt=tm_p)
    s_sbk, s_sbv = _sb_rows(outs["s_proj"], tt=ms)
    return (hp.reshape(P_BATCH, P_SEQ, D_MODEL), hs.reshape(S_BATCH, S_SEQ, D_MODEL),
            st("p_ckv"), st("p_kpe"), st("p_pool"), heads(p_sbk, P_BATCH, P_SEQ), heads(p_sbv, P_BATCH, P_SEQ),
            st("s_ckv"), st("s_kpe"), st("s_pool"), heads(s_sbk, S_BATCH, S_SEQ), heads(s_sbv, S_BATCH, S_SEQ))
```

```python
import functools
import math

import jax
import jax.numpy as jnp
from jax import lax
from jax.experimental import pallas as pl
from jax.experimental.pallas import tpu as pltpu

F32 = jnp.float32
BF16 = jnp.bfloat16

D_MODEL = 4096
D_FF = 11008
DEPTH = 2
P_BATCH, P_SEQ = 4, 2048
S_BATCH, S_SEQ, PAST_LEN = 16, 16, 4096
CHUNK = 64
MLA_HEADS, MLA_NOPE, MLA_ROPE, MLA_V = 16, 128, 64, 128
Q_LORA, KV_LORA = 1024, 512
ROPE_THETA = 10000.0
MLA_SCALE = 1.0 / math.sqrt(MLA_NOPE + MLA_ROPE)
POOL_WINDOWS = (2, 4, 8, 16)
POOL_GROUP = 256
POOL_WIDTH = 1024
POOL_HIST = 15
SB_HEADS, SB_DIM = 8, 128
SB_WIDTH = 1024
SB_SCALE = 1.0 / math.sqrt(SB_DIM)
EPS = 1e-6

V7X_LANES = 128
V7X_VMEM_LIMIT_BYTES = 60000 * 1024
VALUE_SPILL_BYTES = 8 * 1024 * 1024

PROJ_CQ = 0
PROJ_U = 1024
PROJ_SQ = 2048
PROJ_SK = 3072
PROJ_SV = 4096
PROJ_CKV = 5120
PROJ_KPE = 5632
PROJ_W = 5760
KPE_PAD = 128
QK_PAD = 256

ROW_CHUNK = 64
NEG_BIG = -1e30


def _cparams(semantics, vmem_bytes):
    assert vmem_bytes <= V7X_VMEM_LIMIT_BYTES, vmem_bytes
    limit = min(V7X_VMEM_LIMIT_BYTES, vmem_bytes + VALUE_SPILL_BYTES)
    return pltpu.CompilerParams(dimension_semantics=semantics, vmem_limit_bytes=int(limit))


def _rms(x, g):
    ms = jnp.mean(x * x, axis=-1, keepdims=True)
    return (x * lax.rsqrt(ms + EPS)) * g


def _dot(a, b):
    return jnp.dot(a, b, preferred_element_type=F32)


def _dot_nt(a, b):
    return lax.dot_general(a, b, (((1,), (1,)), ((), ())), preferred_element_type=F32)


def _row_loop(rows, fn):
    def body(c, carry):
        fn(pl.multiple_of(c * ROW_CHUNK, ROW_CHUNK))
        return carry
    lax.fori_loop(0, rows // ROW_CHUNK, body, 0)


def _ffn_step(x_ref, g_ref, wg_ref, wu_ref, wd_ref, gf_ref, o_ref, xn_ref, *, nj, rows, final_norm):
    j = pl.program_id(1)

    @pl.when(j == 0)
    def _():
        def prep(r):
            x = x_ref[pl.ds(r, ROW_CHUNK), :]
            xn_ref[pl.ds(r, ROW_CHUNK), :] = _rms(x, g_ref[...]).astype(BF16)
            o_ref[pl.ds(r, ROW_CHUNK), :] = jnp.zeros((ROW_CHUNK, x.shape[1]), F32)
        _row_loop(rows, prep)

    xn = xn_ref[...]
    gate = _dot(xn, wg_ref[...])
    up = _dot(xn, wu_ref[...])
    h = (gate * (1.0 / (1.0 + jnp.exp(-gate))) * up).astype(BF16)
    o_ref[...] += _dot(h, wd_ref[...])

    @pl.when(j == nj - 1)
    def _():
        def fin(r):
            y = x_ref[pl.ds(r, ROW_CHUNK), :] + 0.5 * o_ref[pl.ds(r, ROW_CHUNK), :]
            if final_norm:
                y = _rms(y, gf_ref[...])
            o_ref[pl.ds(r, ROW_CHUNK), :] = y
        _row_loop(rows, fin)


def _ffn_kernel(x_ref, g_ref, wg_ref, wu_ref, wd_ref, gf_ref, o_ref, xn_ref, **kw):
    _ffn_step(x_ref, g_ref, wg_ref, wu_ref, wd_ref, gf_ref, o_ref, xn_ref, **kw)


def _ffn_cast_kernel(x_ref, g_ref, wg_ref, wu_ref, wd_ref, gf_ref, o_ref, wg_o, wu_o, wd_o, xn_ref, **kw):
    for src, dst in ((wg_ref, wg_o), (wu_ref, wu_o), (wd_ref, wd_o)):
        dst[...] = src[0].astype(BF16)
    _ffn_step(x_ref, g_ref, wg_o, wu_o, wd_o, gf_ref, o_ref, xn_ref, **kw)


def _ffn(x, g, wg, wu, wd, g_final, *, tm, tf, final_norm, cast_layer=None):
    m, d = x.shape
    cast = cast_layer is not None
    f = wg.shape[-1]
    assert m % tm == 0 and f % tf == 0 and tm % ROW_CHUNK == 0 and (not cast or m == tm)
    nj = f // tf
    vmem = (2 * tm * d * 4
            + tm * d * 2
            + 2 * tm * d * 4
            + 2 * 3 * d * tf * (6 if cast else 2)
            + tm * d * 4
            + 4 * tm * tf * 4)
    w_specs = [pl.BlockSpec((d, tf), lambda i, j: (0, j)),
               pl.BlockSpec((d, tf), lambda i, j: (0, j)),
               pl.BlockSpec((tf, d), lambda i, j: (j, 0))]
    w_in_specs = w_specs
    if cast:
        w_in_specs = [pl.BlockSpec((1, d, tf), lambda i, j: (cast_layer, 0, j)),
                      pl.BlockSpec((1, d, tf), lambda i, j: (cast_layer, 0, j)),
                      pl.BlockSpec((1, tf, d), lambda i, j: (cast_layer, j, 0))]
    x_out = jax.ShapeDtypeStruct((m, d), F32)
    x_spec = pl.BlockSpec((tm, d), lambda i, j: (i, 0))
    return pl.pallas_call(
        functools.partial(_ffn_cast_kernel if cast else _ffn_kernel, nj=nj, rows=tm, final_norm=final_norm),
        out_shape=(x_out,) + tuple(jax.ShapeDtypeStruct(w.shape[1:], BF16) for w in (wg, wu, wd)) if cast else x_out,
        grid=(m // tm, nj),
        in_specs=[
            pl.BlockSpec((tm, d), lambda i, j: (i, 0)),
            pl.BlockSpec((1, d), lambda i, j: (0, 0)),
            *w_in_specs,
            pl.BlockSpec((1, d), lambda i, j: (0, 0)),
        ],
        out_specs=(x_spec, *w_specs) if cast else x_spec,
        scratch_shapes=[pltpu.VMEM((tm, d), BF16)],
        compiler_params=_cparams(("parallel", "arbitrary"), vmem),
        name="ffn_cast" if cast else "ffn",
    )(x, g, wg, wu, wd, g_final)


def _inproj_kernel(x_ref, g_ref, w_ref, o_ref, xn_ref, *, rows):
    @pl.when(pl.program_id(1) == 0)
    def _():
        def prep(r):
            xn_ref[pl.ds(r, ROW_CHUNK), :] = _rms(x_ref[pl.ds(r, ROW_CHUNK), :], g_ref[...]).astype(BF16)
        _row_loop(rows, prep)

    o_ref[...] = _dot_nt(xn_ref[...], w_ref[...])


def _inproj(x, g, w_t, *, tm, tn):
    m, d = x.shape
    n = w_t.shape[0]
    assert m % tm == 0 and n % tn == 0
    vmem = 2 * tm * d * 4 + tm * d * 2 + 2 * d * tn * 2 + 3 * tm * tn * 4
    return pl.pallas_call(
        functools.partial(_inproj_kernel, rows=tm),
        out_shape=jax.ShapeDtypeStruct((m, n), F32),
        grid=(m // tm, n // tn),
        in_specs=[
            pl.BlockSpec((tm, d), lambda i, j: (i, 0)),
            pl.BlockSpec((1, d), lambda i, j: (0, 0)),
            pl.BlockSpec((tn, d), lambda i, j: (j, 0)),
        ],
        out_specs=pl.BlockSpec((tm, tn), lambda i, j: (i, j)),
        scratch_shapes=[pltpu.VMEM((tm, d), BF16)],
        compiler_params=_cparams(("parallel", "arbitrary"), vmem),
        name="inproj",
    )(x, g, w_t)


def _rope_tile(t, c, sa, sb):
    return t * c + pltpu.roll(t, KPE_PAD - MLA_ROPE // 2, 1) * sa + pltpu.roll(t, MLA_ROPE // 2, 1) * sb


def _mla_prep_kernel(cq_ref, ckv_ref, kpe_ref, c_ref, sa_ref, sb_ref, gq_ref, gkv_ref,
                     wuq_ref, wuk_ref, wuv_ref,
                     ckv_o, kpe_o, q_o, k_o, v_o, cqn_s, ckv_s, kpe_s, *, rows, hb):
    @pl.when(pl.program_id(1) == 0)
    def _():
        def prep(r):
            sl = pl.ds(r, ROW_CHUNK)
            cqn_s[sl, :] = _rms(cq_ref[sl, :], gq_ref[...]).astype(BF16)
            ckv = _rms(ckv_ref[sl, :], gkv_ref[...])
            ckv_o[sl, :] = ckv
            ckv_s[sl, :] = ckv.astype(BF16)
            kpe = _rope_tile(kpe_ref[sl, :], c_ref[sl, :], sa_ref[sl, :], sb_ref[sl, :])
            kpe_o[sl, :] = kpe
            kpe_s[sl, :] = kpe.astype(BF16)
        _row_loop(rows, prep)

    for hh in range(hb):
        qf = _dot(cqn_s[...], wuq_ref[hh]) * MLA_SCALE
        q_o[hh, :, pl.ds(0, MLA_NOPE)] = qf[:, :MLA_NOPE].astype(BF16)
        q_o[hh, :, pl.ds(MLA_NOPE, KPE_PAD)] = _rope_tile(
            qf[:, MLA_NOPE:], c_ref[...], sa_ref[...], sb_ref[...]).astype(BF16)
        k_o[hh, :, pl.ds(0, MLA_NOPE)] = _dot(ckv_s[...], wuk_ref[hh]).astype(BF16)
        k_o[hh, :, pl.ds(MLA_NOPE, KPE_PAD)] = kpe_s[...]
        v_o[hh] = _dot(ckv_s[...], wuv_ref[hh]).astype(BF16)


def _mla_prep(proj, rope_c, rope_sa, rope_sb, gq, gkv, wuq, wuk, wuv, *, tm, hb):
    m = proj.shape[0]
    nh = wuq.shape[0]
    assert m % tm == 0 and nh % hb == 0
    vmem = (2 * tm * (Q_LORA + KV_LORA + 4 * KPE_PAD) * 4
            + 2 * hb * (Q_LORA * QK_PAD + 2 * KV_LORA * MLA_NOPE) * 2
            + 2 * tm * (KV_LORA + KPE_PAD) * 4
            + 2 * hb * tm * (2 * QK_PAD + MLA_V) * 2
            + tm * (Q_LORA + KV_LORA + KPE_PAD) * 2
            + 6 * tm * QK_PAD * 4)
    row_blk = lambda w, c: pl.BlockSpec((tm, w), lambda i, h: (i, c))
    head_w = lambda k, n: pl.BlockSpec((hb, k, n), lambda i, h: (h, 0, 0))
    head_o = lambda n: pl.BlockSpec((hb, tm, n), lambda i, h: (h, i, 0))
    return pl.pallas_call(
        functools.partial(_mla_prep_kernel, rows=tm, hb=hb),
        out_shape=(
            jax.ShapeDtypeStruct((m, KV_LORA), F32),
            jax.ShapeDtypeStruct((m, KPE_PAD), F32),
            jax.ShapeDtypeStruct((nh, m, QK_PAD), BF16),
            jax.ShapeDtypeStruct((nh, m, QK_PAD), BF16),
            jax.ShapeDtypeStruct((nh, m, MLA_V), BF16),
        ),
        grid=(m // tm, nh // hb),
        in_specs=[
            row_blk(Q_LORA, PROJ_CQ // Q_LORA),
            row_blk(KV_LORA, PROJ_CKV // KV_LORA),
            row_blk(KPE_PAD, PROJ_KPE // KPE_PAD),
            row_blk(KPE_PAD, 0), row_blk(KPE_PAD, 0), row_blk(KPE_PAD, 0),
            pl.BlockSpec((1, Q_LORA), lambda i, h: (0, 0)),
            pl.BlockSpec((1, KV_LORA), lambda i, h: (0, 0)),
            head_w(Q_LORA, QK_PAD), head_w(KV_LORA, MLA_NOPE), head_w(KV_LORA, MLA_V),
        ],
        out_specs=(row_blk(KV_LORA, 0), row_blk(KPE_PAD, 0), head_o(QK_PAD), head_o(QK_PAD), head_o(MLA_V)),
        scratch_shapes=[pltpu.VMEM((tm, Q_LORA), BF16), pltpu.VMEM((tm, KV_LORA), BF16),
                        pltpu.VMEM((tm, KPE_PAD), BF16)],
        compiler_params=_cparams(("parallel", "arbitrary"), vmem),
        name="mla_prep",
    )(proj, proj, proj, rope_c, rope_sa, rope_sb, gq, gkv, wuq, wuk, wuv)


def _softmax_step(s, v, m, l, acc):
    m_new = jnp.maximum(m, jnp.max(s, axis=-1, keepdims=True))
    p = jnp.exp(s - m_new)
    alpha = jnp.exp(m - m_new)
    l = alpha * l + jnp.sum(p, axis=-1, keepdims=True)
    acc = alpha * acc + _dot(p.astype(BF16), v)
    return m_new, l, acc


def _mla_prompt_kernel(q_ref, k_ref, v_ref, o_ref, *, seq, tile, hp):
    def q_tile(qi, carry):
        q0 = pl.multiple_of(qi * tile, tile)
        qs = [q_ref[h, pl.ds(q0, tile), :] for h in range(hp)]

        def scores(h, k0):
            return _dot_nt(qs[h], k_ref[h, pl.ds(k0, tile), :])

        def k_tile(ki, sts):
            k0 = pl.multiple_of(ki * tile, tile)
            return tuple(_softmax_step(scores(h, k0), v_ref[h, pl.ds(k0, tile), :], *sts[h]) for h in range(hp))

        init = (jnp.full((tile, 1), NEG_BIG, F32), jnp.zeros((tile, 1), F32), jnp.zeros((tile, MLA_V), F32))
        sts = lax.fori_loop(0, qi, k_tile, (init,) * hp)
        row = lax.broadcasted_iota(jnp.int32, (tile, tile), 0)
        col = lax.broadcasted_iota(jnp.int32, (tile, tile), 1)
        visible = col // CHUNK <= row // CHUNK
        for h in range(hp):
            s = jnp.where(visible, scores(h, q0), NEG_BIG)
            _, l, acc = _softmax_step(s, v_ref[h, pl.ds(q0, tile), :], *sts[h])
            o_ref[pl.ds(q0, tile), pl.ds(h * MLA_V, MLA_V)] = acc / l
        return carry

    lax.fori_loop(0, seq // tile, q_tile, 0)


def _mla_prompt(q, k, v, *, batch, seq, tile, hp):
    nh = q.shape[0]
    assert seq % tile == 0 and tile % CHUNK == 0 and nh % hp == 0
    vmem = hp * (2 * seq * (2 * QK_PAD + MLA_V) * 2 + 2 * seq * MLA_V * 4 + 8 * tile * tile * 4)
    return pl.pallas_call(
        functools.partial(_mla_prompt_kernel, seq=seq, tile=tile, hp=hp),
        out_shape=jax.ShapeDtypeStruct((batch * seq, nh * MLA_V), F32),
        grid=(batch, nh // hp),
        in_specs=[
            pl.BlockSpec((hp, seq, QK_PAD), lambda b, h: (h, b, 0)),
            pl.BlockSpec((hp, seq, QK_PAD), lambda b, h: (h, b, 0)),
            pl.BlockSpec((hp, seq, MLA_V), lambda b, h: (h, b, 0)),
        ],
        out_specs=pl.BlockSpec((seq, hp * MLA_V), lambda b, h: (b, h)),
        compiler_params=_cparams(("parallel", "parallel"), vmem),
        name="mla_prompt",
    )(q, k, v)


def _mla_sample_kernel(q_ref, wuk_ref, wuv_ref, ckv_ref, kpe_ref, ckvn_ref, kpen_ref, o_ref,
                       qlat_s, qpe_s, m_s, l_s, acc_s, *, nkt, nh, tq):
    kt = pl.program_id(1)
    rows = nh * tq

    @pl.when(kt == 0)
    def _():
        for h in range(nh):
            qh = q_ref[h]
            qlat_s[pl.ds(h * tq, tq), :] = _dot_nt(qh[:, :MLA_NOPE], wuk_ref[h]).astype(BF16)
            qpe_s[pl.ds(h * tq, tq), :] = qh[:, MLA_NOPE:MLA_NOPE + KPE_PAD]
        m_s[...] = jnp.full((rows, 1), NEG_BIG, F32)
        l_s[...] = jnp.zeros((rows, 1), F32)
        acc_s[...] = jnp.zeros((rows, KV_LORA), F32)

    def update(ckv, rope_scores):
        s = _dot_nt(qlat_s[...], ckv) + rope_scores
        m, l, acc = _softmax_step(s, ckv, m_s[...], l_s[...], acc_s[...])
        m_s[...] = m
        l_s[...] = l
        acc_s[...] = acc

    qpe = qpe_s[:, pl.ds(0, MLA_ROPE)]
    update(ckv_ref[0, 0].astype(BF16), _dot(qpe, kpe_ref[0, 0].astype(BF16)))

    @pl.when(kt == nkt - 1)
    def _():
        update(ckvn_ref[...].astype(BF16), _dot_nt(qpe, kpen_ref[:, pl.ds(0, MLA_ROPE)].astype(BF16)))
        o_lat = (acc_s[...] / l_s[...]).astype(BF16)
        for h in range(nh):
            o_ref[:, pl.ds(h * MLA_V, MLA_V)] = _dot(o_lat[h * tq:(h + 1) * tq], wuv_ref[h])


def _mla_sample(q, wuk, wuv, cache_ckv, cache_kpe_t, ckv_new, kpe_new, *, layer, row0, tk):
    nh = q.shape[0]
    _, batch, past, _ = cache_ckv.shape
    tq = S_SEQ
    assert past % tk == 0 and row0 % tq == 0
    assert (PAST_LEN + S_SEQ - 1) // CHUNK == PAST_LEN // CHUNK
    nkt = past // tk
    rows = nh * tq
    vmem = (2 * nh * tq * QK_PAD * 2 + 4 * nh * KV_LORA * MLA_NOPE * 2
            + 2 * tk * (KV_LORA + KPE_PAD) * 4 + tk * (KV_LORA + KPE_PAD) * 2
            + 2 * tq * (KV_LORA + KPE_PAD) * 4 + 2 * tq * nh * MLA_V * 4
            + rows * (KV_LORA + KPE_PAD) * 2 + rows * (KV_LORA + 2 * V7X_LANES) * 4
            + 4 * rows * tk * 4)
    rb = row0 // tq
    return pl.pallas_call(
        functools.partial(_mla_sample_kernel, nkt=nkt, nh=nh, tq=tq),
        out_shape=jax.ShapeDtypeStruct((batch * tq, nh * MLA_V), F32),
        grid=(batch, nkt),
        in_specs=[
            pl.BlockSpec((nh, tq, QK_PAD), lambda b, t: (0, rb + b, 0)),
            pl.BlockSpec((nh, KV_LORA, MLA_NOPE), lambda b, t: (0, 0, 0)),
            pl.BlockSpec((nh, KV_LORA, MLA_V), lambda b, t: (0, 0, 0)),
            pl.BlockSpec((1, 1, tk, KV_LORA), lambda b, t: (layer, b, t, 0)),
            pl.BlockSpec((1, 1, MLA_ROPE, tk), lambda b, t: (layer, b, 0, t)),
            pl.BlockSpec((tq, KV_LORA), lambda b, t: (rb + b, 0)),
            pl.BlockSpec((tq, KPE_PAD), lambda b, t: (rb + b, 0)),
        ],
        out_specs=pl.BlockSpec((tq, nh * MLA_V), lambda b, t: (b, 0)),
        scratch_shapes=[pltpu.VMEM((rows, KV_LORA), BF16), pltpu.VMEM((rows, KPE_PAD), BF16),
                        pltpu.VMEM((rows, 1), F32), pltpu.VMEM((rows, 1), F32),
                        pltpu.VMEM((rows, KV_LORA), F32)],
        compiler_params=_cparams(("parallel", "arbitrary"), vmem),
        name="mla_sample",
    )(q, wuk, wuv, cache_ckv, cache_kpe_t, ckv_new, kpe_new)


def _softplus(z):
    return jnp.maximum(z, 0.0) + jnp.log(1.0 + jnp.exp(-jnp.abs(z)))


def _later_keys_matrix(n):
    row = lax.broadcasted_iota(jnp.int32, (n, n), 0)
    col = lax.broadcasted_iota(jnp.int32, (n, n), 1)
    return (row > col).astype(BF16)


def _later_sum(x, u):
    r = x.shape[0]
    hi = x.astype(BF16)
    lo = (x - hi.astype(F32)).astype(BF16)
    e = _dot(jnp.concatenate([hi, lo], axis=0), u)
    return e[:r] + e[r:]


def _sb_weights(z, u, decay, mask=None):
    sp = _softplus(z)
    d = sp if mask is None else jnp.where(mask, sp, 0.0)
    a = jnp.exp(z - sp - _later_sum(d, u) - decay)
    if mask is not None:
        a = jnp.where(mask, a, 0.0)
    return a.astype(BF16), decay + jnp.sum(d, axis=-1, keepdims=True)


def _sb_prompt_kernel(q_ref, k_ref, v_ref, o_ref, qb, kb, vb, *, seq, tq, tk, hp):
    heads = range(hp)
    hcol = lambda h: pl.ds(h * SB_DIM, SB_DIM)

    def cast(r):
        sl = pl.ds(r, ROW_CHUNK)
        qb[sl, :] = (q_ref[sl, :] * SB_SCALE).astype(BF16)
        kb[sl, :] = k_ref[sl, :].astype(BF16)
        vb[sl, :] = v_ref[sl, :].astype(BF16)
    _row_loop(seq, cast)

    u = _later_keys_matrix(tk)
    per_q = tq // tk

    def q_tile(qi, carry):
        q0 = pl.multiple_of(qi * tq, tq)
        qs = [qb[pl.ds(q0, tq), hcol(h)] for h in heads]

        def key_tile(k0, sts, mask=None):
            out = []
            for h in heads:
                decay, acc = sts[h]
                a, decay = _sb_weights(_dot_nt(qs[h], kb[pl.ds(k0, tk), hcol(h)]), u, decay, mask)
                out.append((decay, acc + _dot(a, vb[pl.ds(k0, tk), hcol(h)])))
            return tuple(out)

        sts = ((jnp.zeros((tq, 1), F32), jnp.zeros((tq, SB_DIM), F32)),) * hp
        for d in reversed(range(per_q)):
            row = lax.broadcasted_iota(jnp.int32, (tq, tk), 0)
            col = lax.broadcasted_iota(jnp.int32, (tq, tk), 1) + d * tk
            sts = key_tile(pl.multiple_of(q0 + d * tk, tk), sts, mask=col < row)

        def older(step, sts):
            for d in reversed(range(per_q)):
                sts = key_tile(pl.multiple_of(q0 - (step + 1) * tq + d * tk, tk), sts)
            return sts

        sts = lax.fori_loop(0, qi, older, sts)
        for h in heads:
            o_ref[pl.ds(q0, tq), hcol(h)] = sts[h][1]
        return carry

    lax.fori_loop(0, seq // tq, q_tile, 0)


def _sb_prompt(proj, *, batch, seq, tq, tk, hp):
    assert seq % tq == 0 and tq % tk == 0 and seq % ROW_CHUNK == 0 and SB_HEADS % hp == 0
    w = hp * SB_DIM
    vmem = hp * (2 * 3 * seq * SB_DIM * 4 + 2 * seq * SB_DIM * 4 + 3 * seq * SB_DIM * 2 + 12 * tq * tk * 4)
    col = lambda c0: pl.BlockSpec((seq, w), lambda b, h: (b, c0 // w + h))
    return pl.pallas_call(
        functools.partial(_sb_prompt_kernel, seq=seq, tq=tq, tk=tk, hp=hp),
        out_shape=jax.ShapeDtypeStruct((batch * seq, SB_WIDTH), F32),
        grid=(batch, SB_HEADS // hp),
        in_specs=[col(PROJ_SQ), col(PROJ_SK), col(PROJ_SV)],
        out_specs=pl.BlockSpec((seq, w), lambda b, h: (b, h)),
        scratch_shapes=[pltpu.VMEM((seq, w), BF16)] * 3,
        compiler_params=_cparams(("parallel", "parallel"), vmem),
        name="sb_prompt",
    )(proj, proj, proj)


def _sb_sample_kernel(q_ref, kn_ref, vn_ref, kc_ref, vc_ref, o_ref, decay_s, acc_s, *, nkt, sub, tq):
    step = pl.program_id(1)
    rows = SB_HEADS * tq
    head = lambda h: pl.ds(h * SB_DIM, SB_DIM)
    hrow = lambda h: pl.ds(h * tq, tq)

    def scores(k_of_head):
        return jnp.concatenate(
            [_dot_nt((q_ref[:, head(h)] * SB_SCALE).astype(BF16), k_of_head(h)) for h in range(SB_HEADS)], axis=0)

    def weighted(a, v_of_head):
        return jnp.concatenate(
            [_dot(a[h * tq:(h + 1) * tq], v_of_head(h)) for h in range(SB_HEADS)], axis=0)

    @pl.when(step == 0)
    def _():
        pad = jnp.zeros((V7X_LANES - tq, SB_DIM), BF16)
        z = scores(lambda h: jnp.concatenate([kn_ref[:, head(h)].astype(BF16), pad], axis=0))
        t = lax.broadcasted_iota(jnp.int32, (rows, V7X_LANES), 0) % tq
        s = lax.broadcasted_iota(jnp.int32, (rows, V7X_LANES), 1)
        a, decay = _sb_weights(z, _later_keys_matrix(V7X_LANES), jnp.zeros((rows, 1), F32), mask=s < t)
        acc_s[...] = weighted(a, lambda h: jnp.concatenate([vn_ref[:, head(h)].astype(BF16), pad], axis=0))
        decay_s[...] = decay

    @pl.when(step > 0)
    def _():
        u = _later_keys_matrix(sub)
        decay = decay_s[...]
        acc = acc_s[...]
        for c in reversed(range(kc_ref.shape[2] // (sub * SB_HEADS))):
            keys = lambda h: pl.ds(c * sub * SB_HEADS + h, sub, stride=SB_HEADS)
            a, decay = _sb_weights(scores(lambda h: kc_ref[0, 0, keys(h), :].astype(BF16)), u, decay)
            acc = acc + weighted(a, lambda h: vc_ref[0, 0, keys(h), :].astype(BF16))
        decay_s[...] = decay
        acc_s[...] = acc

    @pl.when(step == nkt)
    def _():
        for h in range(SB_HEADS):
            o_ref[:, head(h)] = acc_s[hrow(h), :]


def _sb_sample(proj, cache_k, cache_v, *, layer, tk, sub):
    _, batch, past_rows, _ = cache_k.shape
    past = past_rows // SB_HEADS
    tq = S_SEQ
    assert past % tk == 0 and tk % sub == 0
    nkt = past // tk
    rows = SB_HEADS * tq
    vmem = (2 * 3 * tq * SB_WIDTH * 4 + 2 * 2 * tk * SB_WIDTH * 4 + 2 * tq * SB_WIDTH * 4
            + rows * (SB_DIM + V7X_LANES) * 4 + 2 * sub * SB_WIDTH * 2 + 16 * rows * sub * 4 + sub * sub * 2)
    newest_first = lambda b, s: (layer, b, nkt - jnp.maximum(s, 1), 0)
    col = lambda c0: pl.BlockSpec((tq, SB_WIDTH), lambda b, s: (b, c0 // SB_WIDTH))
    return pl.pallas_call(
        functools.partial(_sb_sample_kernel, nkt=nkt, sub=sub, tq=tq),
        out_shape=jax.ShapeDtypeStruct((batch * tq, SB_WIDTH), F32),
        grid=(batch, nkt + 1),
        in_specs=[col(PROJ_SQ), col(PROJ_SK), col(PROJ_SV),
                  pl.BlockSpec((1, 1, tk * SB_HEADS, SB_DIM), newest_first),
                  pl.BlockSpec((1, 1, tk * SB_HEADS, SB_DIM), newest_first)],
        out_specs=pl.BlockSpec((tq, SB_WIDTH), lambda b, s: (b, 0)),
        scratch_shapes=[pltpu.VMEM((rows, 1), F32), pltpu.VMEM((rows, SB_DIM), F32)],
        compiler_params=_cparams(("parallel", "arbitrary"), vmem),
        name="sb_sample",
    )(proj, proj, proj, cache_k, cache_v)


def _sb_rows_kernel(*refs, tt, layers):
    srcs, (ko_ref, vo_ref) = refs[:2 * layers], refs[2 * layers:]
    for l in range(layers):
        @pl.when(pl.program_id(0) == l)
        def _():
            for src, dst in ((srcs[2 * l], ko_ref), (srcs[2 * l + 1], vo_ref)):
                for h in range(SB_HEADS):
                    dst[0, pl.ds(h, tt, stride=SB_HEADS), :] = src[:, pl.ds(h * SB_DIM, SB_DIM)]


def _sb_rows(projs, *, tt):
    layers = len(projs)
    m = projs[0].shape[0]
    assert m % tt == 0
    nt = m // tt
    vmem = (2 * layers + 2) * 2 * tt * SB_WIDTH * 4

    def col(l, c0):
        return pl.BlockSpec((tt, SB_WIDTH), lambda g, i: (jnp.clip(i + (g - l) * nt, 0, nt - 1), c0 // SB_WIDTH))

    out = jax.ShapeDtypeStruct((layers, m * SB_HEADS, SB_DIM), F32)
    return pl.pallas_call(
        functools.partial(_sb_rows_kernel, tt=tt, layers=layers),
        out_shape=(out, out),
        grid=(layers, nt),
        in_specs=[col(l, c0) for l in range(layers) for c0 in (PROJ_SK, PROJ_SV)],
        out_specs=(pl.BlockSpec((1, tt * SB_HEADS, SB_DIM), lambda g, i: (g, i, 0)),) * 2,
        compiler_params=_cparams(("arbitrary", "arbitrary"), vmem),
        name="sb_rows",
    )(*[p for p in projs for _ in range(2)])


def _pool_kernel(u_ref, hist_ref, w_ref, scale_ref, o_ref, carry_s, *, tt, pos0):
    ti = pl.program_id(1)
    halo = carry_s.shape[0]

    @pl.when(ti == 0)
    def _():
        carry_s[...] = hist_ref[0]

    pos = pos0 + ti * tt + lax.broadcasted_iota(jnp.int32, (tt, 1), 0)
    for g, w in enumerate(POOL_WINDOWS):
        cols = pl.ds(g * POOL_GROUP, POOL_GROUP)
        tok = u_ref[:, cols]
        s = jnp.concatenate([carry_s[:, cols], tok], axis=0)
        shift = 1
        while shift < w:
            s = s + pltpu.roll(s, shift, 0)
            shift *= 2
        cnt = jnp.minimum(pos + 1, w).astype(F32)
        pooled = s[halo:] / cnt - tok
        o_ref[:, cols] = (_dot(pooled.astype(BF16), w_ref[g]) * scale_ref[:, cols]).astype(BF16)
    carry_s[...] = u_ref[pl.ds(tt - halo, halo), :]


def _pool(proj, hist, w_pool, scale, *, batch, seq, tt, pos0):
    halo = hist.shape[1]
    assert seq % tt == 0 and tt >= halo and halo > max(POOL_WINDOWS) - 1
    nt = seq // tt
    vmem = (2 * tt * POOL_WIDTH * 4 + 2 * halo * POOL_WIDTH * 4 + 2 * POOL_WIDTH * POOL_GROUP * 2
            + 2 * tt * POOL_WIDTH * 2 + halo * POOL_WIDTH * 4 + 8 * (tt + halo) * POOL_GROUP * 4)
    return pl.pallas_call(
        functools.partial(_pool_kernel, tt=tt, pos0=pos0),
        out_shape=jax.ShapeDtypeStruct((batch * seq, POOL_WIDTH), BF16),
        grid=(batch, nt),
        in_specs=[
            pl.BlockSpec((tt, POOL_WIDTH), lambda b, t: (b * nt + t, PROJ_U // POOL_WIDTH)),
            pl.BlockSpec((1, halo, POOL_WIDTH), lambda b, t: (b, 0, 0)),
            pl.BlockSpec((len(POOL_WINDOWS), POOL_GROUP, POOL_GROUP), lambda b, t: (0, 0, 0)),
            pl.BlockSpec((1, POOL_WIDTH), lambda b, t: (0, 0)),
        ],
        out_specs=pl.BlockSpec((tt, POOL_WIDTH), lambda b, t: (b * nt + t, 0)),
        scratch_shapes=[pltpu.VMEM((halo, POOL_WIDTH), F32)],
        compiler_params=_cparams(("parallel", "arbitrary"), vmem),
        name="pool",
    )(proj, hist, w_pool, scale)


def _outproj_kernel(mla_ref, pool_ref, sb_ref, x_ref, gm_ref, gs_ref, w_ref, o_ref, mix_s, *, rows):
    wm = mla_ref.shape[1]
    wp = pool_ref.shape[1]

    @pl.when(pl.program_id(1) == 0)
    def _():
        def prep(r):
            sl = pl.ds(r, ROW_CHUNK)
            mix_s[sl, pl.ds(0, wm)] = _rms(mla_ref[sl, :], gm_ref[...]).astype(BF16)
            mix_s[sl, pl.ds(wm, wp)] = pool_ref[sl, :]
            mix_s[sl, pl.ds(wm + wp, sb_ref.shape[1])] = _rms(sb_ref[sl, :], gs_ref[...]).astype(BF16)
        _row_loop(rows, prep)

    o_ref[...] = x_ref[...] + _dot(mix_s[...], w_ref[...])


def _outproj(o_mla, o_pool, o_sb, x, g_mla, g_sb, w, *, tm, tn):
    m, d = x.shape
    wm, wp, ws = o_mla.shape[1], o_pool.shape[1], o_sb.shape[1]
    k = wm + wp + ws
    assert m % tm == 0 and d % tn == 0 and w.shape == (k, d)
    vmem = (2 * tm * (wm + ws) * 4 + 2 * tm * wp * 2 + 2 * 2 * tm * tn * 4 + 2 * k * tn * 2
            + tm * k * 2 + 2 * tm * tn * 4)
    full = lambda n: pl.BlockSpec((tm, n), lambda i, j: (i, 0))
    return pl.pallas_call(
        functools.partial(_outproj_kernel, rows=tm),
        out_shape=jax.ShapeDtypeStruct((m, d), F32),
        grid=(m // tm, d // tn),
        in_specs=[full(wm), full(wp), full(ws),
                  pl.BlockSpec((tm, tn), lambda i, j: (i, j)),
                  pl.BlockSpec((1, wm), lambda i, j: (0, 0)),
                  pl.BlockSpec((1, ws), lambda i, j: (0, 0)),
                  pl.BlockSpec((k, tn), lambda i, j: (0, j))],
        out_specs=pl.BlockSpec((tm, tn), lambda i, j: (i, j)),
        scratch_shapes=[pltpu.VMEM((tm, k), BF16)],
        compiler_params=_cparams(("parallel", "arbitrary"), vmem),
        name="outproj",
    )(o_mla, o_pool, o_sb, x, g_mla, g_sb, w)


def _rope_tables(pos):
    half = MLA_ROPE // 2
    inv = 1.0 / (ROPE_THETA ** (jnp.arange(half, dtype=F32) / half))
    ang = pos.astype(F32)[:, None] * inv[None, :]
    cos, sin = jnp.cos(ang), jnp.sin(ang)
    z = lambda n: jnp.zeros((pos.shape[0], n), F32)
    return (jnp.concatenate([cos, cos, z(KPE_PAD - MLA_ROPE)], axis=1),
            jnp.concatenate([-sin, z(KPE_PAD - half)], axis=1),
            jnp.concatenate([z(half), sin, z(KPE_PAD - MLA_ROPE)], axis=1))


W_IN_PIECES = (
    (0, Q_LORA, PROJ_CQ),
    (Q_LORA, KV_LORA, PROJ_CKV),
    (Q_LORA + KV_LORA, MLA_ROPE, PROJ_KPE),
    (Q_LORA + KV_LORA + MLA_ROPE, POOL_WIDTH, PROJ_U),
    (Q_LORA + KV_LORA + MLA_ROPE + POOL_WIDTH, SB_WIDTH, PROJ_SQ),
    (Q_LORA + KV_LORA + MLA_ROPE + POOL_WIDTH + SB_WIDTH, SB_WIDTH, PROJ_SK),
    (Q_LORA + KV_LORA + MLA_ROPE + POOL_WIDTH + 2 * SB_WIDTH, SB_WIDTH, PROJ_SV),
)


def _w_in_kernel(w_ref, o_ref):
    cols = o_ref.shape[1]
    o_ref[pl.ds(PROJ_KPE, KPE_PAD), :] = jnp.zeros((KPE_PAD, cols), BF16)
    for src, width, dst in W_IN_PIECES:
        o_ref[pl.ds(dst, width), :] = w_ref[0, pl.ds(src, width), :].astype(BF16)


def _w_in_bf16(w_in_t, l, *, tc):
    _, n, d = w_in_t.shape
    assert d % tc == 0
    vmem = 2 * n * tc * 4 + 2 * PROJ_W * tc * 2 + PROJ_W * tc * 4
    return pl.pallas_call(
        _w_in_kernel,
        out_shape=jax.ShapeDtypeStruct((PROJ_W, d), BF16),
        grid=(d // tc,),
        in_specs=[pl.BlockSpec((1, n, tc), lambda i: (l, 0, i))],
        out_specs=pl.BlockSpec((PROJ_W, tc), lambda i: (0, i)),
        compiler_params=_cparams(("parallel",), vmem),
        name="w_in_bf16",
    )(w_in_t)


def _cast_kernel(w_ref, o_ref):
    o_ref[...] = w_ref[0].astype(BF16)


def _layer_bf16(w, l, *, tr):
    _, r, c = w.shape
    assert r % tr == 0
    return pl.pallas_call(
        _cast_kernel,
        out_shape=jax.ShapeDtypeStruct((r, c), BF16),
        grid=(r // tr,),
        in_specs=[pl.BlockSpec((1, tr, c), lambda i: (l, i, 0))],
        out_specs=pl.BlockSpec((tr, c), lambda i: (i, 0)),
        compiler_params=_cparams(("parallel",), 2 * tr * c * 6),
        name="layer_bf16",
    )(w)


def _layer_weights(l, w_in, w_uq, w_uk, w_uv, w_pool, w_out):
    wuq = w_uq[l].reshape(Q_LORA, MLA_HEADS, MLA_NOPE + MLA_ROPE)
    wuq = jnp.pad(wuq, ((0, 0), (0, 0), (0, QK_PAD - MLA_NOPE - MLA_ROPE)))
    wuk = w_uk[l].transpose(1, 0, 2).astype(BF16)
    wuv = w_uv[l].transpose(1, 0, 2).astype(BF16)
    return dict(
        w_in=_w_in_bf16(jnp.swapaxes(w_in, 1, 2), l, tc=256),
        wuq=wuq.transpose(1, 0, 2).astype(BF16),
        wuk=wuk,
        wuv=wuv,
        w_pool=w_pool[l].astype(BF16),
        w_out=_layer_bf16(w_out, l, tr=512),
    )


def kernel(x_prompt, x_sample, cache_ckv, cache_kpe, state_pool, cache_sb_k, cache_sb_v, g_ffn1, w1_gate, w1_up, w1_down, g_mix, w_in, g_qnorm, w_uq, g_kvnorm, w_uk, w_uv, w_pool, pool_scale, g_mla_out, g_sb_out, w_out, g_ffn2, w2_gate, w2_up, w2_down, g_final):
    mp, ms = P_BATCH * P_SEQ, S_BATCH * S_SEQ
    tm_p, tf = 512, 256
    row = lambda g: g.reshape(1, -1)
    gfin = row(g_final)

    rope_p = _rope_tables(jnp.tile(jnp.arange(P_SEQ), P_BATCH))
    rope_s = _rope_tables(PAST_LEN + jnp.tile(jnp.arange(S_SEQ), S_BATCH))
    hist_p = jnp.zeros((P_BATCH, POOL_HIST + 1, POOL_WIDTH), F32)
    cache_kpe_t = jnp.swapaxes(cache_kpe, 2, 3)
    cache_k = cache_sb_k.reshape(DEPTH, S_BATCH, PAST_LEN * SB_HEADS, SB_DIM)
    cache_v = cache_sb_v.reshape(DEPTH, S_BATCH, PAST_LEN * SB_HEADS, SB_DIM)

    hp = x_prompt.reshape(mp, D_MODEL)
    hs = x_sample.reshape(ms, D_MODEL)
    outs = {k: [] for k in ("p_proj", "p_ckv", "p_kpe", "p_pool", "s_proj", "s_ckv", "s_kpe", "s_pool")}

    def record(tag, batch, seq, proj, ckv, kpe):
        outs[tag + "_proj"].append(proj)
        outs[tag + "_ckv"].append(ckv.reshape(batch, seq, KV_LORA))
        outs[tag + "_kpe"].append(kpe[:, :MLA_ROPE].reshape(batch, seq, MLA_ROPE))
        outs[tag + "_pool"].append(
            proj.reshape(batch, seq, PROJ_W)[:, seq - POOL_HIST:, PROJ_U:PROJ_U + POOL_WIDTH])

    for l in range(DEPTH):
        w = _layer_weights(l, w_in, w_uq, w_uk, w_uv, w_pool, w_out)
        last = l == DEPTH - 1
        gq, gkv = row(g_qnorm[l]), row(g_kvnorm[l])
        pscale = row(pool_scale[l])
        hist_s = jnp.pad(state_pool[l], ((0, 0), (1, 0), (0, 0)))

        def mixer_inputs(h, tm, rope):
            proj = _inproj(h, row(g_mix[l]), w["w_in"], tm=tm, tn=PROJ_W // 5)
            return proj, _mla_prep(proj, *rope, gq, gkv, w["wuq"], w["wuk"], w["wuv"], tm=tm, hb=8)

        def mix(h, tm, o_mla, o_pool, o_sb):
            return _outproj(o_mla, o_pool, o_sb, h, row(g_mla_out[l]), row(g_sb_out[l]), w["w_out"], tm=tm, tn=1024)

        hs, *w1 = _ffn(hs, row(g_ffn1[l]), w1_gate, w1_up, w1_down, gfin, tm=ms, tf=tf // 2,
                       final_norm=False, cast_layer=l)
        proj, (ckv, kpe, q, _, _) = mixer_inputs(hs, ms, rope_s)
        o_mla = _mla_sample(q, w["wuk"], w["wuv"], cache_ckv, cache_kpe_t, ckv, kpe, layer=l, row0=0, tk=2048)
        o_sb = _sb_sample(proj, cache_k, cache_v, layer=l, tk=1024, sub=256)
        o_pool = _pool(proj, hist_s, w["w_pool"], pscale, batch=S_BATCH, seq=S_SEQ, tt=S_SEQ, pos0=PAST_LEN)
        hs = mix(hs, ms, o_mla, o_pool, o_sb)
        hs, *w2 = _ffn(hs, row(g_ffn2[l]), w2_gate, w2_up, w2_down, gfin, tm=ms, tf=tf // 2,
                       final_norm=last, cast_layer=l)
        record("s", S_BATCH, S_SEQ, proj, ckv, kpe)

        hp = _ffn(hp, row(g_ffn1[l]), *w1, gfin, tm=tm_p, tf=tf, final_norm=False)
        proj, (ckv, kpe, q, k, v) = mixer_inputs(hp, tm_p, rope_p)
        o_mla = _mla_prompt(q, k, v, batch=P_BATCH, seq=P_SEQ, tile=512, hp=2)
        o_sb = _sb_prompt(proj, batch=P_BATCH, seq=P_SEQ, tq=512, tk=256, hp=1)
        o_pool = _pool(proj, hist_p, w["w_pool"], pscale, batch=P_BATCH, seq=P_SEQ, tt=512, pos0=0)
        hp = mix(hp, tm_p, o_mla, o_pool, o_sb)
        hp = _ffn(hp, row(g_ffn2[l]), *w2, gfin, tm=tm_p, tf=tf, final_norm=last)
        record("p", P_BATCH, P_SEQ, proj, ckv, kpe)

    st = lambda k: jnp.stack(outs[k])
    heads = lambda a, batch, seq: a.reshape(DEPTH, batch, seq, SB_HEADS, SB_DIM)
    p_sbk, p_sbv = _sb_rows(outs["p_proj"], tt=tm_p)
    s_sbk, s_sbv = _sb_rows(outs["s_proj"], tt=ms)
    return (hp.reshape(P_BATCH, P_SEQ, D_MODEL), hs.reshape(S_BATCH, S_SEQ, D_MODEL),
            st("p_ckv"), st("p_kpe"), st("p_pool"), heads(p_sbk, P_BATCH, P_SEQ), heads(p_sbv, P_BATCH, P_SEQ),
            st("s_ckv"), st("s_kpe"), st("s_pool"), heads(s_sbk, S_BATCH, S_SEQ), heads(s_sbv, S_BATCH, S_SEQ))
```

```python
import functools
import math

import jax
import jax.numpy as jnp
from jax import lax
from jax.experimental import pallas as pl
from jax.experimental.pallas import tpu as pltpu

F32 = jnp.float32
BF16 = jnp.bfloat16

D_MODEL = 4096
D_FF = 11008
DEPTH = 2
P_BATCH, P_SEQ = 4, 2048
S_BATCH, S_SEQ, PAST_LEN = 16, 16, 4096
CHUNK = 64
MLA_HEADS, MLA_NOPE, MLA_ROPE, MLA_V = 16, 128, 64, 128
Q_LORA, KV_LORA = 1024, 512
ROPE_THETA = 10000.0
MLA_SCALE = 1.0 / math.sqrt(MLA_NOPE + MLA_ROPE)
POOL_WINDOWS = (2, 4, 8, 16)
POOL_GROUP = 256
POOL_WIDTH = 1024
POOL_HIST = 15
SB_HEADS, SB_DIM = 8, 128
SB_WIDTH = 1024
SB_SCALE = 1.0 / math.sqrt(SB_DIM)
EPS = 1e-6

V7X_LANES = 128
V7X_VMEM_LIMIT_BYTES = 60000 * 1024
VALUE_SPILL_BYTES = 8 * 1024 * 1024

PROJ_CQ = 0
PROJ_U = 1024
PROJ_SQ = 2048
PROJ_SK = 3072
PROJ_SV = 4096
PROJ_CKV = 5120
PROJ_KPE = 5632
PROJ_W = 5760
KPE_PAD = 128
QK_PAD = 256

ROW_CHUNK = 64
NEG_BIG = -1e30


def _cparams(semantics, vmem_bytes):
    assert vmem_bytes <= V7X_VMEM_LIMIT_BYTES, vmem_bytes
    limit = min(V7X_VMEM_LIMIT_BYTES, vmem_bytes + VALUE_SPILL_BYTES)
    return pltpu.CompilerParams(dimension_semantics=semantics, vmem_limit_bytes=int(limit))


def _rms(x, g):
    ms = jnp.mean(x * x, axis=-1, keepdims=True)
    return (x * lax.rsqrt(ms + EPS)) * g


def _dot(a, b):
    return jnp.dot(a, b, preferred_element_type=F32)


def _dot_nt(a, b):
    return lax.dot_general(a, b, (((1,), (1,)), ((), ())), preferred_element_type=F32)


def _row_loop(rows, fn):
    def body(c, carry):
        fn(pl.multiple_of(c * ROW_CHUNK, ROW_CHUNK))
        return carry
    lax.fori_loop(0, rows // ROW_CHUNK, body, 0)


def _ffn_step(x_ref, g_ref, wg_ref, wu_ref, wd_ref, gf_ref, o_ref, xn_ref, *, nj, rows, final_norm, side_work=None):
    j = pl.program_id(1)

    @pl.when(j == 0)
    def _():
        def prep(r):
            x = x_ref[pl.ds(r, ROW_CHUNK), :]
            xn_ref[pl.ds(r, ROW_CHUNK), :] = _rms(x, g_ref[...]).astype(BF16)
            o_ref[pl.ds(r, ROW_CHUNK), :] = jnp.zeros((ROW_CHUNK, x.shape[1]), F32)
        _row_loop(rows, prep)

    if side_work is not None:
        side_work()
    xn = xn_ref[...]
    gate = _dot(xn, wg_ref[...])
    up = _dot(xn, wu_ref[...])
    h = (gate * (1.0 / (1.0 + jnp.exp(-gate))) * up).astype(BF16)
    o_ref[...] += _dot(h, wd_ref[...])

    @pl.when(j == nj - 1)
    def _():
        def fin(r):
            y = x_ref[pl.ds(r, ROW_CHUNK), :] + 0.5 * o_ref[pl.ds(r, ROW_CHUNK), :]
            if final_norm:
                y = _rms(y, gf_ref[...])
            o_ref[pl.ds(r, ROW_CHUNK), :] = y
        _row_loop(rows, fin)


def _ffn_kernel(x_ref, g_ref, wg_ref, wu_ref, wd_ref, gf_ref, o_ref, xn_ref, **kw):
    _ffn_step(x_ref, g_ref, wg_ref, wu_ref, wd_ref, gf_ref, o_ref, xn_ref, **kw)


def _ffn_cast_kernel(x_ref, g_ref, wg_ref, wu_ref, wd_ref, gf_ref, o_ref, wg_o, wu_o, wd_o, xn_ref, **kw):
    for src, dst in ((wg_ref, wg_o), (wu_ref, wu_o), (wd_ref, wd_o)):
        dst[...] = src[0].astype(BF16)
    _ffn_step(x_ref, g_ref, wg_o, wu_o, wd_o, gf_ref, o_ref, xn_ref, **kw)


def _ffn_next_cast_kernel(x_ref, g_ref, wg_ref, wu_ref, wd_ref, gf_ref, ng_ref, nu_ref, nd_ref,
                          o_ref, og_ref, ou_ref, od_ref, xn_ref, **kw):
    def cast_slice():
        for src, dst in ((ng_ref, og_ref), (nu_ref, ou_ref), (nd_ref, od_ref)):
            dst[...] = src[0].astype(BF16)
    _ffn_step(x_ref, g_ref, wg_ref, wu_ref, wd_ref, gf_ref, o_ref, xn_ref, side_work=cast_slice, **kw)


def _ffn_vmem(tm, d, tf, weight_bytes):
    return (2 * tm * d * 4
            + tm * d * 2
            + 2 * tm * d * 4
            + 2 * 3 * d * tf * weight_bytes
            + 4 * tm * tf * 4)


def _ffn(x, g, wg, wu, wd, g_final, *, tm, tf, final_norm, cast_layer=None):
    m, d = x.shape
    cast = cast_layer is not None
    f = wg.shape[-1]
    assert m % tm == 0 and f % tf == 0 and tm % ROW_CHUNK == 0 and (not cast or m == tm)
    nj = f // tf
    vmem = _ffn_vmem(tm, d, tf, 6 if cast else 2)
    w_specs = [pl.BlockSpec((d, tf), lambda i, j: (0, j)),
               pl.BlockSpec((d, tf), lambda i, j: (0, j)),
               pl.BlockSpec((tf, d), lambda i, j: (j, 0))]
    w_in_specs = w_specs
    if cast:
        w_in_specs = [pl.BlockSpec((1, d, tf), lambda i, j: (cast_layer, 0, j)),
                      pl.BlockSpec((1, d, tf), lambda i, j: (cast_layer, 0, j)),
                      pl.BlockSpec((1, tf, d), lambda i, j: (cast_layer, j, 0))]
    x_out = jax.ShapeDtypeStruct((m, d), F32)
    x_spec = pl.BlockSpec((tm, d), lambda i, j: (i, 0))
    return pl.pallas_call(
        functools.partial(_ffn_cast_kernel if cast else _ffn_kernel, nj=nj, rows=tm, final_norm=final_norm),
        out_shape=(x_out,) + tuple(jax.ShapeDtypeStruct(w.shape[1:], BF16) for w in (wg, wu, wd)) if cast else x_out,
        grid=(m // tm, nj),
        in_specs=[
            pl.BlockSpec((tm, d), lambda i, j: (i, 0)),
            pl.BlockSpec((1, d), lambda i, j: (0, 0)),
            *w_in_specs,
            pl.BlockSpec((1, d), lambda i, j: (0, 0)),
        ],
        out_specs=(x_spec, *w_specs) if cast else x_spec,
        scratch_shapes=[pltpu.VMEM((tm, d), BF16)],
        compiler_params=_cparams(("parallel", "arbitrary"), vmem),
        name="ffn_cast" if cast else "ffn",
    )(x, g, wg, wu, wd, g_final)


NEXT_CAST_ROWS = 16
NEXT_CAST_COL_SPLIT = 2


def _ffn_next_cast(x, g, wg, wu, wd, g_final, next_w, next_layer, *, tm, tf, final_norm):
    m, d = x.shape
    f = wg.shape[-1]
    nj = f // tf
    steps = (m // tm) * nj
    r, split = NEXT_CAST_ROWS, NEXT_CAST_COL_SPLIT
    cw = f // split
    up_blocks = (d // r) * split
    assert m % tm == 0 and f % tf == 0 and tm % ROW_CHUNK == 0
    assert steps * r == f and up_blocks <= steps and cw % V7X_LANES == 0 and d % r == 0
    step = lambda i, j: i * nj + j
    up_idx = lambda i, j: jnp.minimum(step(i, j), up_blocks - 1)
    vmem = _ffn_vmem(tm, d, tf, 2) + 2 * (2 * r * cw + r * d) * 6
    w_specs = [pl.BlockSpec((d, tf), lambda i, j: (0, j)),
               pl.BlockSpec((d, tf), lambda i, j: (0, j)),
               pl.BlockSpec((tf, d), lambda i, j: (j, 0))]
    up_in = pl.BlockSpec((1, r, cw), lambda i, j: (next_layer, up_idx(i, j) // split, up_idx(i, j) % split))
    up_out = pl.BlockSpec((r, cw), lambda i, j: (up_idx(i, j) // split, up_idx(i, j) % split))
    x_spec = pl.BlockSpec((tm, d), lambda i, j: (i, 0))
    return pl.pallas_call(
        functools.partial(_ffn_next_cast_kernel, nj=nj, rows=tm, final_norm=final_norm),
        out_shape=(jax.ShapeDtypeStruct((m, d), F32),) + tuple(
            jax.ShapeDtypeStruct(w.shape[1:], BF16) for w in next_w),
        grid=(m // tm, nj),
        in_specs=[
            x_spec,
            pl.BlockSpec((1, d), lambda i, j: (0, 0)),
            *w_specs,
            pl.BlockSpec((1, d), lambda i, j: (0, 0)),
            up_in, up_in,
            pl.BlockSpec((1, r, d), lambda i, j: (next_layer, step(i, j), 0)),
        ],
        out_specs=(x_spec, up_out, up_out, pl.BlockSpec((r, d), lambda i, j: (step(i, j), 0))),
        scratch_shapes=[pltpu.VMEM((tm, d), BF16)],
        compiler_params=_cparams(("arbitrary", "arbitrary"), vmem),
        name="ffn_next_cast",
    )(x, g, wg, wu, wd, g_final, *next_w)


def _inproj_kernel(x_ref, g_ref, w_ref, o_ref, xn_ref, *, rows):
    @pl.when(pl.program_id(1) == 0)
    def _():
        def prep(r):
            xn_ref[pl.ds(r, ROW_CHUNK), :] = _rms(x_ref[pl.ds(r, ROW_CHUNK), :], g_ref[...]).astype(BF16)
        _row_loop(rows, prep)

    o_ref[...] = _dot_nt(xn_ref[...], w_ref[...])


def _inproj(x, g, w_t, *, tm, tn):
    m, d = x.shape
    n = w_t.shape[0]
    assert m % tm == 0 and n % tn == 0
    vmem = 2 * tm * d * 4 + tm * d * 2 + 2 * d * tn * 2 + 3 * tm * tn * 4
    return pl.pallas_call(
        functools.partial(_inproj_kernel, rows=tm),
        out_shape=jax.ShapeDtypeStruct((m, n), F32),
        grid=(m // tm, n // tn),
        in_specs=[
            pl.BlockSpec((tm, d), lambda i, j: (i, 0)),
            pl.BlockSpec((1, d), lambda i, j: (0, 0)),
            pl.BlockSpec((tn, d), lambda i, j: (j, 0)),
        ],
        out_specs=pl.BlockSpec((tm, tn), lambda i, j: (i, j)),
        scratch_shapes=[pltpu.VMEM((tm, d), BF16)],
        compiler_params=_cparams(("parallel", "arbitrary"), vmem),
        name="inproj",
    )(x, g, w_t)


def _rope_tile(t, c, sa, sb):
    return t * c + pltpu.roll(t, KPE_PAD - MLA_ROPE // 2, 1) * sa + pltpu.roll(t, MLA_ROPE // 2, 1) * sb


def _mla_prep_kernel(cq_ref, ckv_ref, kpe_ref, c_ref, sa_ref, sb_ref, gq_ref, gkv_ref,
                     wuq_ref, wuk_ref, wuv_ref,
                     ckv_o, kpe_o, q_o, k_o, v_o, cqn_s, ckv_s, kpe_s, *, rows, hb):
    @pl.when(pl.program_id(1) == 0)
    def _():
        def prep(r):
            sl = pl.ds(r, ROW_CHUNK)
            cqn_s[sl, :] = _rms(cq_ref[sl, :], gq_ref[...]).astype(BF16)
            ckv = _rms(ckv_ref[sl, :], gkv_ref[...])
            ckv_o[sl, :] = ckv
            ckv_s[sl, :] = ckv.astype(BF16)
            kpe = _rope_tile(kpe_ref[sl, :], c_ref[sl, :], sa_ref[sl, :], sb_ref[sl, :])
            kpe_o[sl, :] = kpe
            kpe_s[sl, :] = kpe.astype(BF16)
        _row_loop(rows, prep)

    for hh in range(hb):
        qf = _dot(cqn_s[...], wuq_ref[hh]) * MLA_SCALE
        q_o[hh, :, pl.ds(0, MLA_NOPE)] = qf[:, :MLA_NOPE].astype(BF16)
        q_o[hh, :, pl.ds(MLA_NOPE, KPE_PAD)] = _rope_tile(
            qf[:, MLA_NOPE:], c_ref[...], sa_ref[...], sb_ref[...]).astype(BF16)
        k_o[hh, :, pl.ds(0, MLA_NOPE)] = _dot(ckv_s[...], wuk_ref[hh]).astype(BF16)
        k_o[hh, :, pl.ds(MLA_NOPE, KPE_PAD)] = kpe_s[...]
        v_o[hh] = _dot(ckv_s[...], wuv_ref[hh]).astype(BF16)


def _mla_prep(proj, rope_c, rope_sa, rope_sb, gq, gkv, wuq, wuk, wuv, *, tm, hb):
    m = proj.shape[0]
    nh = wuq.shape[0]
    assert m % tm == 0 and nh % hb == 0
    vmem = (2 * tm * (Q_LORA + KV_LORA + 4 * KPE_PAD) * 4
            + 2 * hb * (Q_LORA * QK_PAD + 2 * KV_LORA * MLA_NOPE) * 2
            + 2 * tm * (KV_LORA + KPE_PAD) * 4
            + 2 * hb * tm * (2 * QK_PAD + MLA_V) * 2
            + tm * (Q_LORA + KV_LORA + KPE_PAD) * 2
            + 6 * tm * QK_PAD * 4)
    row_blk = lambda w, c: pl.BlockSpec((tm, w), lambda i, h: (i, c))
    head_w = lambda k, n: pl.BlockSpec((hb, k, n), lambda i, h: (h, 0, 0))
    head_o = lambda n: pl.BlockSpec((hb, tm, n), lambda i, h: (h, i, 0))
    return pl.pallas_call(
        functools.partial(_mla_prep_kernel, rows=tm, hb=hb),
        out_shape=(
            jax.ShapeDtypeStruct((m, KV_LORA), F32),
            jax.ShapeDtypeStruct((m, KPE_PAD), F32),
            jax.ShapeDtypeStruct((nh, m, QK_PAD), BF16),
            jax.ShapeDtypeStruct((nh, m, QK_PAD), BF16),
            jax.ShapeDtypeStruct((nh, m, MLA_V), BF16),
        ),
        grid=(m // tm, nh // hb),
        in_specs=[
            row_blk(Q_LORA, PROJ_CQ // Q_LORA),
            row_blk(KV_LORA, PROJ_CKV // KV_LORA),
            row_blk(KPE_PAD, PROJ_KPE // KPE_PAD),
            row_blk(KPE_PAD, 0), row_blk(KPE_PAD, 0), row_blk(KPE_PAD, 0),
            pl.BlockSpec((1, Q_LORA), lambda i, h: (0, 0)),
            pl.BlockSpec((1, KV_LORA), lambda i, h: (0, 0)),
            head_w(Q_LORA, QK_PAD), head_w(KV_LORA, MLA_NOPE), head_w(KV_LORA, MLA_V),
        ],
        out_specs=(row_blk(KV_LORA, 0), row_blk(KPE_PAD, 0), head_o(QK_PAD), head_o(QK_PAD), head_o(MLA_V)),
        scratch_shapes=[pltpu.VMEM((tm, Q_LORA), BF16), pltpu.VMEM((tm, KV_LORA), BF16),
                        pltpu.VMEM((tm, KPE_PAD), BF16)],
        compiler_params=_cparams(("parallel", "arbitrary"), vmem),
        name="mla_prep",
    )(proj, proj, proj, rope_c, rope_sa, rope_sb, gq, gkv, wuq, wuk, wuv)


def _softmax_step(s, v, m, l, acc):
    m_new = jnp.maximum(m, jnp.max(s, axis=-1, keepdims=True))
    p = jnp.exp(s - m_new)
    alpha = jnp.exp(m - m_new)
    l = alpha * l + jnp.sum(p, axis=-1, keepdims=True)
    acc = alpha * acc + _dot(p.astype(BF16), v)
    return m_new, l, acc


def _mla_prompt_kernel(q_ref, k_ref, v_ref, o_ref, *, seq, tile, hp):
    def q_tile(qi, carry):
        q0 = pl.multiple_of(qi * tile, tile)
        qs = [q_ref[h, pl.ds(q0, tile), :] for h in range(hp)]

        def scores(h, k0):
            return _dot_nt(qs[h], k_ref[h, pl.ds(k0, tile), :])

        def k_tile(ki, sts):
            k0 = pl.multiple_of(ki * tile, tile)
            return tuple(_softmax_step(scores(h, k0), v_ref[h, pl.ds(k0, tile), :], *sts[h]) for h in range(hp))

        init = (jnp.full((tile, 1), NEG_BIG, F32), jnp.zeros((tile, 1), F32), jnp.zeros((tile, MLA_V), F32))
        sts = lax.fori_loop(0, qi, k_tile, (init,) * hp)
        row = lax.broadcasted_iota(jnp.int32, (tile, tile), 0)
        col = lax.broadcasted_iota(jnp.int32, (tile, tile), 1)
        visible = col // CHUNK <= row // CHUNK
        for h in range(hp):
            s = jnp.where(visible, scores(h, q0), NEG_BIG)
            _, l, acc = _softmax_step(s, v_ref[h, pl.ds(q0, tile), :], *sts[h])
            o_ref[pl.ds(q0, tile), pl.ds(h * MLA_V, MLA_V)] = acc / l
        return carry

    lax.fori_loop(0, seq // tile, q_tile, 0)


def _mla_prompt(q, k, v, *, batch, seq, tile, hp):
    nh = q.shape[0]
    assert seq % tile == 0 and tile % CHUNK == 0 and nh % hp == 0
    vmem = hp * (2 * seq * (2 * QK_PAD + MLA_V) * 2 + 2 * seq * MLA_V * 4 + 8 * tile * tile * 4)
    return pl.pallas_call(
        functools.partial(_mla_prompt_kernel, seq=seq, tile=tile, hp=hp),
        out_shape=jax.ShapeDtypeStruct((batch * seq, nh * MLA_V), F32),
        grid=(batch, nh // hp),
        in_specs=[
            pl.BlockSpec((hp, seq, QK_PAD), lambda b, h: (h, b, 0)),
            pl.BlockSpec((hp, seq, QK_PAD), lambda b, h: (h, b, 0)),
            pl.BlockSpec((hp, seq, MLA_V), lambda b, h: (h, b, 0)),
        ],
        out_specs=pl.BlockSpec((seq, hp * MLA_V), lambda b, h: (b, h)),
        compiler_params=_cparams(("parallel", "parallel"), vmem),
        name="mla_prompt",
    )(q, k, v)


def _mla_sample_kernel(q_ref, wuk_ref, wuv_ref, ckv_ref, kpe_ref, ckvn_ref, kpen_ref, o_ref,
                       qlat_s, qpe_s, m_s, l_s, acc_s, *, nkt, nh, tq):
    kt = pl.program_id(1)
    rows = nh * tq

    @pl.when(kt == 0)
    def _():
        for h in range(nh):
            qh = q_ref[h]
            qlat_s[pl.ds(h * tq, tq), :] = _dot_nt(qh[:, :MLA_NOPE], wuk_ref[h]).astype(BF16)
            qpe_s[pl.ds(h * tq, tq), :] = qh[:, MLA_NOPE:MLA_NOPE + KPE_PAD]
        m_s[...] = jnp.full((rows, 1), NEG_BIG, F32)
        l_s[...] = jnp.zeros((rows, 1), F32)
        acc_s[...] = jnp.zeros((rows, KV_LORA), F32)

    def update(ckv, rope_scores):
        s = _dot_nt(qlat_s[...], ckv) + rope_scores
        m, l, acc = _softmax_step(s, ckv, m_s[...], l_s[...], acc_s[...])
        m_s[...] = m
        l_s[...] = l
        acc_s[...] = acc

    qpe = qpe_s[:, pl.ds(0, MLA_ROPE)]
    update(ckv_ref[0, 0].astype(BF16), _dot(qpe, kpe_ref[0, 0].astype(BF16)))

    @pl.when(kt == nkt - 1)
    def _():
        update(ckvn_ref[...].astype(BF16), _dot_nt(qpe, kpen_ref[:, pl.ds(0, MLA_ROPE)].astype(BF16)))
        o_lat = (acc_s[...] / l_s[...]).astype(BF16)
        for h in range(nh):
            o_ref[:, pl.ds(h * MLA_V, MLA_V)] = _dot(o_lat[h * tq:(h + 1) * tq], wuv_ref[h])


def _mla_sample(q, wuk, wuv, cache_ckv, cache_kpe_t, ckv_new, kpe_new, *, layer, row0, tk):
    nh = q.shape[0]
    _, batch, past, _ = cache_ckv.shape
    tq = S_SEQ
    assert past % tk == 0 and row0 % tq == 0
    assert (PAST_LEN + S_SEQ - 1) // CHUNK == PAST_LEN // CHUNK
    nkt = past // tk
    rows = nh * tq
    vmem = (2 * nh * tq * QK_PAD * 2 + 4 * nh * KV_LORA * MLA_NOPE * 2
            + 2 * tk * (KV_LORA + KPE_PAD) * 4 + tk * (KV_LORA + KPE_PAD) * 2
            + 2 * tq * (KV_LORA + KPE_PAD) * 4 + 2 * tq * nh * MLA_V * 4
            + rows * (KV_LORA + KPE_PAD) * 2 + rows * (KV_LORA + 2 * V7X_LANES) * 4
            + 4 * rows * tk * 4)
    rb = row0 // tq
    return pl.pallas_call(
        functools.partial(_mla_sample_kernel, nkt=nkt, nh=nh, tq=tq),
        out_shape=jax.ShapeDtypeStruct((batch * tq, nh * MLA_V), F32),
        grid=(batch, nkt),
        in_specs=[
            pl.BlockSpec((nh, tq, QK_PAD), lambda b, t: (0, rb + b, 0)),
            pl.BlockSpec((nh, KV_LORA, MLA_NOPE), lambda b, t: (0, 0, 0)),
            pl.BlockSpec((nh, KV_LORA, MLA_V), lambda b, t: (0, 0, 0)),
            pl.BlockSpec((1, 1, tk, KV_LORA), lambda b, t: (layer, b, t, 0)),
            pl.BlockSpec((1, 1, MLA_ROPE, tk), lambda b, t: (layer, b, 0, t)),
            pl.BlockSpec((tq, KV_LORA), lambda b, t: (rb + b, 0)),
            pl.BlockSpec((tq, KPE_PAD), lambda b, t: (rb + b, 0)),
        ],
        out_specs=pl.BlockSpec((tq, nh * MLA_V), lambda b, t: (b, 0)),
        scratch_shapes=[pltpu.VMEM((rows, KV_LORA), BF16), pltpu.VMEM((rows, KPE_PAD), BF16),
                        pltpu.VMEM((rows, 1), F32), pltpu.VMEM((rows, 1), F32),
                        pltpu.VMEM((rows, KV_LORA), F32)],
        compiler_params=_cparams(("parallel", "arbitrary"), vmem),
        name="mla_sample",
    )(q, wuk, wuv, cache_ckv, cache_kpe_t, ckv_new, kpe_new)


def _softplus(z):
    return jnp.maximum(z, 0.0) + jnp.log(1.0 + jnp.exp(-jnp.abs(z)))


def _later_keys_matrix(n):
    row = lax.broadcasted_iota(jnp.int32, (n, n), 0)
    col = lax.broadcasted_iota(jnp.int32, (n, n), 1)
    return (row > col).astype(BF16)


def _later_sum(x, u):
    r = x.shape[0]
    hi = x.astype(BF16)
    lo = (x - hi.astype(F32)).astype(BF16)
    e = _dot(jnp.concatenate([hi, lo], axis=0), u)
    return e[:r] + e[r:]


def _sb_weights(z, u, decay, mask=None):
    sp = _softplus(z)
    d = sp if mask is None else jnp.where(mask, sp, 0.0)
    a = jnp.exp(z - sp - _later_sum(d, u) - decay)
    if mask is not None:
        a = jnp.where(mask, a, 0.0)
    return a.astype(BF16), decay + jnp.sum(d, axis=-1, keepdims=True)


def _sb_prompt_kernel(q_ref, k_ref, v_ref, o_ref, qb, kb, vb, *, seq, tq, tk, hp):
    heads = range(hp)
    hcol = lambda h: pl.ds(h * SB_DIM, SB_DIM)

    def cast(r):
        sl = pl.ds(r, ROW_CHUNK)
        qb[sl, :] = (q_ref[sl, :] * SB_SCALE).astype(BF16)
        kb[sl, :] = k_ref[sl, :].astype(BF16)
        vb[sl, :] = v_ref[sl, :].astype(BF16)
    _row_loop(seq, cast)

    u = _later_keys_matrix(tk)
    per_q = tq // tk

    def q_tile(qi, carry):
        q0 = pl.multiple_of(qi * tq, tq)
        qs = [qb[pl.ds(q0, tq), hcol(h)] for h in heads]

        def key_tile(k0, sts, mask=None):
            out = []
            for h in heads:
                decay, acc = sts[h]
                a, decay = _sb_weights(_dot_nt(qs[h], kb[pl.ds(k0, tk), hcol(h)]), u, decay, mask)
                out.append((decay, acc + _dot(a, vb[pl.ds(k0, tk), hcol(h)])))
            return tuple(out)

        sts = ((jnp.zeros((tq, 1), F32), jnp.zeros((tq, SB_DIM), F32)),) * hp
        for d in reversed(range(per_q)):
            row = lax.broadcasted_iota(jnp.int32, (tq, tk), 0)
            col = lax.broadcasted_iota(jnp.int32, (tq, tk), 1) + d * tk
            sts = key_tile(pl.multiple_of(q0 + d * tk, tk), sts, mask=col < row)

        def older(step, sts):
            for d in reversed(range(per_q)):
                sts = key_tile(pl.multiple_of(q0 - (step + 1) * tq + d * tk, tk), sts)
            return sts

        sts = lax.fori_loop(0, qi, older, sts)
        for h in heads:
            o_ref[pl.ds(q0, tq), hcol(h)] = sts[h][1]
        return carry

    lax.fori_loop(0, seq // tq, q_tile, 0)


def _sb_prompt(proj, *, batch, seq, tq, tk, hp):
    assert seq % tq == 0 and tq % tk == 0 and seq % ROW_CHUNK == 0 and SB_HEADS % hp == 0
    w = hp * SB_DIM
    vmem = hp * (2 * 3 * seq * SB_DIM * 4 + 2 * seq * SB_DIM * 4 + 3 * seq * SB_DIM * 2 + 12 * tq * tk * 4)
    col = lambda c0: pl.BlockSpec((seq, w), lambda b, h: (b, c0 // w + h))
    return pl.pallas_call(
        functools.partial(_sb_prompt_kernel, seq=seq, tq=tq, tk=tk, hp=hp),
        out_shape=jax.ShapeDtypeStruct((batch * seq, SB_WIDTH), F32),
        grid=(batch, SB_HEADS // hp),
        in_specs=[col(PROJ_SQ), col(PROJ_SK), col(PROJ_SV)],
        out_specs=pl.BlockSpec((seq, w), lambda b, h: (b, h)),
        scratch_shapes=[pltpu.VMEM((seq, w), BF16)] * 3,
        compiler_params=_cparams(("parallel", "parallel"), vmem),
        name="sb_prompt",
    )(proj, proj, proj)


def _sb_sample_kernel(q_ref, kn_ref, vn_ref, kc_ref, vc_ref, o_ref, decay_s, acc_s, *, nkt, sub, tq):
    step = pl.program_id(1)
    rows = SB_HEADS * tq
    head = lambda h: pl.ds(h * SB_DIM, SB_DIM)
    hrow = lambda h: pl.ds(h * tq, tq)

    def scores(k_of_head):
        return jnp.concatenate(
            [_dot_nt((q_ref[:, head(h)] * SB_SCALE).astype(BF16), k_of_head(h)) for h in range(SB_HEADS)], axis=0)

    def weighted(a, v_of_head):
        return jnp.concatenate(
            [_dot(a[h * tq:(h + 1) * tq], v_of_head(h)) for h in range(SB_HEADS)], axis=0)

    @pl.when(step == 0)
    def _():
        pad = jnp.zeros((V7X_LANES - tq, SB_DIM), BF16)
        z = scores(lambda h: jnp.concatenate([kn_ref[:, head(h)].astype(BF16), pad], axis=0))
        t = lax.broadcasted_iota(jnp.int32, (rows, V7X_LANES), 0) % tq
        s = lax.broadcasted_iota(jnp.int32, (rows, V7X_LANES), 1)
        a, decay = _sb_weights(z, _later_keys_matrix(V7X_LANES), jnp.zeros((rows, 1), F32), mask=s < t)
        acc_s[...] = weighted(a, lambda h: jnp.concatenate([vn_ref[:, head(h)].astype(BF16), pad], axis=0))
        decay_s[...] = decay

    @pl.when(step > 0)
    def _():
        u = _later_keys_matrix(sub)
        decay = decay_s[...]
        acc = acc_s[...]
        for c in reversed(range(kc_ref.shape[2] // (sub * SB_HEADS))):
            keys = lambda h: pl.ds(c * sub * SB_HEADS + h, sub, stride=SB_HEADS)
            a, decay = _sb_weights(scores(lambda h: kc_ref[0, 0, keys(h), :].astype(BF16)), u, decay)
            acc = acc + weighted(a, lambda h: vc_ref[0, 0, keys(h), :].astype(BF16))
        decay_s[...] = decay
        acc_s[...] = acc

    @pl.when(step == nkt)
    def _():
        for h in range(SB_HEADS):
            o_ref[:, head(h)] = acc_s[hrow(h), :]


def _sb_sample(proj, cache_k, cache_v, *, layer, tk, sub):
    _, batch, past_rows, _ = cache_k.shape
    past = past_rows // SB_HEADS
    tq = S_SEQ
    assert past % tk == 0 and tk % sub == 0
    nkt = past // tk
    rows = SB_HEADS * tq
    vmem = (2 * 3 * tq * SB_WIDTH * 4 + 2 * 2 * tk * SB_WIDTH * 4 + 2 * tq * SB_WIDTH * 4
            + rows * (SB_DIM + V7X_LANES) * 4 + 2 * sub * SB_WIDTH * 2 + 16 * rows * sub * 4 + sub * sub * 2)
    newest_first = lambda b, s: (layer, b, nkt - jnp.maximum(s, 1), 0)
    col = lambda c0: pl.BlockSpec((tq, SB_WIDTH), lambda b, s: (b, c0 // SB_WIDTH))
    return pl.pallas_call(
        functools.partial(_sb_sample_kernel, nkt=nkt, sub=sub, tq=tq),
        out_shape=jax.ShapeDtypeStruct((batch * tq, SB_WIDTH), F32),
        grid=(batch, nkt + 1),
        in_specs=[col(PROJ_SQ), col(PROJ_SK), col(PROJ_SV),
                  pl.BlockSpec((1, 1, tk * SB_HEADS, SB_DIM), newest_first),
                  pl.BlockSpec((1, 1, tk * SB_HEADS, SB_DIM), newest_first)],
        out_specs=pl.BlockSpec((tq, SB_WIDTH), lambda b, s: (b, 0)),
        scratch_shapes=[pltpu.VMEM((rows, 1), F32), pltpu.VMEM((rows, SB_DIM), F32)],
        compiler_params=_cparams(("parallel", "arbitrary"), vmem),
        name="sb_sample",
    )(proj, proj, proj, cache_k, cache_v)


def _sb_rows_kernel(*refs, tt, layers):
    srcs, (ko_ref, vo_ref) = refs[:2 * layers], refs[2 * layers:]
    for l in range(layers):
        @pl.when(pl.program_id(0) == l)
        def _():
            for src, dst in ((srcs[2 * l], ko_ref), (srcs[2 * l + 1], vo_ref)):
                for h in range(SB_HEADS):
                    dst[0, pl.ds(h, tt, stride=SB_HEADS), :] = src[:, pl.ds(h * SB_DIM, SB_DIM)]


def _sb_rows(projs, *, tt):
    layers = len(projs)
    m = projs[0].shape[0]
    assert m % tt == 0
    nt = m // tt
    vmem = (2 * layers + 2) * 2 * tt * SB_WIDTH * 4

    def col(l, c0):
        return pl.BlockSpec((tt, SB_WIDTH), lambda g, i: (jnp.clip(i + (g - l) * nt, 0, nt - 1), c0 // SB_WIDTH))

    out = jax.ShapeDtypeStruct((layers, m * SB_HEADS, SB_DIM), F32)
    return pl.pallas_call(
        functools.partial(_sb_rows_kernel, tt=tt, layers=layers),
        out_shape=(out, out),
        grid=(layers, nt),
        in_specs=[col(l, c0) for l in range(layers) for c0 in (PROJ_SK, PROJ_SV)],
        out_specs=(pl.BlockSpec((1, tt * SB_HEADS, SB_DIM), lambda g, i: (g, i, 0)),) * 2,
        compiler_params=_cparams(("arbitrary", "arbitrary"), vmem),
        name="sb_rows",
    )(*[p for p in projs for _ in range(2)])


def _pool_kernel(u_ref, hist_ref, w_ref, scale_ref, o_ref, carry_s, *, tt, pos0):
    ti = pl.program_id(1)
    halo = carry_s.shape[0]

    @pl.when(ti == 0)
    def _():
        carry_s[...] = hist_ref[0]

    pos = pos0 + ti * tt + lax.broadcasted_iota(jnp.int32, (tt, 1), 0)
    for g, w in enumerate(POOL_WINDOWS):
        cols = pl.ds(g * POOL_GROUP, POOL_GROUP)
        tok = u_ref[:, cols]
        s = jnp.concatenate([carry_s[:, cols], tok], axis=0)
        shift = 1
        while shift < w:
            s = s + pltpu.roll(s, shift, 0)
            shift *= 2
        cnt = jnp.minimum(pos + 1, w).astype(F32)
        pooled = s[halo:] / cnt - tok
        o_ref[:, cols] = (_dot(pooled.astype(BF16), w_ref[g]) * scale_ref[:, cols]).astype(BF16)
    carry_s[...] = u_ref[pl.ds(tt - halo, halo), :]


def _pool(proj, hist, w_pool, scale, *, batch, seq, tt, pos0):
    halo = hist.shape[1]
    assert seq % tt == 0 and tt >= halo and halo > max(POOL_WINDOWS) - 1
    nt = seq // tt
    vmem = (2 * tt * POOL_WIDTH * 4 + 2 * halo * POOL_WIDTH * 4 + 2 * POOL_WIDTH * POOL_GROUP * 2
            + 2 * tt * POOL_WIDTH * 2 + halo * POOL_WIDTH * 4 + 8 * (tt + halo) * POOL_GROUP * 4)
    return pl.pallas_call(
        functools.partial(_pool_kernel, tt=tt, pos0=pos0),
        out_shape=jax.ShapeDtypeStruct((batch * seq, POOL_WIDTH), BF16),
        grid=(batch, nt),
        in_specs=[
            pl.BlockSpec((tt, POOL_WIDTH), lambda b, t: (b * nt + t, PROJ_U // POOL_WIDTH)),
            pl.BlockSpec((1, halo, POOL_WIDTH), lambda b, t: (b, 0, 0)),
            pl.BlockSpec((len(POOL_WINDOWS), POOL_GROUP, POOL_GROUP), lambda b, t: (0, 0, 0)),
            pl.BlockSpec((1, POOL_WIDTH), lambda b, t: (0, 0)),
        ],
        out_specs=pl.BlockSpec((tt, POOL_WIDTH), lambda b, t: (b * nt + t, 0)),
        scratch_shapes=[pltpu.VMEM((halo, POOL_WIDTH), F32)],
        compiler_params=_cparams(("parallel", "arbitrary"), vmem),
        name="pool",
    )(proj, hist, w_pool, scale)


def _outproj_kernel(mla_ref, pool_ref, sb_ref, x_ref, gm_ref, gs_ref, w_ref, o_ref, mix_s, *, rows):
    wm = mla_ref.shape[1]
    wp = pool_ref.shape[1]

    @pl.when(pl.program_id(1) == 0)
    def _():
        def prep(r):
            sl = pl.ds(r, ROW_CHUNK)
            mix_s[sl, pl.ds(0, wm)] = _rms(mla_ref[sl, :], gm_ref[...]).astype(BF16)
            mix_s[sl, pl.ds(wm, wp)] = pool_ref[sl, :]
            mix_s[sl, pl.ds(wm + wp, sb_ref.shape[1])] = _rms(sb_ref[sl, :], gs_ref[...]).astype(BF16)
        _row_loop(rows, prep)

    o_ref[...] = x_ref[...] + _dot(mix_s[...], w_ref[...])


def _outproj(o_mla, o_pool, o_sb, x, g_mla, g_sb, w, *, tm, tn):
    m, d = x.shape
    wm, wp, ws = o_mla.shape[1], o_pool.shape[1], o_sb.shape[1]
    k = wm + wp + ws
    assert m % tm == 0 and d % tn == 0 and w.shape == (k, d)
    vmem = (2 * tm * (wm + ws) * 4 + 2 * tm * wp * 2 + 2 * 2 * tm * tn * 4 + 2 * k * tn * 2
            + tm * k * 2 + 2 * tm * tn * 4)
    full = lambda n: pl.BlockSpec((tm, n), lambda i, j: (i, 0))
    return pl.pallas_call(
        functools.partial(_outproj_kernel, rows=tm),
        out_shape=jax.ShapeDtypeStruct((m, d), F32),
        grid=(m // tm, d // tn),
        in_specs=[full(wm), full(wp), full(ws),
                  pl.BlockSpec((tm, tn), lambda i, j: (i, j)),
                  pl.BlockSpec((1, wm), lambda i, j: (0, 0)),
                  pl.BlockSpec((1, ws), lambda i, j: (0, 0)),
                  pl.BlockSpec((k, tn), lambda i, j: (0, j))],
        out_specs=pl.BlockSpec((tm, tn), lambda i, j: (i, j)),
        scratch_shapes=[pltpu.VMEM((tm, k), BF16)],
        compiler_params=_cparams(("parallel", "arbitrary"), vmem),
        name="outproj",
    )(o_mla, o_pool, o_sb, x, g_mla, g_sb, w)


def _rope_tables(pos):
    half = MLA_ROPE // 2
    inv = 1.0 / (ROPE_THETA ** (jnp.arange(half, dtype=F32) / half))
    ang = pos.astype(F32)[:, None] * inv[None, :]
    cos, sin = jnp.cos(ang), jnp.sin(ang)
    z = lambda n: jnp.zeros((pos.shape[0], n), F32)
    return (jnp.concatenate([cos, cos, z(KPE_PAD - MLA_ROPE)], axis=1),
            jnp.concatenate([-sin, z(KPE_PAD - half)], axis=1),
            jnp.concatenate([z(half), sin, z(KPE_PAD - MLA_ROPE)], axis=1))


W_IN_PIECES = (
    (0, Q_LORA, PROJ_CQ),
    (Q_LORA, KV_LORA, PROJ_CKV),
    (Q_LORA + KV_LORA, MLA_ROPE, PROJ_KPE),
    (Q_LORA + KV_LORA + MLA_ROPE, POOL_WIDTH, PROJ_U),
    (Q_LORA + KV_LORA + MLA_ROPE + POOL_WIDTH, SB_WIDTH, PROJ_SQ),
    (Q_LORA + KV_LORA + MLA_ROPE + POOL_WIDTH + SB_WIDTH, SB_WIDTH, PROJ_SK),
    (Q_LORA + KV_LORA + MLA_ROPE + POOL_WIDTH + 2 * SB_WIDTH, SB_WIDTH, PROJ_SV),
)


def _w_in_kernel(w_ref, o_ref):
    cols = o_ref.shape[1]
    o_ref[pl.ds(PROJ_KPE, KPE_PAD), :] = jnp.zeros((KPE_PAD, cols), BF16)
    for src, width, dst in W_IN_PIECES:
        o_ref[pl.ds(dst, width), :] = w_ref[0, pl.ds(src, width), :].astype(BF16)


def _w_in_bf16(w_in_t, l, *, tc):
    _, n, d = w_in_t.shape
    assert d % tc == 0
    vmem = 2 * n * tc * 4 + 2 * PROJ_W * tc * 2 + PROJ_W * tc * 4
    return pl.pallas_call(
        _w_in_kernel,
        out_shape=jax.ShapeDtypeStruct((PROJ_W, d), BF16),
        grid=(d // tc,),
        in_specs=[pl.BlockSpec((1, n, tc), lambda i: (l, 0, i))],
        out_specs=pl.BlockSpec((PROJ_W, tc), lambda i: (0, i)),
        compiler_params=_cparams(("parallel",), vmem),
        name="w_in_bf16",
    )(w_in_t)


def _cast_kernel(w_ref, o_ref):
    o_ref[...] = w_ref[0].astype(BF16)


def _layer_bf16(w, l, *, tr):
    _, r, c = w.shape
    assert r % tr == 0
    return pl.pallas_call(
        _cast_kernel,
        out_shape=jax.ShapeDtypeStruct((r, c), BF16),
        grid=(r // tr,),
        in_specs=[pl.BlockSpec((1, tr, c), lambda i: (l, i, 0))],
        out_specs=pl.BlockSpec((tr, c), lambda i: (i, 0)),
        compiler_params=_cparams(("parallel",), 2 * tr * c * 6),
        name="layer_bf16",
    )(w)


def _layer_weights(l, w_in, w_uq, w_uk, w_uv, w_pool, w_out):
    wuq = w_uq[l].reshape(Q_LORA, MLA_HEADS, MLA_NOPE + MLA_ROPE)
    wuq = jnp.pad(wuq, ((0, 0), (0, 0), (0, QK_PAD - MLA_NOPE - MLA_ROPE)))
    wuk = w_uk[l].transpose(1, 0, 2).astype(BF16)
    wuv = w_uv[l].transpose(1, 0, 2).astype(BF16)
    return dict(
        w_in=_w_in_bf16(jnp.swapaxes(w_in, 1, 2), l, tc=256),
        wuq=wuq.transpose(1, 0, 2).astype(BF16),
        wuk=wuk,
        wuv=wuv,
        w_pool=w_pool[l].astype(BF16),
        w_out=_layer_bf16(w_out, l, tr=512),
    )


def kernel(x_prompt, x_sample, cache_ckv, cache_kpe, state_pool, cache_sb_k, cache_sb_v, g_ffn1, w1_gate, w1_up, w1_down, g_mix, w_in, g_qnorm, w_uq, g_kvnorm, w_uk, w_uv, w_pool, pool_scale, g_mla_out, g_sb_out, w_out, g_ffn2, w2_gate, w2_up, w2_down, g_final):
    mp, ms = P_BATCH * P_SEQ, S_BATCH * S_SEQ
    tm_p, tf = 512, 256
    row = lambda g: g.reshape(1, -1)
    gfin = row(g_final)

    rope_p = _rope_tables(jnp.tile(jnp.arange(P_SEQ), P_BATCH))
    rope_s = _rope_tables(PAST_LEN + jnp.tile(jnp.arange(S_SEQ), S_BATCH))
    hist_p = jnp.zeros((P_BATCH, POOL_HIST + 1, POOL_WIDTH), F32)
    cache_kpe_t = jnp.swapaxes(cache_kpe, 2, 3)
    cache_k = cache_sb_k.reshape(DEPTH, S_BATCH, PAST_LEN * SB_HEADS, SB_DIM)
    cache_v = cache_sb_v.reshape(DEPTH, S_BATCH, PAST_LEN * SB_HEADS, SB_DIM)

    hp = x_prompt.reshape(mp, D_MODEL)
    hs = x_sample.reshape(ms, D_MODEL)
    outs = {k: [] for k in ("p_proj", "p_ckv", "p_kpe", "p_pool", "s_proj", "s_ckv", "s_kpe", "s_pool")}

    def record(tag, batch, seq, proj, ckv, kpe):
        outs[tag + "_proj"].append(proj)
        outs[tag + "_ckv"].append(ckv.reshape(batch, seq, KV_LORA))
        outs[tag + "_kpe"].append(kpe[:, :MLA_ROPE].reshape(batch, seq, MLA_ROPE))
        outs[tag + "_pool"].append(
            proj.reshape(batch, seq, PROJ_W)[:, seq - POOL_HIST:, PROJ_U:PROJ_U + POOL_WIDTH])

    d_ff = w1_gate.shape[-1]
    ffn_steps = (mp // tm_p) * (d_ff // tf)
    next_cast = (ffn_steps * NEXT_CAST_ROWS == d_ff
                 and (D_MODEL // NEXT_CAST_ROWS) * NEXT_CAST_COL_SPLIT <= ffn_steps)
    w1 = None
    for l in range(DEPTH):
        w = _layer_weights(l, w_in, w_uq, w_uk, w_uv, w_pool, w_out)
        last = l == DEPTH - 1
        gq, gkv = row(g_qnorm[l]), row(g_kvnorm[l])
        pscale = row(pool_scale[l])
        hist_s = jnp.pad(state_pool[l], ((0, 0), (1, 0), (0, 0)))

        def mixer_inputs(h, tm, rope):
            proj = _inproj(h, row(g_mix[l]), w["w_in"], tm=tm, tn=PROJ_W // 5)
            return proj, _mla_prep(proj, *rope, gq, gkv, w["wuq"], w["wuk"], w["wuv"], tm=tm, hb=8)

        def mix(h, tm, o_mla, o_pool, o_sb):
            return _outproj(o_mla, o_pool, o_sb, h, row(g_mla_out[l]), row(g_sb_out[l]), w["w_out"], tm=tm, tn=1024)

        if w1 is None:
            hs, *w1 = _ffn(hs, row(g_ffn1[l]), w1_gate, w1_up, w1_down, gfin, tm=ms, tf=tf // 2,
                           final_norm=False, cast_layer=l)
        else:
            hs = _ffn(hs, row(g_ffn1[l]), *w1, gfin, tm=ms, tf=tf, final_norm=False)
        proj, (ckv, kpe, q, _, _) = mixer_inputs(hs, ms, rope_s)
        o_mla = _mla_sample(q, w["wuk"], w["wuv"], cache_ckv, cache_kpe_t, ckv, kpe, layer=l, row0=0, tk=2048)
        o_sb = _sb_sample(proj, cache_k, cache_v, layer=l, tk=1024, sub=256)
        o_pool = _pool(proj, hist_s, w["w_pool"], pscale, batch=S_BATCH, seq=S_SEQ, tt=S_SEQ, pos0=PAST_LEN)
        hs = mix(hs, ms, o_mla, o_pool, o_sb)
        record("s", S_BATCH, S_SEQ, proj, ckv, kpe)

        if next_cast:
            hp, *w2 = _ffn_next_cast(hp, row(g_ffn1[l]), *w1, gfin, (w2_gate, w2_up, w2_down), l,
                                     tm=tm_p, tf=tf, final_norm=False)
            hs = _ffn(hs, row(g_ffn2[l]), *w2, gfin, tm=ms, tf=tf, final_norm=last)
        else:
            hs, *w2 = _ffn(hs, row(g_ffn2[l]), w2_gate, w2_up, w2_down, gfin, tm=ms, tf=tf // 2,
                           final_norm=last, cast_layer=l)
            hp = _ffn(hp, row(g_ffn1[l]), *w1, gfin, tm=tm_p, tf=tf, final_norm=False)
        proj, (ckv, kpe, q, k, v) = mixer_inputs(hp, tm_p, rope_p)
        o_mla = _mla_prompt(q, k, v, batch=P_BATCH, seq=P_SEQ, tile=512, hp=2)
        o_sb = _sb_prompt(proj, batch=P_BATCH, seq=P_SEQ, tq=512, tk=256, hp=1)
        o_pool = _pool(proj, hist_p, w["w_pool"], pscale, batch=P_BATCH, seq=P_SEQ, tt=512, pos0=0)
        hp = mix(hp, tm_p, o_mla, o_pool, o_sb)
        if next_cast and not last:
            hp, *w1 = _ffn_next_cast(hp, row(g_ffn2[l]), *w2, gfin, (w1_gate, w1_up, w1_down), l + 1,
                                     tm=tm_p, tf=tf, final_norm=last)
        else:
            hp = _ffn(hp, row(g_ffn2[l]), *w2, gfin, tm=tm_p, tf=tf, final_norm=last)
            w1 = None
        record("p", P_BATCH, P_SEQ, proj, ckv, kpe)

    st = lambda k: jnp.stack(outs[k])
    heads = lambda a, batch, seq: a.reshape(DEPTH, batch, seq, SB_HEADS, SB_DIM)
    p_sbk, p_sbv = _sb_rows(outs["p_proj"], tt=tm_p)
    s_sbk, s_sbv = _sb_rows(outs["s_proj"], tt=ms)
    return (hp.reshape(P_BATCH, P_SEQ, D_MODEL), hs.reshape(S_BATCH, S_SEQ, D_MODEL),
            st("p_ckv"), st("p_kpe"), st("p_pool"), heads(p_sbk, P_BATCH, P_SEQ), heads(p_sbv, P_BATCH, P_SEQ),
            st("s_ckv"), st("s_kpe"), st("s_pool"), heads(s_sbk, S_BATCH, S_SEQ), heads(s_sbv, S_BATCH, S_SEQ))
```

```python
import functools
import math

import jax
import jax.numpy as jnp
from jax import lax
from jax.experimental import pallas as pl
from jax.experimental.pallas import tpu as pltpu

F32 = jnp.float32
BF16 = jnp.bfloat16

D_MODEL = 4096
D_FF = 11008
DEPTH = 2
P_BATCH, P_SEQ = 4, 2048
S_BATCH, S_SEQ, PAST_LEN = 16, 16, 4096
CHUNK = 64
MLA_HEADS, MLA_NOPE, MLA_ROPE, MLA_V = 16, 128, 64, 128
Q_LORA, KV_LORA = 1024, 512
ROPE_THETA = 10000.0
MLA_SCALE = 1.0 / math.sqrt(MLA_NOPE + MLA_ROPE)
POOL_WINDOWS = (2, 4, 8, 16)
POOL_GROUP = 256
POOL_WIDTH = 1024
POOL_HIST = 15
SB_HEADS, SB_DIM = 8, 128
SB_WIDTH = 1024
SB_SCALE = 1.0 / math.sqrt(SB_DIM)
EPS = 1e-6

V7X_LANES = 128
V7X_VMEM_LIMIT_BYTES = 60000 * 1024
VALUE_SPILL_BYTES = 8 * 1024 * 1024

PROJ_CQ = 0
PROJ_U = 1024
PROJ_SQ = 2048
PROJ_SK = 3072
PROJ_SV = 4096
PROJ_CKV = 5120
PROJ_KPE = 5632
PROJ_W = 5760
KPE_PAD = 128
QK_PAD = 256

ROW_CHUNK = 64
NEG_BIG = -1e30


def _cparams(semantics, vmem_bytes):
    assert vmem_bytes <= V7X_VMEM_LIMIT_BYTES, vmem_bytes
    limit = min(V7X_VMEM_LIMIT_BYTES, vmem_bytes + VALUE_SPILL_BYTES)
    return pltpu.CompilerParams(dimension_semantics=semantics, vmem_limit_bytes=int(limit))


def _rms(x, g):
    ms = jnp.mean(x * x, axis=-1, keepdims=True)
    return (x * lax.rsqrt(ms + EPS)) * g


def _dot(a, b):
    return jnp.dot(a, b, preferred_element_type=F32)


def _dot_nt(a, b):
    return lax.dot_general(a, b, (((1,), (1,)), ((), ())), preferred_element_type=F32)


def _row_loop(rows, fn):
    def body(c, carry):
        fn(pl.multiple_of(c * ROW_CHUNK, ROW_CHUNK))
        return carry
    lax.fori_loop(0, rows // ROW_CHUNK, body, 0)


def _ffn_step(x_ref, g_ref, wg_ref, wu_ref, wd_ref, gf_ref, o_ref, xn_ref, *, nj, rows, final_norm, side_work=None):
    j = pl.program_id(1)

    @pl.when(j == 0)
    def _():
        def prep(r):
            x = x_ref[pl.ds(r, ROW_CHUNK), :]
            xn_ref[pl.ds(r, ROW_CHUNK), :] = _rms(x, g_ref[...]).astype(BF16)
            o_ref[pl.ds(r, ROW_CHUNK), :] = jnp.zeros((ROW_CHUNK, x.shape[1]), F32)
        _row_loop(rows, prep)

    if side_work is not None:
        side_work()
    xn = xn_ref[...]
    gate = _dot(xn, wg_ref[...])
    up = _dot(xn, wu_ref[...])
    h = (gate * (1.0 / (1.0 + jnp.exp(-gate))) * up).astype(BF16)
    o_ref[...] += _dot(h, wd_ref[...])

    @pl.when(j == nj - 1)
    def _():
        def fin(r):
            y = x_ref[pl.ds(r, ROW_CHUNK), :] + 0.5 * o_ref[pl.ds(r, ROW_CHUNK), :]
            if final_norm:
                y = _rms(y, gf_ref[...])
            o_ref[pl.ds(r, ROW_CHUNK), :] = y
        _row_loop(rows, fin)


def _ffn_kernel(x_ref, g_ref, wg_ref, wu_ref, wd_ref, gf_ref, o_ref, xn_ref, **kw):
    _ffn_step(x_ref, g_ref, wg_ref, wu_ref, wd_ref, gf_ref, o_ref, xn_ref, **kw)


def _ffn_cast_kernel(x_ref, g_ref, wg_ref, wu_ref, wd_ref, gf_ref, o_ref, wg_o, wu_o, wd_o, xn_ref, **kw):
    for src, dst in ((wg_ref, wg_o), (wu_ref, wu_o), (wd_ref, wd_o)):
        dst[...] = src[0].astype(BF16)
    _ffn_step(x_ref, g_ref, wg_o, wu_o, wd_o, gf_ref, o_ref, xn_ref, **kw)


def _ffn_next_cast_kernel(x_ref, g_ref, wg_ref, wu_ref, wd_ref, gf_ref, ng_ref, nu_ref, nd_ref,
                          o_ref, og_ref, ou_ref, od_ref, xn_ref, **kw):
    def cast_slice():
        for src, dst in ((ng_ref, og_ref), (nu_ref, ou_ref), (nd_ref, od_ref)):
            dst[...] = src[0].astype(BF16)
    _ffn_step(x_ref, g_ref, wg_ref, wu_ref, wd_ref, gf_ref, o_ref, xn_ref, side_work=cast_slice, **kw)


def _ffn_vmem(tm, d, tf, weight_bytes):
    return (2 * tm * d * 4
            + tm * d * 2
            + 2 * tm * d * 4
            + 2 * 3 * d * tf * weight_bytes
            + 4 * tm * tf * 4)


def _ffn(x, g, wg, wu, wd, g_final, *, tm, tf, final_norm, cast_layer=None):
    m, d = x.shape
    cast = cast_layer is not None
    f = wg.shape[-1]
    assert m % tm == 0 and f % tf == 0 and tm % ROW_CHUNK == 0 and (not cast or m == tm)
    nj = f // tf
    vmem = _ffn_vmem(tm, d, tf, 6 if cast else 2)
    w_specs = [pl.BlockSpec((d, tf), lambda i, j: (0, j)),
               pl.BlockSpec((d, tf), lambda i, j: (0, j)),
               pl.BlockSpec((tf, d), lambda i, j: (j, 0))]
    w_in_specs = w_specs
    if cast:
        w_in_specs = [pl.BlockSpec((1, d, tf), lambda i, j: (cast_layer, 0, j)),
                      pl.BlockSpec((1, d, tf), lambda i, j: (cast_layer, 0, j)),
                      pl.BlockSpec((1, tf, d), lambda i, j: (cast_layer, j, 0))]
    x_out = jax.ShapeDtypeStruct((m, d), F32)
    x_spec = pl.BlockSpec((tm, d), lambda i, j: (i, 0))
    return pl.pallas_call(
        functools.partial(_ffn_cast_kernel if cast else _ffn_kernel, nj=nj, rows=tm, final_norm=final_norm),
        out_shape=(x_out,) + tuple(jax.ShapeDtypeStruct(w.shape[1:], BF16) for w in (wg, wu, wd)) if cast else x_out,
        grid=(m // tm, nj),
        in_specs=[
            pl.BlockSpec((tm, d), lambda i, j: (i, 0)),
            pl.BlockSpec((1, d), lambda i, j: (0, 0)),
            *w_in_specs,
            pl.BlockSpec((1, d), lambda i, j: (0, 0)),
        ],
        out_specs=(x_spec, *w_specs) if cast else x_spec,
        scratch_shapes=[pltpu.VMEM((tm, d), BF16)],
        compiler_params=_cparams(("parallel", "arbitrary"), vmem),
        name="ffn_cast" if cast else "ffn",
    )(x, g, wg, wu, wd, g_final)


NEXT_CAST_ROWS = 16
NEXT_CAST_COL_SPLIT = 2


def _ffn_next_cast(x, g, wg, wu, wd, g_final, next_w, next_layer, *, tm, tf, final_norm):
    m, d = x.shape
    f = wg.shape[-1]
    nj = f // tf
    steps = (m // tm) * nj
    r, split = NEXT_CAST_ROWS, NEXT_CAST_COL_SPLIT
    cw = f // split
    up_blocks = (d // r) * split
    assert m % tm == 0 and f % tf == 0 and tm % ROW_CHUNK == 0
    assert steps * r == f and up_blocks <= steps and cw % V7X_LANES == 0 and d % r == 0
    step = lambda i, j: i * nj + j
    up_idx = lambda i, j: jnp.minimum(step(i, j), up_blocks - 1)
    vmem = _ffn_vmem(tm, d, tf, 2) + 2 * (2 * r * cw + r * d) * 6
    w_specs = [pl.BlockSpec((d, tf), lambda i, j: (0, j)),
               pl.BlockSpec((d, tf), lambda i, j: (0, j)),
               pl.BlockSpec((tf, d), lambda i, j: (j, 0))]
    up_in = pl.BlockSpec((1, r, cw), lambda i, j: (next_layer, up_idx(i, j) // split, up_idx(i, j) % split))
    up_out = pl.BlockSpec((r, cw), lambda i, j: (up_idx(i, j) // split, up_idx(i, j) % split))
    x_spec = pl.BlockSpec((tm, d), lambda i, j: (i, 0))
    return pl.pallas_call(
        functools.partial(_ffn_next_cast_kernel, nj=nj, rows=tm, final_norm=final_norm),
        out_shape=(jax.ShapeDtypeStruct((m, d), F32),) + tuple(
            jax.ShapeDtypeStruct(w.shape[1:], BF16) for w in next_w),
        grid=(m // tm, nj),
        in_specs=[
            x_spec,
            pl.BlockSpec((1, d), lambda i, j: (0, 0)),
            *w_specs,
            pl.BlockSpec((1, d), lambda i, j: (0, 0)),
            up_in, up_in,
            pl.BlockSpec((1, r, d), lambda i, j: (next_layer, step(i, j), 0)),
        ],
        out_specs=(x_spec, up_out, up_out, pl.BlockSpec((r, d), lambda i, j: (step(i, j), 0))),
        scratch_shapes=[pltpu.VMEM((tm, d), BF16)],
        compiler_params=_cparams(("arbitrary", "arbitrary"), vmem),
        name="ffn_next_cast",
    )(x, g, wg, wu, wd, g_final, *next_w)


def _inproj_kernel(x_ref, g_ref, w_ref, o_ref, xn_ref, *, rows):
    @pl.when(pl.program_id(1) == 0)
    def _():
        def prep(r):
            xn_ref[pl.ds(r, ROW_CHUNK), :] = _rms(x_ref[pl.ds(r, ROW_CHUNK), :], g_ref[...]).astype(BF16)
        _row_loop(rows, prep)

    o_ref[...] = _dot_nt(xn_ref[...], w_ref[...])


def _inproj(x, g, w_t, *, tm, tn):
    m, d = x.shape
    n = w_t.shape[0]
    assert m % tm == 0 and n % tn == 0
    vmem = 2 * tm * d * 4 + tm * d * 2 + 2 * d * tn * 2 + 3 * tm * tn * 4
    return pl.pallas_call(
        functools.partial(_inproj_kernel, rows=tm),
        out_shape=jax.ShapeDtypeStruct((m, n), F32),
        grid=(m // tm, n // tn),
        in_specs=[
            pl.BlockSpec((tm, d), lambda i, j: (i, 0)),
            pl.BlockSpec((1, d), lambda i, j: (0, 0)),
            pl.BlockSpec((tn, d), lambda i, j: (j, 0)),
        ],
        out_specs=pl.BlockSpec((tm, tn), lambda i, j: (i, j)),
        scratch_shapes=[pltpu.VMEM((tm, d), BF16)],
        compiler_params=_cparams(("parallel", "arbitrary"), vmem),
        name="inproj",
    )(x, g, w_t)


def _rope_tile(t, c, sa, sb):
    return t * c + pltpu.roll(t, KPE_PAD - MLA_ROPE // 2, 1) * sa + pltpu.roll(t, MLA_ROPE // 2, 1) * sb


def _mla_prep_kernel(cq_ref, ckv_ref, kpe_ref, c_ref, sa_ref, sb_ref, gq_ref, gkv_ref,
                     wuq_ref, wuk_ref, wuv_ref,
                     ckv_o, kpe_o, q_o, k_o, v_o, cqn_s, ckv_s, kpe_s, *, rows, hb):
    @pl.when(pl.program_id(1) == 0)
    def _():
        def prep(r):
            sl = pl.ds(r, ROW_CHUNK)
            cqn_s[sl, :] = _rms(cq_ref[sl, :], gq_ref[...]).astype(BF16)
            ckv = _rms(ckv_ref[sl, :], gkv_ref[...])
            ckv_o[sl, :] = ckv
            ckv_s[sl, :] = ckv.astype(BF16)
            kpe = _rope_tile(kpe_ref[sl, :], c_ref[sl, :], sa_ref[sl, :], sb_ref[sl, :])
            kpe_o[sl, :] = kpe
            kpe_s[sl, :] = kpe.astype(BF16)
        _row_loop(rows, prep)

    for hh in range(hb):
        qf = _dot(cqn_s[...], wuq_ref[hh]) * MLA_SCALE
        q_o[hh, :, pl.ds(0, MLA_NOPE)] = qf[:, :MLA_NOPE].astype(BF16)
        q_o[hh, :, pl.ds(MLA_NOPE, KPE_PAD)] = _rope_tile(
            qf[:, MLA_NOPE:], c_ref[...], sa_ref[...], sb_ref[...]).astype(BF16)
        k_o[hh, :, pl.ds(0, MLA_NOPE)] = _dot(ckv_s[...], wuk_ref[hh]).astype(BF16)
        k_o[hh, :, pl.ds(MLA_NOPE, KPE_PAD)] = kpe_s[...]
        v_o[hh] = _dot(ckv_s[...], wuv_ref[hh]).astype(BF16)


def _mla_prep(proj, rope_c, rope_sa, rope_sb, gq, gkv, wuq, wuk, wuv, *, tm, hb):
    m = proj.shape[0]
    nh = wuq.shape[0]
    assert m % tm == 0 and nh % hb == 0
    vmem = (2 * tm * (Q_LORA + KV_LORA + 4 * KPE_PAD) * 4
            + 2 * hb * (Q_LORA * QK_PAD + 2 * KV_LORA * MLA_NOPE) * 2
            + 2 * tm * (KV_LORA + KPE_PAD) * 4
            + 2 * hb * tm * (2 * QK_PAD + MLA_V) * 2
            + tm * (Q_LORA + KV_LORA + KPE_PAD) * 2
            + 6 * tm * QK_PAD * 4)
    row_blk = lambda w, c: pl.BlockSpec((tm, w), lambda i, h: (i, c))
    head_w = lambda k, n: pl.BlockSpec((hb, k, n), lambda i, h: (h, 0, 0))
    head_o = lambda n: pl.BlockSpec((hb, tm, n), lambda i, h: (h, i, 0))
    return pl.pallas_call(
        functools.partial(_mla_prep_kernel, rows=tm, hb=hb),
        out_shape=(
            jax.ShapeDtypeStruct((m, KV_LORA), F32),
            jax.ShapeDtypeStruct((m, KPE_PAD), F32),
            jax.ShapeDtypeStruct((nh, m, QK_PAD), BF16),
            jax.ShapeDtypeStruct((nh, m, QK_PAD), BF16),
            jax.ShapeDtypeStruct((nh, m, MLA_V), BF16),
        ),
        grid=(m // tm, nh // hb),
        in_specs=[
            row_blk(Q_LORA, PROJ_CQ // Q_LORA),
            row_blk(KV_LORA, PROJ_CKV // KV_LORA),
            row_blk(KPE_PAD, PROJ_KPE // KPE_PAD),
            row_blk(KPE_PAD, 0), row_blk(KPE_PAD, 0), row_blk(KPE_PAD, 0),
            pl.BlockSpec((1, Q_LORA), lambda i, h: (0, 0)),
            pl.BlockSpec((1, KV_LORA), lambda i, h: (0, 0)),
            head_w(Q_LORA, QK_PAD), head_w(KV_LORA, MLA_NOPE), head_w(KV_LORA, MLA_V),
        ],
        out_specs=(row_blk(KV_LORA, 0), row_blk(KPE_PAD, 0), head_o(QK_PAD), head_o(QK_PAD), head_o(MLA_V)),
        scratch_shapes=[pltpu.VMEM((tm, Q_LORA), BF16), pltpu.VMEM((tm, KV_LORA), BF16),
                        pltpu.VMEM((tm, KPE_PAD), BF16)],
        compiler_params=_cparams(("parallel", "arbitrary"), vmem),
        name="mla_prep",
    )(proj, proj, proj, rope_c, rope_sa, rope_sb, gq, gkv, wuq, wuk, wuv)


def _softmax_step(s, v, m, l, acc):
    m_new = jnp.maximum(m, jnp.max(s, axis=-1, keepdims=True))
    p = jnp.exp(s - m_new)
    alpha = jnp.exp(m - m_new)
    l = alpha * l + jnp.sum(p, axis=-1, keepdims=True)
    acc = alpha * acc + _dot(p.astype(BF16), v)
    return m_new, l, acc


def _mla_prompt_kernel(q_ref, k_ref, v_ref, o_ref, *, seq, tile, hp):
    def q_tile(qi, carry):
        q0 = pl.multiple_of(qi * tile, tile)
        qs = [q_ref[h, pl.ds(q0, tile), :] for h in range(hp)]

        def scores(h, k0):
            return _dot_nt(qs[h], k_ref[h, pl.ds(k0, tile), :])

        def k_tile(ki, sts):
            k0 = pl.multiple_of(ki * tile, tile)
            return tuple(_softmax_step(scores(h, k0), v_ref[h, pl.ds(k0, tile), :], *sts[h]) for h in range(hp))

        init = (jnp.full((tile, 1), NEG_BIG, F32), jnp.zeros((tile, 1), F32), jnp.zeros((tile, MLA_V), F32))
        sts = lax.fori_loop(0, qi, k_tile, (init,) * hp)
        row = lax.broadcasted_iota(jnp.int32, (tile, tile), 0)
        col = lax.broadcasted_iota(jnp.int32, (tile, tile), 1)
        visible = col // CHUNK <= row // CHUNK
        for h in range(hp):
            s = jnp.where(visible, scores(h, q0), NEG_BIG)
            _, l, acc = _softmax_step(s, v_ref[h, pl.ds(q0, tile), :], *sts[h])
            o_ref[pl.ds(q0, tile), pl.ds(h * MLA_V, MLA_V)] = acc / l
        return carry

    lax.fori_loop(0, seq // tile, q_tile, 0)


def _mla_prompt(q, k, v, *, batch, seq, tile, hp):
    nh = q.shape[0]
    assert seq % tile == 0 and tile % CHUNK == 0 and nh % hp == 0
    vmem = hp * (2 * seq * (2 * QK_PAD + MLA_V) * 2 + 2 * seq * MLA_V * 4 + 4 * tile * tile * 4)
    return pl.pallas_call(
        functools.partial(_mla_prompt_kernel, seq=seq, tile=tile, hp=hp),
        out_shape=jax.ShapeDtypeStruct((batch * seq, nh * MLA_V), F32),
        grid=(batch, nh // hp),
        in_specs=[
            pl.BlockSpec((hp, seq, QK_PAD), lambda b, h: (h, b, 0)),
            pl.BlockSpec((hp, seq, QK_PAD), lambda b, h: (h, b, 0)),
            pl.BlockSpec((hp, seq, MLA_V), lambda b, h: (h, b, 0)),
        ],
        out_specs=pl.BlockSpec((seq, hp * MLA_V), lambda b, h: (b, h)),
        compiler_params=_cparams(("parallel", "parallel"), vmem),
        name="mla_prompt",
    )(q, k, v)


def _mla_sample_kernel(q_ref, wuk_ref, wuv_ref, ckv_ref, kpe_ref, ckvn_ref, kpen_ref, o_ref,
                       qlat_s, qpe_s, m_s, l_s, acc_s, *, nkt, nh, tq):
    kt = pl.program_id(1)
    rows = nh * tq

    @pl.when(kt == 0)
    def _():
        for h in range(nh):
            qh = q_ref[h]
            qlat_s[pl.ds(h * tq, tq), :] = _dot_nt(qh[:, :MLA_NOPE], wuk_ref[h]).astype(BF16)
            qpe_s[pl.ds(h * tq, tq), :] = qh[:, MLA_NOPE:MLA_NOPE + KPE_PAD]
        m_s[...] = jnp.full((rows, 1), NEG_BIG, F32)
        l_s[...] = jnp.zeros((rows, 1), F32)
        acc_s[...] = jnp.zeros((rows, KV_LORA), F32)

    def update(ckv, rope_scores):
        s = _dot_nt(qlat_s[...], ckv) + rope_scores
        m, l, acc = _softmax_step(s, ckv, m_s[...], l_s[...], acc_s[...])
        m_s[...] = m
        l_s[...] = l
        acc_s[...] = acc

    qpe = qpe_s[:, pl.ds(0, MLA_ROPE)]
    update(ckv_ref[0, 0].astype(BF16), _dot(qpe, kpe_ref[0, 0].astype(BF16)))

    @pl.when(kt == nkt - 1)
    def _():
        update(ckvn_ref[...].astype(BF16), _dot_nt(qpe, kpen_ref[:, pl.ds(0, MLA_ROPE)].astype(BF16)))
        o_lat = (acc_s[...] / l_s[...]).astype(BF16)
        for h in range(nh):
            o_ref[:, pl.ds(h * MLA_V, MLA_V)] = _dot(o_lat[h * tq:(h + 1) * tq], wuv_ref[h])


def _mla_sample(q, wuk, wuv, cache_ckv, cache_kpe_t, ckv_new, kpe_new, *, layer, row0, tk):
    nh = q.shape[0]
    _, batch, past, _ = cache_ckv.shape
    tq = S_SEQ
    assert past % tk == 0 and row0 % tq == 0
    assert (PAST_LEN + S_SEQ - 1) // CHUNK == PAST_LEN // CHUNK
    nkt = past // tk
    rows = nh * tq
    vmem = (2 * nh * tq * QK_PAD * 2 + 4 * nh * KV_LORA * MLA_NOPE * 2
            + 2 * tk * (KV_LORA + KPE_PAD) * 4 + tk * (KV_LORA + KPE_PAD) * 2
            + 2 * tq * (KV_LORA + KPE_PAD) * 4 + 2 * tq * nh * MLA_V * 4
            + rows * (KV_LORA + KPE_PAD) * 2 + rows * (KV_LORA + 2 * V7X_LANES) * 4
            + 4 * rows * tk * 4)
    rb = row0 // tq
    return pl.pallas_call(
        functools.partial(_mla_sample_kernel, nkt=nkt, nh=nh, tq=tq),
        out_shape=jax.ShapeDtypeStruct((batch * tq, nh * MLA_V), F32),
        grid=(batch, nkt),
        in_specs=[
            pl.BlockSpec((nh, tq, QK_PAD), lambda b, t: (0, rb + b, 0)),
            pl.BlockSpec((nh, KV_LORA, MLA_NOPE), lambda b, t: (0, 0, 0)),
            pl.BlockSpec((nh, KV_LORA, MLA_V), lambda b, t: (0, 0, 0)),
            pl.BlockSpec((1, 1, tk, KV_LORA), lambda b, t: (layer, b, t, 0)),
            pl.BlockSpec((1, 1, MLA_ROPE, tk), lambda b, t: (layer, b, 0, t)),
            pl.BlockSpec((tq, KV_LORA), lambda b, t: (rb + b, 0)),
            pl.BlockSpec((tq, KPE_PAD), lambda b, t: (rb + b, 0)),
        ],
        out_specs=pl.BlockSpec((tq, nh * MLA_V), lambda b, t: (b, 0)),
        scratch_shapes=[pltpu.VMEM((rows, KV_LORA), BF16), pltpu.VMEM((rows, KPE_PAD), BF16),
                        pltpu.VMEM((rows, 1), F32), pltpu.VMEM((rows, 1), F32),
                        pltpu.VMEM((rows, KV_LORA), F32)],
        compiler_params=_cparams(("parallel", "arbitrary"), vmem),
        name="mla_sample",
    )(q, wuk, wuv, cache_ckv, cache_kpe_t, ckv_new, kpe_new)


def _softplus(z):
    return jnp.maximum(z, 0.0) + jnp.log(1.0 + jnp.exp(-jnp.abs(z)))


def _later_keys_matrix(n):
    row = lax.broadcasted_iota(jnp.int32, (n, n), 0)
    col = lax.broadcasted_iota(jnp.int32, (n, n), 1)
    return (row > col).astype(BF16)


def _later_sum(x, u):
    r = x.shape[0]
    hi = x.astype(BF16)
    lo = (x - hi.astype(F32)).astype(BF16)
    e = _dot(jnp.concatenate([hi, lo], axis=0), u)
    return e[:r] + e[r:]


def _sb_weights(z, u, decay, mask=None):
    sp = _softplus(z)
    d = sp if mask is None else jnp.where(mask, sp, 0.0)
    a = jnp.exp(z - sp - _later_sum(d, u) - decay)
    if mask is not None:
        a = jnp.where(mask, a, 0.0)
    return a.astype(BF16), decay + jnp.sum(d, axis=-1, keepdims=True)


def _sb_prompt_kernel(q_ref, k_ref, v_ref, o_ref, qb, kb, vb, *, seq, tq, tk, hp):
    heads = range(hp)
    hcol = lambda h: pl.ds(h * SB_DIM, SB_DIM)

    def cast(r):
        sl = pl.ds(r, ROW_CHUNK)
        qb[sl, :] = (q_ref[sl, :] * SB_SCALE).astype(BF16)
        kb[sl, :] = k_ref[sl, :].astype(BF16)
        vb[sl, :] = v_ref[sl, :].astype(BF16)
    _row_loop(seq, cast)

    u = _later_keys_matrix(tk)
    per_q = tq // tk

    def q_tile(qi, carry):
        q0 = pl.multiple_of(qi * tq, tq)
        qs = [qb[pl.ds(q0, tq), hcol(h)] for h in heads]

        def key_tile(k0, sts, mask=None):
            out = []
            for h in heads:
                decay, acc = sts[h]
                a, decay = _sb_weights(_dot_nt(qs[h], kb[pl.ds(k0, tk), hcol(h)]), u, decay, mask)
                out.append((decay, acc + _dot(a, vb[pl.ds(k0, tk), hcol(h)])))
            return tuple(out)

        sts = ((jnp.zeros((tq, 1), F32), jnp.zeros((tq, SB_DIM), F32)),) * hp
        for d in reversed(range(per_q)):
            row = lax.broadcasted_iota(jnp.int32, (tq, tk), 0)
            col = lax.broadcasted_iota(jnp.int32, (tq, tk), 1) + d * tk
            sts = key_tile(pl.multiple_of(q0 + d * tk, tk), sts, mask=col < row)

        def older(step, sts):
            for d in reversed(range(per_q)):
                sts = key_tile(pl.multiple_of(q0 - (step + 1) * tq + d * tk, tk), sts)
            return sts

        sts = lax.fori_loop(0, qi, older, sts)
        for h in heads:
            o_ref[pl.ds(q0, tq), hcol(h)] = sts[h][1]
        return carry

    lax.fori_loop(0, seq // tq, q_tile, 0)


def _sb_prompt(proj, *, batch, seq, tq, tk, hp):
    assert seq % tq == 0 and tq % tk == 0 and seq % ROW_CHUNK == 0 and SB_HEADS % hp == 0
    w = hp * SB_DIM
    vmem = hp * (2 * 3 * seq * SB_DIM * 4 + 2 * seq * SB_DIM * 4 + 3 * seq * SB_DIM * 2 + 12 * tq * tk * 4)
    col = lambda c0: pl.BlockSpec((seq, w), lambda b, h: (b, c0 // w + h))
    return pl.pallas_call(
        functools.partial(_sb_prompt_kernel, seq=seq, tq=tq, tk=tk, hp=hp),
        out_shape=jax.ShapeDtypeStruct((batch * seq, SB_WIDTH), F32),
        grid=(batch, SB_HEADS // hp),
        in_specs=[col(PROJ_SQ), col(PROJ_SK), col(PROJ_SV)],
        out_specs=pl.BlockSpec((seq, w), lambda b, h: (b, h)),
        scratch_shapes=[pltpu.VMEM((seq, w), BF16)] * 3,
        compiler_params=_cparams(("parallel", "parallel"), vmem),
        name="sb_prompt",
    )(proj, proj, proj)


def _sb_sample_kernel(q_ref, kn_ref, vn_ref, kc_ref, vc_ref, o_ref, decay_s, acc_s, *, nkt, sub, tq):
    step = pl.program_id(1)
    rows = SB_HEADS * tq
    head = lambda h: pl.ds(h * SB_DIM, SB_DIM)
    hrow = lambda h: pl.ds(h * tq, tq)

    def scores(k_of_head):
        return jnp.concatenate(
            [_dot_nt((q_ref[:, head(h)] * SB_SCALE).astype(BF16), k_of_head(h)) for h in range(SB_HEADS)], axis=0)

    def weighted(a, v_of_head):
        return jnp.concatenate(
            [_dot(a[h * tq:(h + 1) * tq], v_of_head(h)) for h in range(SB_HEADS)], axis=0)

    @pl.when(step == 0)
    def _():
        pad = jnp.zeros((V7X_LANES - tq, SB_DIM), BF16)
        z = scores(lambda h: jnp.concatenate([kn_ref[:, head(h)].astype(BF16), pad], axis=0))
        t = lax.broadcasted_iota(jnp.int32, (rows, V7X_LANES), 0) % tq
        s = lax.broadcasted_iota(jnp.int32, (rows, V7X_LANES), 1)
        a, decay = _sb_weights(z, _later_keys_matrix(V7X_LANES), jnp.zeros((rows, 1), F32), mask=s < t)
        acc_s[...] = weighted(a, lambda h: jnp.concatenate([vn_ref[:, head(h)].astype(BF16), pad], axis=0))
        decay_s[...] = decay

    @pl.when(step > 0)
    def _():
        u = _later_keys_matrix(sub)
        decay = decay_s[...]
        acc = acc_s[...]
        for c in reversed(range(kc_ref.shape[2] // (sub * SB_HEADS))):
            keys = lambda h: pl.ds(c * sub * SB_HEADS + h, sub, stride=SB_HEADS)
            a, decay = _sb_weights(scores(lambda h: kc_ref[0, 0, keys(h), :].astype(BF16)), u, decay)
            acc = acc + weighted(a, lambda h: vc_ref[0, 0, keys(h), :].astype(BF16))
        decay_s[...] = decay
        acc_s[...] = acc

    @pl.when(step == nkt)
    def _():
        for h in range(SB_HEADS):
            o_ref[:, head(h)] = acc_s[hrow(h), :]


def _sb_sample(proj, cache_k, cache_v, *, layer, tk, sub):
    _, batch, past_rows, _ = cache_k.shape
    past = past_rows // SB_HEADS
    tq = S_SEQ
    assert past % tk == 0 and tk % sub == 0
    nkt = past // tk
    rows = SB_HEADS * tq
    vmem = (2 * 3 * tq * SB_WIDTH * 4 + 2 * 2 * tk * SB_WIDTH * 4 + 2 * tq * SB_WIDTH * 4
            + rows * (SB_DIM + V7X_LANES) * 4 + 2 * sub * SB_WIDTH * 2 + 16 * rows * sub * 4 + sub * sub * 2)
    newest_first = lambda b, s: (layer, b, nkt - jnp.maximum(s, 1), 0)
    col = lambda c0: pl.BlockSpec((tq, SB_WIDTH), lambda b, s: (b, c0 // SB_WIDTH))
    return pl.pallas_call(
        functools.partial(_sb_sample_kernel, nkt=nkt, sub=sub, tq=tq),
        out_shape=jax.ShapeDtypeStruct((batch * tq, SB_WIDTH), F32),
        grid=(batch, nkt + 1),
        in_specs=[col(PROJ_SQ), col(PROJ_SK), col(PROJ_SV),
                  pl.BlockSpec((1, 1, tk * SB_HEADS, SB_DIM), newest_first),
                  pl.BlockSpec((1, 1, tk * SB_HEADS, SB_DIM), newest_first)],
        out_specs=pl.BlockSpec((tq, SB_WIDTH), lambda b, s: (b, 0)),
        scratch_shapes=[pltpu.VMEM((rows, 1), F32), pltpu.VMEM((rows, SB_DIM), F32)],
        compiler_params=_cparams(("parallel", "arbitrary"), vmem),
        name="sb_sample",
    )(proj, proj, proj, cache_k, cache_v)


def _sb_rows_kernel(*refs, tt, layers):
    srcs, (ko_ref, vo_ref) = refs[:2 * layers], refs[2 * layers:]
    for l in range(layers):
        @pl.when(pl.program_id(0) == l)
        def _():
            for src, dst in ((srcs[2 * l], ko_ref), (srcs[2 * l + 1], vo_ref)):
                for h in range(SB_HEADS):
                    dst[0, pl.ds(h, tt, stride=SB_HEADS), :] = src[:, pl.ds(h * SB_DIM, SB_DIM)]


def _sb_rows(projs, *, tt):
    layers = len(projs)
    m = projs[0].shape[0]
    assert m % tt == 0
    nt = m // tt
    vmem = (2 * layers + 2) * 2 * tt * SB_WIDTH * 4

    def col(l, c0):
        return pl.BlockSpec((tt, SB_WIDTH), lambda g, i: (jnp.clip(i + (g - l) * nt, 0, nt - 1), c0 // SB_WIDTH))

    out = jax.ShapeDtypeStruct((layers, m * SB_HEADS, SB_DIM), F32)
    return pl.pallas_call(
        functools.partial(_sb_rows_kernel, tt=tt, layers=layers),
        out_shape=(out, out),
        grid=(layers, nt),
        in_specs=[col(l, c0) for l in range(layers) for c0 in (PROJ_SK, PROJ_SV)],
        out_specs=(pl.BlockSpec((1, tt * SB_HEADS, SB_DIM), lambda g, i: (g, i, 0)),) * 2,
        compiler_params=_cparams(("arbitrary", "arbitrary"), vmem),
        name="sb_rows",
    )(*[p for p in projs for _ in range(2)])


def _pool_kernel(u_ref, hist_ref, w_ref, scale_ref, o_ref, carry_s, *, tt, pos0):
    ti = pl.program_id(1)
    halo = carry_s.shape[0]

    @pl.when(ti == 0)
    def _():
        carry_s[...] = hist_ref[0]

    pos = pos0 + ti * tt + lax.broadcasted_iota(jnp.int32, (tt, 1), 0)
    for g, w in enumerate(POOL_WINDOWS):
        cols = pl.ds(g * POOL_GROUP, POOL_GROUP)
        tok = u_ref[:, cols]
        s = jnp.concatenate([carry_s[:, cols], tok], axis=0)
        shift = 1
        while shift < w:
            s = s + pltpu.roll(s, shift, 0)
            shift *= 2
        cnt = jnp.minimum(pos + 1, w).astype(F32)
        pooled = s[halo:] / cnt - tok
        o_ref[:, cols] = (_dot(pooled.astype(BF16), w_ref[g]) * scale_ref[:, cols]).astype(BF16)
    carry_s[...] = u_ref[pl.ds(tt - halo, halo), :]


def _pool(proj, hist, w_pool, scale, *, batch, seq, tt, pos0):
    halo = hist.shape[1]
    assert seq % tt == 0 and tt >= halo and halo > max(POOL_WINDOWS) - 1
    nt = seq // tt
    vmem = (2 * tt * POOL_WIDTH * 4 + 2 * halo * POOL_WIDTH * 4 + 2 * POOL_WIDTH * POOL_GROUP * 2
            + 2 * tt * POOL_WIDTH * 2 + halo * POOL_WIDTH * 4 + 8 * (tt + halo) * POOL_GROUP * 4)
    return pl.pallas_call(
        functools.partial(_pool_kernel, tt=tt, pos0=pos0),
        out_shape=jax.ShapeDtypeStruct((batch * seq, POOL_WIDTH), BF16),
        grid=(batch, nt),
        in_specs=[
            pl.BlockSpec((tt, POOL_WIDTH), lambda b, t: (b * nt + t, PROJ_U // POOL_WIDTH)),
            pl.BlockSpec((1, halo, POOL_WIDTH), lambda b, t: (b, 0, 0)),
            pl.BlockSpec((len(POOL_WINDOWS), POOL_GROUP, POOL_GROUP), lambda b, t: (0, 0, 0)),
            pl.BlockSpec((1, POOL_WIDTH), lambda b, t: (0, 0)),
        ],
        out_specs=pl.BlockSpec((tt, POOL_WIDTH), lambda b, t: (b * nt + t, 0)),
        scratch_shapes=[pltpu.VMEM((halo, POOL_WIDTH), F32)],
        compiler_params=_cparams(("parallel", "arbitrary"), vmem),
        name="pool",
    )(proj, hist, w_pool, scale)


def _outproj_kernel(mla_ref, pool_ref, sb_ref, x_ref, gm_ref, gs_ref, w_ref, o_ref, mix_s, *, rows):
    wm = mla_ref.shape[1]
    wp = pool_ref.shape[1]

    @pl.when(pl.program_id(1) == 0)
    def _():
        def prep(r):
            sl = pl.ds(r, ROW_CHUNK)
            mix_s[sl, pl.ds(0, wm)] = _rms(mla_ref[sl, :], gm_ref[...]).astype(BF16)
            mix_s[sl, pl.ds(wm, wp)] = pool_ref[sl, :]
            mix_s[sl, pl.ds(wm + wp, sb_ref.shape[1])] = _rms(sb_ref[sl, :], gs_ref[...]).astype(BF16)
        _row_loop(rows, prep)

    o_ref[...] = x_ref[...] + _dot(mix_s[...], w_ref[...])


def _outproj(o_mla, o_pool, o_sb, x, g_mla, g_sb, w, *, tm, tn):
    m, d = x.shape
    wm, wp, ws = o_mla.shape[1], o_pool.shape[1], o_sb.shape[1]
    k = wm + wp + ws
    assert m % tm == 0 and d % tn == 0 and w.shape == (k, d)
    vmem = (2 * tm * (wm + ws) * 4 + 2 * tm * wp * 2 + 2 * 2 * tm * tn * 4 + 2 * k * tn * 2
            + tm * k * 2 + 2 * tm * tn * 4)
    full = lambda n: pl.BlockSpec((tm, n), lambda i, j: (i, 0))
    return pl.pallas_call(
        functools.partial(_outproj_kernel, rows=tm),
        out_shape=jax.ShapeDtypeStruct((m, d), F32),
        grid=(m // tm, d // tn),
        in_specs=[full(wm), full(wp), full(ws),
                  pl.BlockSpec((tm, tn), lambda i, j: (i, j)),
                  pl.BlockSpec((1, wm), lambda i, j: (0, 0)),
                  pl.BlockSpec((1, ws), lambda i, j: (0, 0)),
                  pl.BlockSpec((k, tn), lambda i, j: (0, j))],
        out_specs=pl.BlockSpec((tm, tn), lambda i, j: (i, j)),
        scratch_shapes=[pltpu.VMEM((tm, k), BF16)],
        compiler_params=_cparams(("parallel", "arbitrary"), vmem),
        name="outproj",
    )(o_mla, o_pool, o_sb, x, g_mla, g_sb, w)


def _rope_tables(pos):
    half = MLA_ROPE // 2
    inv = 1.0 / (ROPE_THETA ** (jnp.arange(half, dtype=F32) / half))
    ang = pos.astype(F32)[:, None] * inv[None, :]
    cos, sin = jnp.cos(ang), jnp.sin(ang)
    z = lambda n: jnp.zeros((pos.shape[0], n), F32)
    return (jnp.concatenate([cos, cos, z(KPE_PAD - MLA_ROPE)], axis=1),
            jnp.concatenate([-sin, z(KPE_PAD - half)], axis=1),
            jnp.concatenate([z(half), sin, z(KPE_PAD - MLA_ROPE)], axis=1))


W_IN_PIECES = (
    (0, Q_LORA, PROJ_CQ),
    (Q_LORA, KV_LORA, PROJ_CKV),
    (Q_LORA + KV_LORA, MLA_ROPE, PROJ_KPE),
    (Q_LORA + KV_LORA + MLA_ROPE, POOL_WIDTH, PROJ_U),
    (Q_LORA + KV_LORA + MLA_ROPE + POOL_WIDTH, SB_WIDTH, PROJ_SQ),
    (Q_LORA + KV_LORA + MLA_ROPE + POOL_WIDTH + SB_WIDTH, SB_WIDTH, PROJ_SK),
    (Q_LORA + KV_LORA + MLA_ROPE + POOL_WIDTH + 2 * SB_WIDTH, SB_WIDTH, PROJ_SV),
)


def _w_in_kernel(w_ref, o_ref):
    cols = o_ref.shape[1]
    o_ref[pl.ds(PROJ_KPE, KPE_PAD), :] = jnp.zeros((KPE_PAD, cols), BF16)
    for src, width, dst in W_IN_PIECES:
        o_ref[pl.ds(dst, width), :] = w_ref[0, pl.ds(src, width), :].astype(BF16)


def _w_in_bf16(w_in_t, l, *, tc):
    _, n, d = w_in_t.shape
    assert d % tc == 0
    vmem = 2 * n * tc * 4 + 2 * PROJ_W * tc * 2 + PROJ_W * tc * 4
    return pl.pallas_call(
        _w_in_kernel,
        out_shape=jax.ShapeDtypeStruct((PROJ_W, d), BF16),
        grid=(d // tc,),
        in_specs=[pl.BlockSpec((1, n, tc), lambda i: (l, 0, i))],
        out_specs=pl.BlockSpec((PROJ_W, tc), lambda i: (0, i)),
        compiler_params=_cparams(("parallel",), vmem),
        name="w_in_bf16",
    )(w_in_t)


def _cast_kernel(w_ref, o_ref):
    o_ref[...] = w_ref[0].astype(BF16)


def _layer_bf16(w, l, *, tr):
    _, r, c = w.shape
    assert r % tr == 0
    return pl.pallas_call(
        _cast_kernel,
        out_shape=jax.ShapeDtypeStruct((r, c), BF16),
        grid=(r // tr,),
        in_specs=[pl.BlockSpec((1, tr, c), lambda i: (l, i, 0))],
        out_specs=pl.BlockSpec((tr, c), lambda i: (i, 0)),
        compiler_params=_cparams(("parallel",), 2 * tr * c * 6),
        name="layer_bf16",
    )(w)


def _layer_weights(l, w_in, w_uq, w_uk, w_uv, w_pool, w_out):
    wuq = w_uq[l].reshape(Q_LORA, MLA_HEADS, MLA_NOPE + MLA_ROPE)
    wuq = jnp.pad(wuq, ((0, 0), (0, 0), (0, QK_PAD - MLA_NOPE - MLA_ROPE)))
    wuk = w_uk[l].transpose(1, 0, 2).astype(BF16)
    wuv = w_uv[l].transpose(1, 0, 2).astype(BF16)
    return dict(
        w_in=_w_in_bf16(jnp.swapaxes(w_in, 1, 2), l, tc=256),
        wuq=wuq.transpose(1, 0, 2).astype(BF16),
        wuk=wuk,
        wuv=wuv,
        w_pool=w_pool[l].astype(BF16),
        w_out=_layer_bf16(w_out, l, tr=512),
    )


def kernel(x_prompt, x_sample, cache_ckv, cache_kpe, state_pool, cache_sb_k, cache_sb_v, g_ffn1, w1_gate, w1_up, w1_down, g_mix, w_in, g_qnorm, w_uq, g_kvnorm, w_uk, w_uv, w_pool, pool_scale, g_mla_out, g_sb_out, w_out, g_ffn2, w2_gate, w2_up, w2_down, g_final):
    mp, ms = P_BATCH * P_SEQ, S_BATCH * S_SEQ
    tm_p, tf = 512, 256
    row = lambda g: g.reshape(1, -1)
    gfin = row(g_final)

    rope_p = _rope_tables(jnp.tile(jnp.arange(P_SEQ), P_BATCH))
    rope_s = _rope_tables(PAST_LEN + jnp.tile(jnp.arange(S_SEQ), S_BATCH))
    hist_p = jnp.zeros((P_BATCH, POOL_HIST + 1, POOL_WIDTH), F32)
    cache_kpe_t = jnp.swapaxes(cache_kpe, 2, 3)
    cache_k = cache_sb_k.reshape(DEPTH, S_BATCH, PAST_LEN * SB_HEADS, SB_DIM)
    cache_v = cache_sb_v.reshape(DEPTH, S_BATCH, PAST_LEN * SB_HEADS, SB_DIM)

    hp = x_prompt.reshape(mp, D_MODEL)
    hs = x_sample.reshape(ms, D_MODEL)
    outs = {k: [] for k in ("p_proj", "p_ckv", "p_kpe", "p_pool", "s_proj", "s_ckv", "s_kpe", "s_pool")}

    def record(tag, batch, seq, proj, ckv, kpe):
        outs[tag + "_proj"].append(proj)
        outs[tag + "_ckv"].append(ckv.reshape(batch, seq, KV_LORA))
        outs[tag + "_kpe"].append(kpe[:, :MLA_ROPE].reshape(batch, seq, MLA_ROPE))
        outs[tag + "_pool"].append(
            proj.reshape(batch, seq, PROJ_W)[:, seq - POOL_HIST:, PROJ_U:PROJ_U + POOL_WIDTH])

    d_ff = w1_gate.shape[-1]
    ffn_steps = (mp // tm_p) * (d_ff // tf)
    next_cast = (ffn_steps * NEXT_CAST_ROWS == d_ff
                 and (D_MODEL // NEXT_CAST_ROWS) * NEXT_CAST_COL_SPLIT <= ffn_steps)
    w1 = None
    for l in range(DEPTH):
        w = _layer_weights(l, w_in, w_uq, w_uk, w_uv, w_pool, w_out)
        last = l == DEPTH - 1
        gq, gkv = row(g_qnorm[l]), row(g_kvnorm[l])
        pscale = row(pool_scale[l])
        hist_s = jnp.pad(state_pool[l], ((0, 0), (1, 0), (0, 0)))

        def mixer_inputs(h, tm, rope):
            proj = _inproj(h, row(g_mix[l]), w["w_in"], tm=tm, tn=PROJ_W // 5)
            return proj, _mla_prep(proj, *rope, gq, gkv, w["wuq"], w["wuk"], w["wuv"], tm=tm, hb=8)

        def mix(h, tm, o_mla, o_pool, o_sb):
            return _outproj(o_mla, o_pool, o_sb, h, row(g_mla_out[l]), row(g_sb_out[l]), w["w_out"], tm=tm, tn=1024)

        if w1 is None:
            hs, *w1 = _ffn(hs, row(g_ffn1[l]), w1_gate, w1_up, w1_down, gfin, tm=ms, tf=tf // 2,
                           final_norm=False, cast_layer=l)
        else:
            hs = _ffn(hs, row(g_ffn1[l]), *w1, gfin, tm=ms, tf=tf, final_norm=False)
        proj, (ckv, kpe, q, _, _) = mixer_inputs(hs, ms, rope_s)
        o_mla = _mla_sample(q, w["wuk"], w["wuv"], cache_ckv, cache_kpe_t, ckv, kpe, layer=l, row0=0, tk=2048)
        o_sb = _sb_sample(proj, cache_k, cache_v, layer=l, tk=1024, sub=256)
        o_pool = _pool(proj, hist_s, w["w_pool"], pscale, batch=S_BATCH, seq=S_SEQ, tt=S_SEQ, pos0=PAST_LEN)
        hs = mix(hs, ms, o_mla, o_pool, o_sb)
        record("s", S_BATCH, S_SEQ, proj, ckv, kpe)

        if next_cast:
            hp, *w2 = _ffn_next_cast(hp, row(g_ffn1[l]), *w1, gfin, (w2_gate, w2_up, w2_down), l,
                                     tm=tm_p, tf=tf, final_norm=False)
            hs = _ffn(hs, row(g_ffn2[l]), *w2, gfin, tm=ms, tf=tf, final_norm=last)
        else:
            hs, *w2 = _ffn(hs, row(g_ffn2[l]), w2_gate, w2_up, w2_down, gfin, tm=ms, tf=tf // 2,
                           final_norm=last, cast_layer=l)
            hp = _ffn(hp, row(g_ffn1[l]), *w1, gfin, tm=tm_p, tf=tf, final_norm=False)
        proj, (ckv, kpe, q, k, v) = mixer_inputs(hp, tm_p, rope_p)
        o_mla = _mla_prompt(q, k, v, batch=P_BATCH, seq=P_SEQ, tile=512, hp=4)
        o_sb = _sb_prompt(proj, batch=P_BATCH, seq=P_SEQ, tq=512, tk=256, hp=2)
        o_pool = _pool(proj, hist_p, w["w_pool"], pscale, batch=P_BATCH, seq=P_SEQ, tt=512, pos0=0)
        hp = mix(hp, tm_p, o_mla, o_pool, o_sb)
        if next_cast and not last:
            hp, *w1 = _ffn_next_cast(hp, row(g_ffn2[l]), *w2, gfin, (w1_gate, w1_up, w1_down), l + 1,
                                     tm=tm_p, tf=tf, final_norm=last)
        else:
            hp = _ffn(hp, row(g_ffn2[l]), *w2, gfin, tm=tm_p, tf=tf, final_norm=last)
            w1 = None
        record("p", P_BATCH, P_SEQ, proj, ckv, kpe)

    st = lambda k: jnp.stack(outs[k])
    heads = lambda a, batch, seq: a.reshape(DEPTH, batch, seq, SB_HEADS, SB_DIM)
    p_sbk, p_sbv = _sb_rows(outs["p_proj"], tt=tm_p)
    s_sbk, s_sbv = _sb_rows(outs["s_proj"], tt=ms)
    return (hp.reshape(P_BATCH, P_SEQ, D_MODEL), hs.reshape(S_BATCH, S_SEQ, D_MODEL),
            st("p_ckv"), st("p_kpe"), st("p_pool"), heads(p_sbk, P_BATCH, P_SEQ), heads(p_sbv, P_BATCH, P_SEQ),
            st("s_ckv"), st("s_kpe"), st("s_pool"), heads(s_sbk, S_BATCH, S_SEQ), heads(s_sbv, S_BATCH, S_SEQ))
```

```python
import functools
import math

import jax
import jax.numpy as jnp
from jax import lax
from jax.experimental import pallas as pl
from jax.experimental.pallas import tpu as pltpu

F32 = jnp.float32
BF16 = jnp.bfloat16

D_MODEL = 4096
D_FF = 11008
DEPTH = 2
P_BATCH, P_SEQ = 4, 2048
S_BATCH, S_SEQ, PAST_LEN = 16, 16, 4096
CHUNK = 64
MLA_HEADS, MLA_NOPE, MLA_ROPE, MLA_V = 16, 128, 64, 128
Q_LORA, KV_LORA = 1024, 512
ROPE_THETA = 10000.0
MLA_SCALE = 1.0 / math.sqrt(MLA_NOPE + MLA_ROPE)
POOL_WINDOWS = (2, 4, 8, 16)
POOL_GROUP = 256
POOL_WIDTH = 1024
POOL_HIST = 15
SB_HEADS, SB_DIM = 8, 128
SB_WIDTH = 1024
SB_SCALE = 1.0 / math.sqrt(SB_DIM)
EPS = 1e-6

V7X_LANES = 128
V7X_VMEM_LIMIT_BYTES = 60000 * 1024
VALUE_SPILL_BYTES = 8 * 1024 * 1024

PROJ_CQ = 0
PROJ_U = 1024
PROJ_SQ = 2048
PROJ_SK = 3072
PROJ_SV = 4096
PROJ_CKV = 5120
PROJ_KPE = 5632
PROJ_W = 5760
KPE_PAD = 128
QK_PAD = 256

ROW_CHUNK = 64
NEG_BIG = -1e30


def _cparams(semantics, vmem_bytes):
    assert vmem_bytes <= V7X_VMEM_LIMIT_BYTES, vmem_bytes
    limit = min(V7X_VMEM_LIMIT_BYTES, vmem_bytes + VALUE_SPILL_BYTES)
    return pltpu.CompilerParams(dimension_semantics=semantics, vmem_limit_bytes=int(limit))


def _rms(x, g):
    ms = jnp.mean(x * x, axis=-1, keepdims=True)
    return (x * lax.rsqrt(ms + EPS)) * g


def _dot(a, b):
    return jnp.dot(a, b, preferred_element_type=F32)


def _dot_nt(a, b):
    return lax.dot_general(a, b, (((1,), (1,)), ((), ())), preferred_element_type=F32)


def _row_loop(rows, fn):
    def body(c, carry):
        fn(pl.multiple_of(c * ROW_CHUNK, ROW_CHUNK))
        return carry
    lax.fori_loop(0, rows // ROW_CHUNK, body, 0)


def _ffn_step(x_ref, g_ref, wg_ref, wu_ref, wd_ref, gf_ref, o_ref, xn_ref, *, nj, rows, final_norm, side_work=None):
    j = pl.program_id(1)

    @pl.when(j == 0)
    def _():
        def prep(r):
            x = x_ref[pl.ds(r, ROW_CHUNK), :]
            xn_ref[pl.ds(r, ROW_CHUNK), :] = _rms(x, g_ref[...]).astype(BF16)
            o_ref[pl.ds(r, ROW_CHUNK), :] = x
        _row_loop(rows, prep)

    if side_work is not None:
        side_work()
    xn = xn_ref[...]
    gate = _dot(xn, wg_ref[...])
    up = _dot(xn, wu_ref[...])
    h = (0.5 * gate * (1.0 / (1.0 + jnp.exp(-gate))) * up).astype(BF16)
    o_ref[...] += _dot(h, wd_ref[...])

    if final_norm:
        @pl.when(j == nj - 1)
        def _():
            def fin(r):
                o_ref[pl.ds(r, ROW_CHUNK), :] = _rms(o_ref[pl.ds(r, ROW_CHUNK), :], gf_ref[...])
            _row_loop(rows, fin)


def _ffn_kernel(x_ref, g_ref, wg_ref, wu_ref, wd_ref, gf_ref, o_ref, xn_ref, **kw):
    _ffn_step(x_ref, g_ref, wg_ref, wu_ref, wd_ref, gf_ref, o_ref, xn_ref, **kw)


def _ffn_cast_kernel(x_ref, g_ref, wg_ref, wu_ref, wd_ref, gf_ref, o_ref, wg_o, wu_o, wd_o, xn_ref, **kw):
    for src, dst in ((wg_ref, wg_o), (wu_ref, wu_o), (wd_ref, wd_o)):
        dst[...] = src[0].astype(BF16)
    _ffn_step(x_ref, g_ref, wg_o, wu_o, wd_o, gf_ref, o_ref, xn_ref, **kw)


def _ffn_next_cast_kernel(x_ref, g_ref, wg_ref, wu_ref, wd_ref, gf_ref, ng_ref, nu_ref, nd_ref,
                          o_ref, og_ref, ou_ref, od_ref, xn_ref, **kw):
    def cast_slice():
        for src, dst in ((ng_ref, og_ref), (nu_ref, ou_ref), (nd_ref, od_ref)):
            dst[...] = src[0].astype(BF16)
    _ffn_step(x_ref, g_ref, wg_ref, wu_ref, wd_ref, gf_ref, o_ref, xn_ref, side_work=cast_slice, **kw)


def _ffn_vmem(tm, d, tf, weight_bytes):
    return (2 * tm * d * 4
            + tm * d * 2
            + 2 * tm * d * 4
            + 2 * 3 * d * tf * weight_bytes
            + 4 * tm * tf * 4)


def _ffn(x, g, wg, wu, wd, g_final, *, tm, tf, final_norm, cast_layer=None):
    m, d = x.shape
    cast = cast_layer is not None
    f = wg.shape[-1]
    assert m % tm == 0 and f % tf == 0 and tm % ROW_CHUNK == 0 and (not cast or m == tm)
    nj = f // tf
    vmem = _ffn_vmem(tm, d, tf, 6 if cast else 2)
    w_specs = [pl.BlockSpec((d, tf), lambda i, j: (0, j)),
               pl.BlockSpec((d, tf), lambda i, j: (0, j)),
               pl.BlockSpec((tf, d), lambda i, j: (j, 0))]
    w_in_specs = w_specs
    if cast:
        w_in_specs = [pl.BlockSpec((1, d, tf), lambda i, j: (cast_layer, 0, j)),
                      pl.BlockSpec((1, d, tf), lambda i, j: (cast_layer, 0, j)),
                      pl.BlockSpec((1, tf, d), lambda i, j: (cast_layer, j, 0))]
    x_out = jax.ShapeDtypeStruct((m, d), F32)
    x_spec = pl.BlockSpec((tm, d), lambda i, j: (i, 0))
    return pl.pallas_call(
        functools.partial(_ffn_cast_kernel if cast else _ffn_kernel, nj=nj, rows=tm, final_norm=final_norm),
        out_shape=(x_out,) + tuple(jax.ShapeDtypeStruct(w.shape[1:], BF16) for w in (wg, wu, wd)) if cast else x_out,
        grid=(m // tm, nj),
        in_specs=[
            pl.BlockSpec((tm, d), lambda i, j: (i, 0)),
            pl.BlockSpec((1, d), lambda i, j: (0, 0)),
            *w_in_specs,
            pl.BlockSpec((1, d), lambda i, j: (0, 0)),
        ],
        out_specs=(x_spec, *w_specs) if cast else x_spec,
        scratch_shapes=[pltpu.VMEM((tm, d), BF16)],
        compiler_params=_cparams(("parallel", "arbitrary"), vmem),
        name="ffn_cast" if cast else "ffn",
    )(x, g, wg, wu, wd, g_final)


NEXT_CAST_ROWS = 16
NEXT_CAST_COL_SPLIT = 2


def _ffn_next_cast(x, g, wg, wu, wd, g_final, next_w, next_layer, *, tm, tf, final_norm):
    m, d = x.shape
    f = wg.shape[-1]
    nj = f // tf
    steps = (m // tm) * nj
    r, split = NEXT_CAST_ROWS, NEXT_CAST_COL_SPLIT
    cw = f // split
    up_blocks = (d // r) * split
    assert m % tm == 0 and f % tf == 0 and tm % ROW_CHUNK == 0
    assert steps * r == f and up_blocks <= steps and cw % V7X_LANES == 0 and d % r == 0
    step = lambda i, j: i * nj + j
    up_idx = lambda i, j: jnp.minimum(step(i, j), up_blocks - 1)
    vmem = _ffn_vmem(tm, d, tf, 2) + 2 * (2 * r * cw + r * d) * 6
    w_specs = [pl.BlockSpec((d, tf), lambda i, j: (0, j)),
               pl.BlockSpec((d, tf), lambda i, j: (0, j)),
               pl.BlockSpec((tf, d), lambda i, j: (j, 0))]
    up_in = pl.BlockSpec((1, r, cw), lambda i, j: (next_layer, up_idx(i, j) // split, up_idx(i, j) % split))
    up_out = pl.BlockSpec((r, cw), lambda i, j: (up_idx(i, j) // split, up_idx(i, j) % split))
    x_spec = pl.BlockSpec((tm, d), lambda i, j: (i, 0))
    return pl.pallas_call(
        functools.partial(_ffn_next_cast_kernel, nj=nj, rows=tm, final_norm=final_norm),
        out_shape=(jax.ShapeDtypeStruct((m, d), F32),) + tuple(
            jax.ShapeDtypeStruct(w.shape[1:], BF16) for w in next_w),
        grid=(m // tm, nj),
        in_specs=[
            x_spec,
            pl.BlockSpec((1, d), lambda i, j: (0, 0)),
            *w_specs,
            pl.BlockSpec((1, d), lambda i, j: (0, 0)),
            up_in, up_in,
            pl.BlockSpec((1, r, d), lambda i, j: (next_layer, step(i, j), 0)),
        ],
        out_specs=(x_spec, up_out, up_out, pl.BlockSpec((r, d), lambda i, j: (step(i, j), 0))),
        scratch_shapes=[pltpu.VMEM((tm, d), BF16)],
        compiler_params=_cparams(("arbitrary", "arbitrary"), vmem),
        name="ffn_next_cast",
    )(x, g, wg, wu, wd, g_final, *next_w)


def _inproj_kernel(x_ref, g_ref, w_ref, o_ref, xn_ref, *, rows):
    @pl.when(pl.program_id(1) == 0)
    def _():
        def prep(r):
            xn_ref[pl.ds(r, ROW_CHUNK), :] = _rms(x_ref[pl.ds(r, ROW_CHUNK), :], g_ref[...]).astype(BF16)
        _row_loop(rows, prep)

    o_ref[...] = _dot_nt(xn_ref[...], w_ref[...])


def _inproj(x, g, w_t, *, tm, tn):
    m, d = x.shape
    n = w_t.shape[0]
    assert m % tm == 0 and n % tn == 0
    vmem = 2 * tm * d * 4 + tm * d * 2 + 2 * d * tn * 2 + 3 * tm * tn * 4
    return pl.pallas_call(
        functools.partial(_inproj_kernel, rows=tm),
        out_shape=jax.ShapeDtypeStruct((m, n), F32),
        grid=(m // tm, n // tn),
        in_specs=[
            pl.BlockSpec((tm, d), lambda i, j: (i, 0)),
            pl.BlockSpec((1, d), lambda i, j: (0, 0)),
            pl.BlockSpec((tn, d), lambda i, j: (j, 0)),
        ],
        out_specs=pl.BlockSpec((tm, tn), lambda i, j: (i, j)),
        scratch_shapes=[pltpu.VMEM((tm, d), BF16)],
        compiler_params=_cparams(("parallel", "arbitrary"), vmem),
        name="inproj",
    )(x, g, w_t)


def _rope_tile(t, c, sa, sb):
    return t * c + pltpu.roll(t, KPE_PAD - MLA_ROPE // 2, 1) * sa + pltpu.roll(t, MLA_ROPE // 2, 1) * sb


def _mla_prep_kernel(cq_ref, ckv_ref, kpe_ref, c_ref, sa_ref, sb_ref, gq_ref, gkv_ref,
                     wuq_ref, wuk_ref, wuv_ref,
                     ckv_o, kpe_o, q_o, k_o, v_o, cqn_s, ckv_s, kpe_s, *, rows, hb):
    @pl.when(pl.program_id(1) == 0)
    def _():
        def prep(r):
            sl = pl.ds(r, ROW_CHUNK)
            cqn_s[sl, :] = _rms(cq_ref[sl, :], gq_ref[...]).astype(BF16)
            ckv = _rms(ckv_ref[sl, :], gkv_ref[...])
            ckv_o[sl, :] = ckv
            ckv_s[sl, :] = ckv.astype(BF16)
            kpe = _rope_tile(kpe_ref[sl, :], c_ref[sl, :], sa_ref[sl, :], sb_ref[sl, :])
            kpe_o[sl, :] = kpe
            kpe_s[sl, :] = kpe.astype(BF16)
        _row_loop(rows, prep)

    for hh in range(hb):
        qf = _dot(cqn_s[...], wuq_ref[hh]) * MLA_SCALE
        q_o[hh, :, pl.ds(0, MLA_NOPE)] = qf[:, :MLA_NOPE].astype(BF16)
        q_o[hh, :, pl.ds(MLA_NOPE, KPE_PAD)] = _rope_tile(
            qf[:, MLA_NOPE:], c_ref[...], sa_ref[...], sb_ref[...]).astype(BF16)
        k_o[hh, :, pl.ds(0, MLA_NOPE)] = _dot(ckv_s[...], wuk_ref[hh]).astype(BF16)
        k_o[hh, :, pl.ds(MLA_NOPE, KPE_PAD)] = kpe_s[...]
        v_o[hh] = _dot(ckv_s[...], wuv_ref[hh]).astype(BF16)


def _mla_prep(proj, rope_c, rope_sa, rope_sb, gq, gkv, wuq, wuk, wuv, *, tm, hb):
    m = proj.shape[0]
    nh = wuq.shape[0]
    assert m % tm == 0 and nh % hb == 0
    vmem = (2 * tm * (Q_LORA + KV_LORA + 4 * KPE_PAD) * 4
            + 2 * hb * (Q_LORA * QK_PAD + 2 * KV_LORA * MLA_NOPE) * 2
            + 2 * tm * (KV_LORA + KPE_PAD) * 4
            + 2 * hb * tm * (2 * QK_PAD + MLA_V) * 2
            + tm * (Q_LORA + KV_LORA + KPE_PAD) * 2
            + 6 * tm * QK_PAD * 4)
    row_blk = lambda w, c: pl.BlockSpec((tm, w), lambda i, h: (i, c))
    head_w = lambda k, n: pl.BlockSpec((hb, k, n), lambda i, h: (h, 0, 0))
    head_o = lambda n: pl.BlockSpec((hb, tm, n), lambda i, h: (h, i, 0))
    return pl.pallas_call(
        functools.partial(_mla_prep_kernel, rows=tm, hb=hb),
        out_shape=(
            jax.ShapeDtypeStruct((m, KV_LORA), F32),
            jax.ShapeDtypeStruct((m, KPE_PAD), F32),
            jax.ShapeDtypeStruct((nh, m, QK_PAD), BF16),
            jax.ShapeDtypeStruct((nh, m, QK_PAD), BF16),
            jax.ShapeDtypeStruct((nh, m, MLA_V), BF16),
        ),
        grid=(m // tm, nh // hb),
        in_specs=[
            row_blk(Q_LORA, PROJ_CQ // Q_LORA),
            row_blk(KV_LORA, PROJ_CKV // KV_LORA),
            row_blk(KPE_PAD, PROJ_KPE // KPE_PAD),
            row_blk(KPE_PAD, 0), row_blk(KPE_PAD, 0), row_blk(KPE_PAD, 0),
            pl.BlockSpec((1, Q_LORA), lambda i, h: (0, 0)),
            pl.BlockSpec((1, KV_LORA), lambda i, h: (0, 0)),
            head_w(Q_LORA, QK_PAD), head_w(KV_LORA, MLA_NOPE), head_w(KV_LORA, MLA_V),
        ],
        out_specs=(row_blk(KV_LORA, 0), row_blk(KPE_PAD, 0), head_o(QK_PAD), head_o(QK_PAD), head_o(MLA_V)),
        scratch_shapes=[pltpu.VMEM((tm, Q_LORA), BF16), pltpu.VMEM((tm, KV_LORA), BF16),
                        pltpu.VMEM((tm, KPE_PAD), BF16)],
        compiler_params=_cparams(("parallel", "arbitrary"), vmem),
        name="mla_prep",
    )(proj, proj, proj, rope_c, rope_sa, rope_sb, gq, gkv, wuq, wuk, wuv)


def _softmax_step(s, v, m, l, acc):
    m_new = jnp.maximum(m, jnp.max(s, axis=-1, keepdims=True))
    p = jnp.exp(s - m_new)
    alpha = jnp.exp(m - m_new)
    l = alpha * l + jnp.sum(p, axis=-1, keepdims=True)
    acc = alpha * acc + _dot(p.astype(BF16), v)
    return m_new, l, acc


def _mla_prompt_kernel(q_ref, k_ref, v_ref, o_ref, *, seq, tile, hp):
    def q_tile(qi, carry):
        q0 = pl.multiple_of(qi * tile, tile)
        qs = [q_ref[h, pl.ds(q0, tile), :] for h in range(hp)]

        def scores(h, k0):
            return _dot_nt(qs[h], k_ref[h, pl.ds(k0, tile), :])

        def k_tile(ki, sts):
            k0 = pl.multiple_of(ki * tile, tile)
            return tuple(_softmax_step(scores(h, k0), v_ref[h, pl.ds(k0, tile), :], *sts[h]) for h in range(hp))

        init = (jnp.full((tile, 1), NEG_BIG, F32), jnp.zeros((tile, 1), F32), jnp.zeros((tile, MLA_V), F32))
        sts = lax.fori_loop(0, qi, k_tile, (init,) * hp)
        row = lax.broadcasted_iota(jnp.int32, (tile, tile), 0)
        col = lax.broadcasted_iota(jnp.int32, (tile, tile), 1)
        visible = col // CHUNK <= row // CHUNK
        for h in range(hp):
            s = jnp.where(visible, scores(h, q0), NEG_BIG)
            _, l, acc = _softmax_step(s, v_ref[h, pl.ds(q0, tile), :], *sts[h])
            o_ref[pl.ds(q0, tile), pl.ds(h * MLA_V, MLA_V)] = acc / l
        return carry

    lax.fori_loop(0, seq // tile, q_tile, 0)


def _mla_prompt(q, k, v, *, batch, seq, tile, hp):
    nh = q.shape[0]
    assert seq % tile == 0 and tile % CHUNK == 0 and nh % hp == 0
    vmem = hp * (2 * seq * (2 * QK_PAD + MLA_V) * 2 + 2 * seq * MLA_V * 4 + 4 * tile * tile * 4)
    return pl.pallas_call(
        functools.partial(_mla_prompt_kernel, seq=seq, tile=tile, hp=hp),
        out_shape=jax.ShapeDtypeStruct((batch * seq, nh * MLA_V), F32),
        grid=(batch, nh // hp),
        in_specs=[
            pl.BlockSpec((hp, seq, QK_PAD), lambda b, h: (h, b, 0)),
            pl.BlockSpec((hp, seq, QK_PAD), lambda b, h: (h, b, 0)),
            pl.BlockSpec((hp, seq, MLA_V), lambda b, h: (h, b, 0)),
        ],
        out_specs=pl.BlockSpec((seq, hp * MLA_V), lambda b, h: (b, h)),
        compiler_params=_cparams(("parallel", "parallel"), vmem),
        name="mla_prompt",
    )(q, k, v)


def _mla_sample_kernel(q_ref, wuk_ref, wuv_ref, ckv_ref, kpe_ref, ckvn_ref, kpen_ref, o_ref,
                       qlat_s, qpe_s, m_s, l_s, acc_s, *, nkt, nh, tq):
    kt = pl.program_id(1)
    rows = nh * tq

    @pl.when(kt == 0)
    def _():
        for h in range(nh):
            qh = q_ref[h]
            qlat_s[pl.ds(h * tq, tq), :] = _dot_nt(qh[:, :MLA_NOPE], wuk_ref[h]).astype(BF16)
            qpe_s[pl.ds(h * tq, tq), :] = qh[:, MLA_NOPE:MLA_NOPE + KPE_PAD]
        m_s[...] = jnp.full((rows, 1), NEG_BIG, F32)
        l_s[...] = jnp.zeros((rows, 1), F32)
        acc_s[...] = jnp.zeros((rows, KV_LORA), F32)

    def update(ckv, rope_scores):
        s = _dot_nt(qlat_s[...], ckv) + rope_scores
        m, l, acc = _softmax_step(s, ckv, m_s[...], l_s[...], acc_s[...])
        m_s[...] = m
        l_s[...] = l
        acc_s[...] = acc

    qpe = qpe_s[:, pl.ds(0, MLA_ROPE)]
    update(ckv_ref[0, 0].astype(BF16), _dot(qpe, kpe_ref[0, 0].astype(BF16)))

    @pl.when(kt == nkt - 1)
    def _():
        update(ckvn_ref[...].astype(BF16), _dot_nt(qpe, kpen_ref[:, pl.ds(0, MLA_ROPE)].astype(BF16)))
        o_lat = (acc_s[...] / l_s[...]).astype(BF16)
        for h in range(nh):
            o_ref[:, pl.ds(h * MLA_V, MLA_V)] = _dot(o_lat[h * tq:(h + 1) * tq], wuv_ref[h])


def _mla_sample(q, wuk, wuv, cache_ckv, cache_kpe_t, ckv_new, kpe_new, *, layer, row0, tk):
    nh = q.shape[0]
    _, batch, past, _ = cache_ckv.shape
    tq = S_SEQ
    assert past % tk == 0 and row0 % tq == 0
    assert (PAST_LEN + S_SEQ - 1) // CHUNK == PAST_LEN // CHUNK
    nkt = past // tk
    rows = nh * tq
    vmem = (2 * nh * tq * QK_PAD * 2 + 4 * nh * KV_LORA * MLA_NOPE * 2
            + 2 * tk * (KV_LORA + KPE_PAD) * 4 + tk * (KV_LORA + KPE_PAD) * 2
            + 2 * tq * (KV_LORA + KPE_PAD) * 4 + 2 * tq * nh * MLA_V * 4
            + rows * (KV_LORA + KPE_PAD) * 2 + rows * (KV_LORA + 2 * V7X_LANES) * 4
            + 4 * rows * tk * 4)
    rb = row0 // tq
    return pl.pallas_call(
        functools.partial(_mla_sample_kernel, nkt=nkt, nh=nh, tq=tq),
        out_shape=jax.ShapeDtypeStruct((batch * tq, nh * MLA_V), F32),
        grid=(batch, nkt),
        in_specs=[
            pl.BlockSpec((nh, tq, QK_PAD), lambda b, t: (0, rb + b, 0)),
            pl.BlockSpec((nh, KV_LORA, MLA_NOPE), lambda b, t: (0, 0, 0)),
            pl.BlockSpec((nh, KV_LORA, MLA_V), lambda b, t: (0, 0, 0)),
            pl.BlockSpec((1, 1, tk, KV_LORA), lambda b, t: (layer, b, t, 0)),
            pl.BlockSpec((1, 1, MLA_ROPE, tk), lambda b, t: (layer, b, 0, t)),
            pl.BlockSpec((tq, KV_LORA), lambda b, t: (rb + b, 0)),
            pl.BlockSpec((tq, KPE_PAD), lambda b, t: (rb + b, 0)),
        ],
        out_specs=pl.BlockSpec((tq, nh * MLA_V), lambda b, t: (b, 0)),
        scratch_shapes=[pltpu.VMEM((rows, KV_LORA), BF16), pltpu.VMEM((rows, KPE_PAD), BF16),
                        pltpu.VMEM((rows, 1), F32), pltpu.VMEM((rows, 1), F32),
                        pltpu.VMEM((rows, KV_LORA), F32)],
        compiler_params=_cparams(("parallel", "arbitrary"), vmem),
        name="mla_sample",
    )(q, wuk, wuv, cache_ckv, cache_kpe_t, ckv_new, kpe_new)


def _softplus(z):
    return jnp.maximum(z, 0.0) + jnp.log(1.0 + jnp.exp(-jnp.abs(z)))


def _later_keys_matrix(n):
    row = lax.broadcasted_iota(jnp.int32, (n, n), 0)
    col = lax.broadcasted_iota(jnp.int32, (n, n), 1)
    return (row > col).astype(BF16)


def _later_sum(x, u):
    r = x.shape[0]
    hi = x.astype(BF16)
    lo = (x - hi.astype(F32)).astype(BF16)
    e = _dot(jnp.concatenate([hi, lo], axis=0), u)
    return e[:r] + e[r:]


def _sb_weights(z, u, decay, mask=None):
    sp = _softplus(z)
    d = sp if mask is None else jnp.where(mask, sp, 0.0)
    a = jnp.exp(z - sp - _later_sum(d, u) - decay)
    if mask is not None:
        a = jnp.where(mask, a, 0.0)
    return a.astype(BF16), decay + jnp.sum(d, axis=-1, keepdims=True)


def _sb_prompt_kernel(q_ref, k_ref, v_ref, o_ref, qb, kb, vb, *, seq, tq, tk, hp):
    heads = range(hp)
    hcol = lambda h: pl.ds(h * SB_DIM, SB_DIM)

    def cast(r):
        sl = pl.ds(r, ROW_CHUNK)
        qb[sl, :] = (q_ref[sl, :] * SB_SCALE).astype(BF16)
        kb[sl, :] = k_ref[sl, :].astype(BF16)
        vb[sl, :] = v_ref[sl, :].astype(BF16)
    _row_loop(seq, cast)

    u = _later_keys_matrix(tk)
    per_q = tq // tk

    def q_tile(qi, carry):
        q0 = pl.multiple_of(qi * tq, tq)
        qs = [qb[pl.ds(q0, tq), hcol(h)] for h in heads]

        def key_tile(k0, sts, mask=None):
            out = []
            for h in heads:
                decay, acc = sts[h]
                a, decay = _sb_weights(_dot_nt(qs[h], kb[pl.ds(k0, tk), hcol(h)]), u, decay, mask)
                out.append((decay, acc + _dot(a, vb[pl.ds(k0, tk), hcol(h)])))
            return tuple(out)

        sts = ((jnp.zeros((tq, 1), F32), jnp.zeros((tq, SB_DIM), F32)),) * hp
        for d in reversed(range(per_q)):
            row = lax.broadcasted_iota(jnp.int32, (tq, tk), 0)
            col = lax.broadcasted_iota(jnp.int32, (tq, tk), 1) + d * tk
            sts = key_tile(pl.multiple_of(q0 + d * tk, tk), sts, mask=col < row)

        def older(step, sts):
            for d in reversed(range(per_q)):
                sts = key_tile(pl.multiple_of(q0 - (step + 1) * tq + d * tk, tk), sts)
            return sts

        sts = lax.fori_loop(0, qi, older, sts)
        for h in heads:
            o_ref[pl.ds(q0, tq), hcol(h)] = sts[h][1]
        return carry

    lax.fori_loop(0, seq // tq, q_tile, 0)


def _sb_prompt(proj, *, batch, seq, tq, tk, hp):
    assert seq % tq == 0 and tq % tk == 0 and seq % ROW_CHUNK == 0 and SB_HEADS % hp == 0
    w = hp * SB_DIM
    vmem = hp * (2 * 3 * seq * SB_DIM * 4 + 2 * seq * SB_DIM * 4 + 3 * seq * SB_DIM * 2 + 12 * tq * tk * 4)
    col = lambda c0: pl.BlockSpec((seq, w), lambda b, h: (b, c0 // w + h))
    return pl.pallas_call(
        functools.partial(_sb_prompt_kernel, seq=seq, tq=tq, tk=tk, hp=hp),
        out_shape=jax.ShapeDtypeStruct((batch * seq, SB_WIDTH), F32),
        grid=(batch, SB_HEADS // hp),
        in_specs=[col(PROJ_SQ), col(PROJ_SK), col(PROJ_SV)],
        out_specs=pl.BlockSpec((seq, w), lambda b, h: (b, h)),
        scratch_shapes=[pltpu.VMEM((seq, w), BF16)] * 3,
        compiler_params=_cparams(("parallel", "parallel"), vmem),
        name="sb_prompt",
    )(proj, proj, proj)


def _sb_sample_kernel(q_ref, kn_ref, vn_ref, kc_ref, vc_ref, o_ref, decay_s, acc_s, *, nkt, sub, tq):
    step = pl.program_id(1)
    rows = SB_HEADS * tq
    head = lambda h: pl.ds(h * SB_DIM, SB_DIM)
    hrow = lambda h: pl.ds(h * tq, tq)

    def scores(k_of_head):
        return jnp.concatenate(
            [_dot_nt((q_ref[:, head(h)] * SB_SCALE).astype(BF16), k_of_head(h)) for h in range(SB_HEADS)], axis=0)

    def weighted(a, v_of_head):
        return jnp.concatenate(
            [_dot(a[h * tq:(h + 1) * tq], v_of_head(h)) for h in range(SB_HEADS)], axis=0)

    @pl.when(step == 0)
    def _():
        pad = jnp.zeros((V7X_LANES - tq, SB_DIM), BF16)
        z = scores(lambda h: jnp.concatenate([kn_ref[:, head(h)].astype(BF16), pad], axis=0))
        t = lax.broadcasted_iota(jnp.int32, (rows, V7X_LANES), 0) % tq
        s = lax.broadcasted_iota(jnp.int32, (rows, V7X_LANES), 1)
        a, decay = _sb_weights(z, _later_keys_matrix(V7X_LANES), jnp.zeros((rows, 1), F32), mask=s < t)
        acc_s[...] = weighted(a, lambda h: jnp.concatenate([vn_ref[:, head(h)].astype(BF16), pad], axis=0))
        decay_s[...] = decay

    @pl.when(step > 0)
    def _():
        u = _later_keys_matrix(sub)
        decay = decay_s[...]
        acc = acc_s[...]
        for c in reversed(range(kc_ref.shape[2] // (sub * SB_HEADS))):
            keys = lambda h: pl.ds(c * sub * SB_HEADS + h, sub, stride=SB_HEADS)
            a, decay = _sb_weights(scores(lambda h: kc_ref[0, 0, keys(h), :].astype(BF16)), u, decay)
            acc = acc + weighted(a, lambda h: vc_ref[0, 0, keys(h), :].astype(BF16))
        decay_s[...] = decay
        acc_s[...] = acc

    @pl.when(step == nkt)
    def _():
        for h in range(SB_HEADS):
            o_ref[:, head(h)] = acc_s[hrow(h), :]


def _sb_sample(proj, cache_k, cache_v, *, layer, tk, sub):
    _, batch, past_rows, _ = cache_k.shape
    past = past_rows // SB_HEADS
    tq = S_SEQ
    assert past % tk == 0 and tk % sub == 0
    nkt = past // tk
    rows = SB_HEADS * tq
    vmem = (2 * 3 * tq * SB_WIDTH * 4 + 2 * 2 * tk * SB_WIDTH * 4 + 2 * tq * SB_WIDTH * 4
            + rows * (SB_DIM + V7X_LANES) * 4 + 2 * sub * SB_WIDTH * 2 + 16 * rows * sub * 4 + sub * sub * 2)
    newest_first = lambda b, s: (layer, b, nkt - jnp.maximum(s, 1), 0)
    col = lambda c0: pl.BlockSpec((tq, SB_WIDTH), lambda b, s: (b, c0 // SB_WIDTH))
    return pl.pallas_call(
        functools.partial(_sb_sample_kernel, nkt=nkt, sub=sub, tq=tq),
        out_shape=jax.ShapeDtypeStruct((batch * tq, SB_WIDTH), F32),
        grid=(batch, nkt + 1),
        in_specs=[col(PROJ_SQ), col(PROJ_SK), col(PROJ_SV),
                  pl.BlockSpec((1, 1, tk * SB_HEADS, SB_DIM), newest_first),
                  pl.BlockSpec((1, 1, tk * SB_HEADS, SB_DIM), newest_first)],
        out_specs=pl.BlockSpec((tq, SB_WIDTH), lambda b, s: (b, 0)),
        scratch_shapes=[pltpu.VMEM((rows, 1), F32), pltpu.VMEM((rows, SB_DIM), F32)],
        compiler_params=_cparams(("parallel", "arbitrary"), vmem),
        name="sb_sample",
    )(proj, proj, proj, cache_k, cache_v)


def _sb_rows_kernel(*refs, tt, layers):
    srcs, (ko_ref, vo_ref) = refs[:2 * layers], refs[2 * layers:]
    for l in range(layers):
        @pl.when(pl.program_id(0) == l)
        def _():
            for src, dst in ((srcs[2 * l], ko_ref), (srcs[2 * l + 1], vo_ref)):
                for h in range(SB_HEADS):
                    dst[0, pl.ds(h, tt, stride=SB_HEADS), :] = src[:, pl.ds(h * SB_DIM, SB_DIM)]


def _sb_rows(projs, *, tt):
    layers = len(projs)
    m = projs[0].shape[0]
    assert m % tt == 0
    nt = m // tt
    vmem = (2 * layers + 2) * 2 * tt * SB_WIDTH * 4

    def col(l, c0):
        return pl.BlockSpec((tt, SB_WIDTH), lambda g, i: (jnp.clip(i + (g - l) * nt, 0, nt - 1), c0 // SB_WIDTH))

    out = jax.ShapeDtypeStruct((layers, m * SB_HEADS, SB_DIM), F32)
    return pl.pallas_call(
        functools.partial(_sb_rows_kernel, tt=tt, layers=layers),
        out_shape=(out, out),
        grid=(layers, nt),
        in_specs=[col(l, c0) for l in range(layers) for c0 in (PROJ_SK, PROJ_SV)],
        out_specs=(pl.BlockSpec((1, tt * SB_HEADS, SB_DIM), lambda g, i: (g, i, 0)),) * 2,
        compiler_params=_cparams(("arbitrary", "arbitrary"), vmem),
        name="sb_rows",
    )(*[p for p in projs for _ in range(2)])


def _pool_kernel(u_ref, hist_ref, w_ref, scale_ref, o_ref, carry_s, *, tt, pos0):
    ti = pl.program_id(1)
    halo = carry_s.shape[0]

    @pl.when(ti == 0)
    def _():
        carry_s[...] = hist_ref[0]

    pos = pos0 + ti * tt + lax.broadcasted_iota(jnp.int32, (tt, 1), 0)
    for g, w in enumerate(POOL_WINDOWS):
        cols = pl.ds(g * POOL_GROUP, POOL_GROUP)
        tok = u_ref[:, cols]
        s = jnp.concatenate([carry_s[:, cols], tok], axis=0)
        shift = 1
        while shift < w:
            s = s + pltpu.roll(s, shift, 0)
            shift *= 2
        cnt = jnp.minimum(pos + 1, w).astype(F32)
        pooled = s[halo:] / cnt - tok
        o_ref[:, cols] = (_dot(pooled.astype(BF16), w_ref[g]) * scale_ref[:, cols]).astype(BF16)
    carry_s[...] = u_ref[pl.ds(tt - halo, halo), :]


def _pool(proj, hist, w_pool, scale, *, batch, seq, tt, pos0):
    halo = hist.shape[1]
    assert seq % tt == 0 and tt >= halo and halo > max(POOL_WINDOWS) - 1
    nt = seq // tt
    vmem = (2 * tt * POOL_WIDTH * 4 + 2 * halo * POOL_WIDTH * 4 + 2 * POOL_WIDTH * POOL_GROUP * 2
            + 2 * tt * POOL_WIDTH * 2 + halo * POOL_WIDTH * 4 + 8 * (tt + halo) * POOL_GROUP * 4)
    return pl.pallas_call(
        functools.partial(_pool_kernel, tt=tt, pos0=pos0),
        out_shape=jax.ShapeDtypeStruct((batch * seq, POOL_WIDTH), BF16),
        grid=(batch, nt),
        in_specs=[
            pl.BlockSpec((tt, POOL_WIDTH), lambda b, t: (b * nt + t, PROJ_U // POOL_WIDTH)),
            pl.BlockSpec((1, halo, POOL_WIDTH), lambda b, t: (b, 0, 0)),
            pl.BlockSpec((len(POOL_WINDOWS), POOL_GROUP, POOL_GROUP), lambda b, t: (0, 0, 0)),
            pl.BlockSpec((1, POOL_WIDTH), lambda b, t: (0, 0)),
        ],
        out_specs=pl.BlockSpec((tt, POOL_WIDTH), lambda b, t: (b * nt + t, 0)),
        scratch_shapes=[pltpu.VMEM((halo, POOL_WIDTH), F32)],
        compiler_params=_cparams(("parallel", "arbitrary"), vmem),
        name="pool",
    )(proj, hist, w_pool, scale)


def _outproj_kernel(mla_ref, pool_ref, sb_ref, x_ref, gm_ref, gs_ref, w_ref, o_ref, mix_s, *, rows):
    wm = mla_ref.shape[1]
    wp = pool_ref.shape[1]

    @pl.when(pl.program_id(1) == 0)
    def _():
        def prep(r):
            sl = pl.ds(r, ROW_CHUNK)
            mix_s[sl, pl.ds(0, wm)] = _rms(mla_ref[sl, :], gm_ref[...]).astype(BF16)
            mix_s[sl, pl.ds(wm, wp)] = pool_ref[sl, :]
            mix_s[sl, pl.ds(wm + wp, sb_ref.shape[1])] = _rms(sb_ref[sl, :], gs_ref[...]).astype(BF16)
        _row_loop(rows, prep)

    o_ref[...] = x_ref[...] + _dot(mix_s[...], w_ref[...])


def _outproj(o_mla, o_pool, o_sb, x, g_mla, g_sb, w, *, tm, tn):
    m, d = x.shape
    wm, wp, ws = o_mla.shape[1], o_pool.shape[1], o_sb.shape[1]
    k = wm + wp + ws
    assert m % tm == 0 and d % tn == 0 and w.shape == (k, d)
    vmem = (2 * tm * (wm + ws) * 4 + 2 * tm * wp * 2 + 2 * 2 * tm * tn * 4 + 2 * k * tn * 2
            + tm * k * 2 + 2 * tm * tn * 4)
    full = lambda n: pl.BlockSpec((tm, n), lambda i, j: (i, 0))
    return pl.pallas_call(
        functools.partial(_outproj_kernel, rows=tm),
        out_shape=jax.ShapeDtypeStruct((m, d), F32),
        grid=(m // tm, d // tn),
        in_specs=[full(wm), full(wp), full(ws),
                  pl.BlockSpec((tm, tn), lambda i, j: (i, j)),
                  pl.BlockSpec((1, wm), lambda i, j: (0, 0)),
                  pl.BlockSpec((1, ws), lambda i, j: (0, 0)),
                  pl.BlockSpec((k, tn), lambda i, j: (0, j))],
        out_specs=pl.BlockSpec((tm, tn), lambda i, j: (i, j)),
        scratch_shapes=[pltpu.VMEM((tm, k), BF16)],
        compiler_params=_cparams(("parallel", "arbitrary"), vmem),
        name="outproj",
    )(o_mla, o_pool, o_sb, x, g_mla, g_sb, w)


def _rope_tables(pos):
    half = MLA_ROPE // 2
    inv = 1.0 / (ROPE_THETA ** (jnp.arange(half, dtype=F32) / half))
    ang = pos.astype(F32)[:, None] * inv[None, :]
    cos, sin = jnp.cos(ang), jnp.sin(ang)
    z = lambda n: jnp.zeros((pos.shape[0], n), F32)
    return (jnp.concatenate([cos, cos, z(KPE_PAD - MLA_ROPE)], axis=1),
            jnp.concatenate([-sin, z(KPE_PAD - half)], axis=1),
            jnp.concatenate([z(half), sin, z(KPE_PAD - MLA_ROPE)], axis=1))


W_IN_PIECES = (
    (0, Q_LORA, PROJ_CQ),
    (Q_LORA, KV_LORA, PROJ_CKV),
    (Q_LORA + KV_LORA, MLA_ROPE, PROJ_KPE),
    (Q_LORA + KV_LORA + MLA_ROPE, POOL_WIDTH, PROJ_U),
    (Q_LORA + KV_LORA + MLA_ROPE + POOL_WIDTH, SB_WIDTH, PROJ_SQ),
    (Q_LORA + KV_LORA + MLA_ROPE + POOL_WIDTH + SB_WIDTH, SB_WIDTH, PROJ_SK),
    (Q_LORA + KV_LORA + MLA_ROPE + POOL_WIDTH + 2 * SB_WIDTH, SB_WIDTH, PROJ_SV),
)


def _w_in_kernel(w_ref, o_ref):
    cols = o_ref.shape[1]
    o_ref[pl.ds(PROJ_KPE, KPE_PAD), :] = jnp.zeros((KPE_PAD, cols), BF16)
    for src, width, dst in W_IN_PIECES:
        o_ref[pl.ds(dst, width), :] = w_ref[0, pl.ds(src, width), :].astype(BF16)


def _w_in_bf16(w_in_t, l, *, tc):
    _, n, d = w_in_t.shape
    assert d % tc == 0
    vmem = 2 * n * tc * 4 + 2 * PROJ_W * tc * 2 + PROJ_W * tc * 4
    return pl.pallas_call(
        _w_in_kernel,
        out_shape=jax.ShapeDtypeStruct((PROJ_W, d), BF16),
        grid=(d // tc,),
        in_specs=[pl.BlockSpec((1, n, tc), lambda i: (l, 0, i))],
        out_specs=pl.BlockSpec((PROJ_W, tc), lambda i: (0, i)),
        compiler_params=_cparams(("parallel",), vmem),
        name="w_in_bf16",
    )(w_in_t)


def _cast_kernel(w_ref, o_ref):
    o_ref[...] = w_ref[0].astype(BF16)


def _layer_bf16(w, l, *, tr):
    _, r, c = w.shape
    assert r % tr == 0
    return pl.pallas_call(
        _cast_kernel,
        out_shape=jax.ShapeDtypeStruct((r, c), BF16),
        grid=(r // tr,),
        in_specs=[pl.BlockSpec((1, tr, c), lambda i: (l, i, 0))],
        out_specs=pl.BlockSpec((tr, c), lambda i: (i, 0)),
        compiler_params=_cparams(("parallel",), 2 * tr * c * 6),
        name="layer_bf16",
    )(w)


def _layer_weights(l, w_in, w_uq, w_uk, w_uv, w_pool, w_out):
    wuq = w_uq[l].reshape(Q_LORA, MLA_HEADS, MLA_NOPE + MLA_ROPE)
    wuq = jnp.pad(wuq, ((0, 0), (0, 0), (0, QK_PAD - MLA_NOPE - MLA_ROPE)))
    wuk = w_uk[l].transpose(1, 0, 2).astype(BF16)
    wuv = w_uv[l].transpose(1, 0, 2).astype(BF16)
    return dict(
        w_in=_w_in_bf16(jnp.swapaxes(w_in, 1, 2), l, tc=256),
        wuq=wuq.transpose(1, 0, 2).astype(BF16),
        wuk=wuk,
        wuv=wuv,
        w_pool=w_pool[l].astype(BF16),
        w_out=_layer_bf16(w_out, l, tr=512),
    )


def kernel(x_prompt, x_sample, cache_ckv, cache_kpe, state_pool, cache_sb_k, cache_sb_v, g_ffn1, w1_gate, w1_up, w1_down, g_mix, w_in, g_qnorm, w_uq, g_kvnorm, w_uk, w_uv, w_pool, pool_scale, g_mla_out, g_sb_out, w_out, g_ffn2, w2_gate, w2_up, w2_down, g_final):
    mp, ms = P_BATCH * P_SEQ, S_BATCH * S_SEQ
    tm_p, tf = 512, 256
    row = lambda g: g.reshape(1, -1)
    gfin = row(g_final)

    rope_p = _rope_tables(jnp.tile(jnp.arange(P_SEQ), P_BATCH))
    rope_s = _rope_tables(PAST_LEN + jnp.tile(jnp.arange(S_SEQ), S_BATCH))
    hist_p = jnp.zeros((P_BATCH, POOL_HIST + 1, POOL_WIDTH), F32)
    cache_kpe_t = jnp.swapaxes(cache_kpe, 2, 3)
    cache_k = cache_sb_k.reshape(DEPTH, S_BATCH, PAST_LEN * SB_HEADS, SB_DIM)
    cache_v = cache_sb_v.reshape(DEPTH, S_BATCH, PAST_LEN * SB_HEADS, SB_DIM)

    hp = x_prompt.reshape(mp, D_MODEL)
    hs = x_sample.reshape(ms, D_MODEL)
    outs = {k: [] for k in ("p_proj", "p_ckv", "p_kpe", "p_pool", "s_proj", "s_ckv", "s_kpe", "s_pool")}

    def record(tag, batch, seq, proj, ckv, kpe):
        outs[tag + "_proj"].append(proj)
        outs[tag + "_ckv"].append(ckv.reshape(batch, seq, KV_LORA))
        outs[tag + "_kpe"].append(kpe[:, :MLA_ROPE].reshape(batch, seq, MLA_ROPE))
        outs[tag + "_pool"].append(
            proj.reshape(batch, seq, PROJ_W)[:, seq - POOL_HIST:, PROJ_U:PROJ_U + POOL_WIDTH])

    d_ff = w1_gate.shape[-1]
    ffn_steps = (mp // tm_p) * (d_ff // tf)
    next_cast = (ffn_steps * NEXT_CAST_ROWS == d_ff
                 and (D_MODEL // NEXT_CAST_ROWS) * NEXT_CAST_COL_SPLIT <= ffn_steps)
    w1 = None
    for l in range(DEPTH):
        w = _layer_weights(l, w_in, w_uq, w_uk, w_uv, w_pool, w_out)
        last = l == DEPTH - 1
        gq, gkv = row(g_qnorm[l]), row(g_kvnorm[l])
        pscale = row(pool_scale[l])
        hist_s = jnp.pad(state_pool[l], ((0, 0), (1, 0), (0, 0)))

        def mixer_inputs(h, tm, rope):
            proj = _inproj(h, row(g_mix[l]), w["w_in"], tm=tm, tn=PROJ_W // 5)
            return proj, _mla_prep(proj, *rope, gq, gkv, w["wuq"], w["wuk"], w["wuv"], tm=tm, hb=8)

        def mix(h, tm, o_mla, o_pool, o_sb):
            return _outproj(o_mla, o_pool, o_sb, h, row(g_mla_out[l]), row(g_sb_out[l]), w["w_out"], tm=tm, tn=1024)

        if w1 is None:
            hs, *w1 = _ffn(hs, row(g_ffn1[l]), w1_gate, w1_up, w1_down, gfin, tm=ms, tf=tf // 2,
                           final_norm=False, cast_layer=l)
        else:
            hs = _ffn(hs, row(g_ffn1[l]), *w1, gfin, tm=ms, tf=tf, final_norm=False)
        proj, (ckv, kpe, q, _, _) = mixer_inputs(hs, ms, rope_s)
        o_mla = _mla_sample(q, w["wuk"], w["wuv"], cache_ckv, cache_kpe_t, ckv, kpe, layer=l, row0=0, tk=2048)
        o_sb = _sb_sample(proj, cache_k, cache_v, layer=l, tk=1024, sub=256)
        o_pool = _pool(proj, hist_s, w["w_pool"], pscale, batch=S_BATCH, seq=S_SEQ, tt=S_SEQ, pos0=PAST_LEN)
        hs = mix(hs, ms, o_mla, o_pool, o_sb)
        record("s", S_BATCH, S_SEQ, proj, ckv, kpe)

        if next_cast:
            hp, *w2 = _ffn_next_cast(hp, row(g_ffn1[l]), *w1, gfin, (w2_gate, w2_up, w2_down), l,
                                     tm=tm_p, tf=tf, final_norm=False)
            hs = _ffn(hs, row(g_ffn2[l]), *w2, gfin, tm=ms, tf=tf, final_norm=last)
        else:
            hs, *w2 = _ffn(hs, row(g_ffn2[l]), w2_gate, w2_up, w2_down, gfin, tm=ms, tf=tf // 2,
                           final_norm=last, cast_layer=l)
            hp = _ffn(hp, row(g_ffn1[l]), *w1, gfin, tm=tm_p, tf=tf, final_norm=False)
        proj, (ckv, kpe, q, k, v) = mixer_inputs(hp, tm_p, rope_p)
        o_mla = _mla_prompt(q, k, v, batch=P_BATCH, seq=P_SEQ, tile=512, hp=4)
        o_sb = _sb_prompt(proj, batch=P_BATCH, seq=P_SEQ, tq=512, tk=256, hp=2)
        o_pool = _pool(proj, hist_p, w["w_pool"], pscale, batch=P_BATCH, seq=P_SEQ, tt=512, pos0=0)
        hp = mix(hp, tm_p, o_mla, o_pool, o_sb)
        if next_cast and not last:
            hp, *w1 = _ffn_next_cast(hp, row(g_ffn2[l]), *w2, gfin, (w1_gate, w1_up, w1_down), l + 1,
                                     tm=tm_p, tf=tf, final_norm=last)
        else:
            hp = _ffn(hp, row(g_ffn2[l]), *w2, gfin, tm=tm_p, tf=tf, final_norm=last)
            w1 = None
        record("p", P_BATCH, P_SEQ, proj, ckv, kpe)

    st = lambda k: jnp.stack(outs[k])
    heads = lambda a, batch, seq: a.reshape(DEPTH, batch, seq, SB_HEADS, SB_DIM)
    p_sbk, p_sbv = _sb_rows(outs["p_proj"], tt=tm_p)
    s_sbk, s_sbv = _sb_rows(outs["s_proj"], tt=ms)
    return (hp.reshape(P_BATCH, P_SEQ, D_MODEL), hs.reshape(S_BATCH, S_SEQ, D_MODEL),
            st("p_ckv"), st("p_kpe"), st("p_pool"), heads(p_sbk, P_BATCH, P_SEQ), heads(p_sbv, P_BATCH, P_SEQ),
            st("s_ckv"), st("s_kpe"), st("s_pool"), heads(s_sbk, S_BATCH, S_SEQ), heads(s_sbv, S_BATCH, S_SEQ))
```
